```python
import math
import jax, jax.numpy as jnp
from jax import lax
import numpy as np

D_MODEL = 2048
BATCH = 4
SEQ = 2048
DEPTH = 2
DEC_BATCH = 32
DEC_SEQ = 1
PAST_LEN = 8192
PAGE_SIZE = 128

HEAD_DIM = 64
GROUP_WIDTH = D_MODEL // 4
NSA_HEADS = GROUP_WIDTH // HEAD_DIM
NSA_KV_HEADS = 2
NSA_GROUP = NSA_HEADS // NSA_KV_HEADS
CMP_STRIDE = 16
CMP_LEN = 2 * CMP_STRIDE
SEL_BLOCK = 64
SEL_TOPK = 16
WINDOW = 512
FORCE_BONUS = 1e4
SC_WIDTH = GROUP_WIDTH
CONV_W = 3
DIFF_HEADS = GROUP_WIDTH // (2 * HEAD_DIM)
DIFF_KV_HEADS = 2
DIFF_GROUP = DIFF_HEADS // DIFF_KV_HEADS
S5_WIDTH = GROUP_WIDTH
S5_CH = 16
S5_GROUPS = S5_WIDTH // S5_CH
S5_STATE = 64
D_FF = 256 * ((8 * D_MODEL // 3 + 255) // 256)
ROPE_THETA = 10000.0
QBLOCK = 128
LN_EPS = 1e-5
RMS_EPS = 1e-6
NEG_INF = -1e30
DN_ALPHA = (2 * DEPTH) ** 0.25
DN_BETA = (8 * DEPTH) ** -0.25

NSA_Q = NSA_HEADS * HEAD_DIM
NSA_KV = 2 * NSA_KV_HEADS * HEAD_DIM
DIFF_Q = DIFF_HEADS * 2 * HEAD_DIM
DIFF_KV = DIFF_KV_HEADS * 2 * HEAD_DIM
IN_SPLITS = (NSA_Q, NSA_KV, NSA_KV, NSA_KV, 3 * NSA_HEADS,
             SC_WIDTH, SC_WIDTH, SC_WIDTH,
             DIFF_Q, DIFF_KV, DIFF_KV,
             S5_WIDTH)
IN_COLS = sum(IN_SPLITS)
SPLIT_AT = tuple(sum(IN_SPLITS[:i + 1]) for i in range(len(IN_SPLITS) - 1))

kernel_name = 'hybrid_nsa_conv_diff_s5_step'


def layer_norm(x, g, b):
    xf = x.astype(jnp.float32)
    mu = jnp.mean(xf, axis=-1, keepdims=True)
    var = jnp.mean(jnp.square(xf - mu), axis=-1, keepdims=True)
    y = (xf - mu) * lax.rsqrt(var + LN_EPS) * g.astype(jnp.float32) + b.astype(jnp.float32)
    return y.astype(x.dtype)


def rms_unit(x):
    xf = x.astype(jnp.float32)
    return (xf * lax.rsqrt(jnp.mean(jnp.square(xf), axis=-1, keepdims=True) + RMS_EPS)).astype(x.dtype)


def rope(x, pos):
    half = x.shape[-1] // 2
    inv = ROPE_THETA ** (-jnp.arange(half, dtype=jnp.float32) / half)
    ang = pos.astype(jnp.float32)[:, None] * inv[None, :]
    shp = (1, x.shape[1]) + (1,) * (x.ndim - 3) + (half,)
    cos = jnp.cos(ang).reshape(shp)
    sin = jnp.sin(ang).reshape(shp)
    xf = x.astype(jnp.float32)
    x1, x2 = xf[..., :half], xf[..., half:]
    return jnp.concatenate([x1 * cos - x2 * sin, x2 * cos + x1 * sin], axis=-1).astype(x.dtype)


def masked_softmax(s, mask):
    s = jnp.where(mask, s, NEG_INF)
    m = jnp.max(s, axis=-1, keepdims=True)
    e = jnp.where(mask, jnp.exp(s - m), 0.0)
    return e / jnp.maximum(jnp.sum(e, axis=-1, keepdims=True), 1e-30)


def causal_dwconv(z, prev, w):
    t = z.shape[1]
    zz = jnp.concatenate([prev.astype(z.dtype), z], axis=1)
    y = w[0] * zz[:, 0:t]
    for i in range(1, CONV_W):
        y = y + w[i] * zz[:, i:i + t]
    return y, zz[:, t:]


def nsa_compress(k, pe, w):
    b, tk, h, dh = k.shape
    sub = k.reshape(b, tk // CMP_STRIDE, CMP_STRIDE, h, dh)
    w2 = w.reshape(2, CMP_STRIDE, dh, dh)
    first = sub[:, :-1] + pe[:CMP_STRIDE][None, None, :, None, :]
    second = sub[:, 1:] + pe[CMP_STRIDE:][None, None, :, None, :]
    return jnp.einsum('bnshd,sde->bhne', first, w2[0]) + jnp.einsum('bnshd,sde->bhne', second, w2[1])


def cmp_to_sel(n_cmp, n_sel):
    start = jnp.arange(n_cmp)[:, None] * CMP_STRIDE
    blk = jnp.arange(n_sel)[None, :] * SEL_BLOCK
    cover = jnp.minimum(start + CMP_LEN, blk + SEL_BLOCK) - jnp.maximum(start, blk)
    return jnp.clip(cover, 0, CMP_LEN).astype(jnp.float32) / CMP_LEN


def s5_mixer(u, h_re, h_im, a_re, a_im, log_dt, b_re, b_im, c_re, c_im, d_skip, glu_w, glu_b):
    f32 = jnp.float32
    bsz, t, _ = u.shape
    ar, ai = a_re.astype(f32), a_im.astype(f32)
    dt = jnp.exp(log_dt.astype(f32))[:, None]
    mag = jnp.exp(ar * dt)
    abar_re, abar_im = mag * jnp.cos(ai * dt), mag * jnp.sin(ai * dt)
    den = ar * ar + ai * ai
    nr, ni = abar_re - 1.0, abar_im
    coef_re = ((nr * ar + ni * ai) / den)[..., None]
    coef_im = ((ni * ar - nr * ai) / den)[..., None]
    br, bi = b_re.astype(f32), b_im.astype(f32)
    bbar_re = coef_re * br - coef_im * bi
    bbar_im = coef_re * bi + coef_im * br
    ug = u.astype(f32).reshape(bsz, t, S5_GROUPS, S5_CH)
    x_re = jnp.einsum('btgc,gpc->btgp', ug, bbar_re)
    x_im = jnp.einsum('btgc,gpc->btgp', ug, bbar_im)
    h0r, h0i = h_re.astype(f32), h_im.astype(f32)
    x_re = x_re.at[:, 0].add(abar_re * h0r - abar_im * h0i)
    x_im = x_im.at[:, 0].add(abar_re * h0i + abar_im * h0r)
    a_r = jnp.broadcast_to(abar_re, x_re.shape)
    a_i = jnp.broadcast_to(abar_im, x_im.shape)

    def combine(e1, e2):
        a1r, a1i, b1r, b1i = e1
        a2r, a2i, b2r, b2i = e2
        return (a1r * a2r - a1i * a2i, a1r * a2i + a1i * a2r,
                a2r * b1r - a2i * b1i + b2r, a2r * b1i + a2i * b1r + b2i)

    _, _, hr, hi = lax.associative_scan(combine, (a_r, a_i, x_re, x_im), axis=1)
    y = (jnp.einsum('btgp,gcp->btgc', hr, c_re.astype(f32))
         - jnp.einsum('btgp,gcp->btgc', hi, c_im.astype(f32)))
    y = y.reshape(bsz, t, S5_WIDTH) + d_skip.astype(f32) * u.astype(f32)
    y = jax.nn.gelu(y)
    y = y * jax.nn.sigmoid(y @ glu_w.astype(f32) + glu_b.astype(f32))
    return y.astype(u.dtype), hr[:, -1], hi[:, -1]


def hybrid_layer(x, past_len, past, params, layer_idx):
    (w_in, phi_pe, phi_w, sc_conv_w, diff_lambda, a_re, a_im, log_dt, b_re, b_im, c_re, c_im,
     d_skip, glu_w, glu_b, mix_gain, w_out, ln1_g, ln1_b, ffn_w_up, ffn_conv_w, ffn_w_down,
     ln2_g, ln2_b) = params
    (past_cmp, past_slc, past_win, past_diff, past_sc, past_s5r, past_s5i, past_ffn) = past
    f32 = jnp.float32
    bsz, t, _ = x.shape
    dh = HEAD_DIM
    tk = past_len + t
    pos = past_len + jnp.arange(t)
    qb = QBLOCK if t % QBLOCK == 0 else t
    nb = t // qb
    starts = jnp.arange(nb) * qb
    scale = dh ** -0.5

    def to_blocks(a):
        return jnp.moveaxis(a.reshape((bsz, nb, qb) + a.shape[2:]), 1, 0)

    def from_blocks(a):
        return jnp.moveaxis(a, 0, 1).reshape((bsz, t) + a.shape[3:])

    hcat = x @ w_in
    (nq, ncmp, nslc, nwin, ngate, sc_b, sc_c, sc_h, dq, dk, dv, s5_u) = jnp.split(hcat, SPLIT_AT, axis=-1)

    q_raw = nq.reshape(bsz, t, NSA_KV_HEADS, NSA_GROUP, dh)
    q_rot = rope(q_raw, pos)
    kv_cmp = ncmp.reshape(bsz, t, 2, NSA_KV_HEADS, dh)
    kv_slc = nslc.reshape(bsz, t, 2, NSA_KV_HEADS, dh)
    kv_slc = jnp.stack([rope(kv_slc[:, :, 0], pos), kv_slc[:, :, 1]], axis=2)
    kv_win = nwin.reshape(bsz, t, 2, NSA_KV_HEADS, dh)
    kv_win = jnp.stack([rope(kv_win[:, :, 0], pos), kv_win[:, :, 1]], axis=2)
    gates = jax.nn.sigmoid(ngate.astype(f32)).reshape(bsz, t, NSA_KV_HEADS, NSA_GROUP, 3)

    tk_pad = -(-tk // SEL_BLOCK) * SEL_BLOCK
    pad_end = ((0, 0), (0, tk_pad - tk), (0, 0), (0, 0), (0, 0))
    full_cmp = jnp.pad(jnp.concatenate([past_cmp.astype(x.dtype), kv_cmp], axis=1), pad_end)
    full_slc = jnp.pad(jnp.concatenate([past_slc.astype(x.dtype), kv_slc], axis=1), pad_end)
    kc = nsa_compress(full_cmp[:, :, 0], phi_pe[0], phi_w[0])
    vc = nsa_compress(full_cmp[:, :, 1], phi_pe[1], phi_w[1])
    n_cmp = kc.shape[2]
    n_sel = tk_pad // SEL_BLOCK
    n_top = min(SEL_TOPK, n_sel)
    cmp_last = jnp.arange(n_cmp) * CMP_STRIDE + CMP_LEN - 1
    cover = cmp_to_sel(n_cmp, n_sel)

    def sel_blocks(a):
        a = a.reshape(bsz, n_sel, SEL_BLOCK, NSA_KV_HEADS, dh).transpose(0, 3, 1, 2, 4)
        return a.reshape(bsz, NSA_KV_HEADS, n_sel, SEL_BLOCK * dh)

    ks_blk = sel_blocks(full_slc[:, :, 0])
    vs_blk = sel_blocks(full_slc[:, :, 1])
    win_all = jnp.concatenate([past_win.astype(x.dtype), kv_win], axis=1)
    win_pad = jnp.pad(win_all, ((0, 0), (WINDOW - past_win.shape[1], 0), (0, 0), (0, 0), (0, 0)))
    win_k = win_pad[:, :, 0].transpose(0, 2, 1, 3)
    win_v = win_pad[:, :, 1].transpose(0, 2, 1, 3)
    gather_blocks = jax.vmap(jax.vmap(lambda blk, idx: blk[idx]))

    def nsa_block(args):
        qr, qo, g, s0 = args
        qr = qr.transpose(0, 2, 3, 1, 4)
        qo = qo.transpose(0, 2, 3, 1, 4)
        g = g.transpose(0, 2, 3, 1, 4)
        qpos = past_len + s0 + jnp.arange(qb)
        s_c = jnp.einsum('bhgqd,bhnd->bhgqn', qr, kc, preferred_element_type=f32) * scale
        p_cmp = masked_softmax(s_c, cmp_last[None, :] <= qpos[:, None])
        o_cmp = jnp.einsum('bhgqn,bhnd->bhgqd', p_cmp, vc, preferred_element_type=f32)
        imp = jnp.einsum('bhgqn,nj->bhqj', p_cmp, cover)
        cur = (qpos // SEL_BLOCK)[:, None]
        jj = jnp.arange(n_sel)[None, :]
        forced = (jj == 0) | (jj == cur) | (jj == cur - 1)
        imp = jnp.where(forced, imp + FORCE_BONUS, imp)
        imp = jnp.where(jj <= cur, imp, -FORCE_BONUS)
        _, idx = lax.top_k(imp, n_top)
        gk = gather_blocks(ks_blk, idx).reshape(bsz, NSA_KV_HEADS, qb, n_top * SEL_BLOCK, dh)
        gv = gather_blocks(vs_blk, idx).reshape(bsz, NSA_KV_HEADS, qb, n_top * SEL_BLOCK, dh)
        kpos = (idx[..., None] * SEL_BLOCK + jnp.arange(SEL_BLOCK)).reshape(bsz, NSA_KV_HEADS, qb, n_top * SEL_BLOCK)
        s_s = jnp.einsum('bhgqd,bhqmd->bhgqm', qo, gk, preferred_element_type=f32) * scale
        p_s = masked_softmax(s_s, (kpos <= qpos[:, None])[:, :, None])
        o_slc = jnp.einsum('bhgqm,bhqmd->bhgqd', p_s, gv, preferred_element_type=f32)
        wk = lax.dynamic_slice_in_dim(win_k, s0, WINDOW + qb, axis=2)
        wv = lax.dynamic_slice_in_dim(win_v, s0, WINDOW + qb, axis=2)
        wpos = past_len - WINDOW + s0 + jnp.arange(WINDOW + qb)
        wmask = ((wpos[None, :] >= 0) & (wpos[None, :] <= qpos[:, None])
                 & (qpos[:, None] - wpos[None, :] < WINDOW))
        s_w = jnp.einsum('bhgqd,bhkd->bhgqk', qo, wk, preferred_element_type=f32) * scale
        p_w = masked_softmax(s_w, wmask)
        o_win = jnp.einsum('bhgqk,bhkd->bhgqd', p_w, wv, preferred_element_type=f32)
        o = g[..., 0:1] * o_cmp + g[..., 1:2] * o_slc + g[..., 2:3] * o_win
        return o.transpose(0, 3, 1, 2, 4).reshape(bsz, qb, NSA_Q).astype(x.dtype)

    nsa_out = from_blocks(lax.map(nsa_block, (to_blocks(q_raw), to_blocks(q_rot), to_blocks(gates), starts)))
    new_win = win_all[:, win_all.shape[1] - min(WINDOW, win_all.shape[1]):]

    sc_conv, new_sc = causal_dwconv(sc_c * sc_h, past_sc, sc_conv_w)
    sc_out = sc_b * sc_conv

    dq_r = rope(dq.reshape(bsz, t, DIFF_KV_HEADS, DIFF_GROUP, 2, dh), pos)
    dk_r = rope(dk.reshape(bsz, t, DIFF_KV_HEADS, 2, dh), pos).reshape(bsz, t, DIFF_KV_HEADS, 2 * dh)
    kv_diff = jnp.stack([dk_r, dv.reshape(bsz, t, DIFF_KV_HEADS, 2 * dh)], axis=2)
    full_diff = jnp.concatenate([past_diff.astype(x.dtype), kv_diff], axis=1)
    fk = full_diff[:, :, 0].reshape(bsz, tk, DIFF_KV_HEADS, 2, dh).transpose(0, 2, 3, 1, 4)
    fv = full_diff[:, :, 1].transpose(0, 2, 1, 3)
    kpos_d = jnp.arange(tk)
    lam_init = 0.8 - 0.6 * math.exp(-0.3 * layer_idx)
    dl = diff_lambda.astype(f32)
    lam = jnp.exp(jnp.sum(dl[0] * dl[1])) - jnp.exp(jnp.sum(dl[2] * dl[3])) + lam_init

    def diff_block(args):
        qd, s0 = args
        qd = qd.transpose(0, 2, 3, 4, 1, 5)
        qpos = past_len + s0 + jnp.arange(qb)
        s = jnp.einsum('bhgiqd,bhikd->bhgiqk', qd, fk, preferred_element_type=f32) * scale
        a = masked_softmax(s, kpos_d[None, :] <= qpos[:, None])
        attn = a[:, :, :, 0] - lam * a[:, :, :, 1]
        o = jnp.einsum('bhgqk,bhkd->bhgqd', attn, fv, preferred_element_type=f32)
        o = rms_unit(o) * (1.0 - lam_init)
        return o.transpose(0, 3, 1, 2, 4).reshape(bsz, qb, DIFF_Q).astype(x.dtype)

    diff_out = from_blocks(lax.map(diff_block, (to_blocks(dq_r), starts)))

    s5_out, new_s5r, new_s5i = s5_mixer(s5_u, past_s5r, past_s5i, a_re, a_im, log_dt, b_re, b_im,
                                        c_re, c_im, d_skip, glu_w, glu_b)

    mix = jnp.concatenate([rms_unit(nsa_out), rms_unit(sc_out), diff_out, rms_unit(s5_out)], axis=-1) * mix_gain
    x = layer_norm(DN_ALPHA * x + mix @ w_out, ln1_g, ln1_b)

    up = x @ ffn_w_up
    a_up, b_up = jnp.split(up, 2, axis=-1)
    a_conv, new_ffn = causal_dwconv(a_up, past_ffn, ffn_conv_w)
    f = (jax.nn.silu(a_conv) * b_up) @ ffn_w_down
    x = layer_norm(DN_ALPHA * x + f, ln2_g, ln2_b)
    return x, (kv_cmp, kv_slc, new_win, kv_diff, new_sc, new_s5r, new_s5i, new_ffn)


def run_trunk(x, past_len, layer_pasts, weights):
    states = []
    for l in range(DEPTH):
        x, st = hybrid_layer(x, past_len, layer_pasts[l], tuple(w[l] for w in weights), l)
        states.append(st)
    stacked = [jnp.stack([s[i] for s in states], axis=0) for i in range(len(states[0]))]
    return x, stacked


def setup_inputs(seed: int = 0) -> dict:
    key = jax.random.key(seed)
    ks = iter(jax.random.split(key, 40))
    f32 = jnp.float32

    def nrm(shape, scale=1.0):
        return jax.random.normal(next(ks), shape, f32) * scale

    n_pages = PAST_LEN // PAGE_SIZE
    n_used = DEC_BATCH * n_pages
    n_pool = n_used + (n_used + 3) // 4
    page_table = jax.random.permutation(next(ks), n_pool)[:n_used].reshape(DEC_BATCH, n_pages).astype(jnp.int32)
    w_buf = min(WINDOW, PAST_LEN)
    n_idx = jnp.arange(S5_STATE, dtype=f32)
    inputs = {}
    inputs['x_prompt'] = nrm((BATCH, SEQ, D_MODEL))
    inputs['x_sample'] = nrm((DEC_BATCH, DEC_SEQ, D_MODEL))
    inputs['cache_nsa_cmp'] = nrm((DEPTH, n_pool, PAGE_SIZE, 2, NSA_KV_HEADS, HEAD_DIM))
    inputs['cache_nsa_slc'] = nrm((DEPTH, n_pool, PAGE_SIZE, 2, NSA_KV_HEADS, HEAD_DIM))
    inputs['cache_diff'] = nrm((DEPTH, n_pool, PAGE_SIZE, 2, DIFF_KV_HEADS, 2 * HEAD_DIM))
    inputs['state_nsa_win'] = nrm((DEPTH, DEC_BATCH, w_buf, 2, NSA_KV_HEADS, HEAD_DIM))
    inputs['state_sconv'] = nrm((DEPTH, DEC_BATCH, CONV_W - 1, SC_WIDTH))
    inputs['state_s5_re'] = nrm((DEPTH, DEC_BATCH, S5_GROUPS, S5_STATE))
    inputs['state_s5_im'] = nrm((DEPTH, DEC_BATCH, S5_GROUPS, S5_STATE))
    inputs['state_ffn_conv'] = nrm((DEPTH, DEC_BATCH, CONV_W - 1, D_FF))
    inputs['page_table'] = page_table
    inputs['w_in'] = nrm((DEPTH, D_MODEL, IN_COLS), D_MODEL ** -0.5)
    inputs['nsa_phi_pe'] = nrm((DEPTH, 2, CMP_LEN, HEAD_DIM), 0.1)
    inputs['nsa_phi_w'] = nrm((DEPTH, 2, CMP_LEN * HEAD_DIM, HEAD_DIM), (CMP_LEN * HEAD_DIM) ** -0.5)
    inputs['sc_conv_w'] = nrm((DEPTH, CONV_W, SC_WIDTH), CONV_W ** -0.5)
    inputs['diff_lambda'] = nrm((DEPTH, 4, HEAD_DIM), 0.1)
    inputs['s5_a_re'] = -0.5 + nrm((DEPTH, S5_GROUPS, S5_STATE), 0.01)
    inputs['s5_a_im'] = math.pi * n_idx[None, None, :] + nrm((DEPTH, S5_GROUPS, S5_STATE), 0.01)
    inputs['s5_log_dt'] = jax.random.uniform(next(ks), (DEPTH, S5_GROUPS), f32, math.log(1e-3), math.log(1e-1))
    inputs['s5_b_re'] = nrm((DEPTH, S5_GROUPS, S5_STATE, S5_CH), (2 * S5_CH) ** -0.5)
    inputs['s5_b_im'] = nrm((DEPTH, S5_GROUPS, S5_STATE, S5_CH), (2 * S5_CH) ** -0.5)
    inputs['s5_c_re'] = nrm((DEPTH, S5_GROUPS, S5_CH, S5_STATE), (2 * S5_STATE) ** -0.5)
    inputs['s5_c_im'] = nrm((DEPTH, S5_GROUPS, S5_CH, S5_STATE), (2 * S5_STATE) ** -0.5)
    inputs['s5_d'] = nrm((DEPTH, S5_WIDTH))
    inputs['s5_glu_w'] = nrm((DEPTH, S5_WIDTH, S5_WIDTH), S5_WIDTH ** -0.5)
    inputs['s5_glu_b'] = nrm((DEPTH, S5_WIDTH), 0.01)
    inputs['mix_gain'] = 1.0 + nrm((DEPTH, D_MODEL), 0.01)
    inputs['w_out'] = nrm((DEPTH, D_MODEL, D_MODEL), D_MODEL ** -0.5 * DN_BETA)
    inputs['ln1_g'] = 1.0 + nrm((DEPTH, D_MODEL), 0.01)
    inputs['ln1_b'] = nrm((DEPTH, D_MODEL), 0.01)
    inputs['ffn_w_up'] = nrm((DEPTH, D_MODEL, 2 * D_FF), D_MODEL ** -0.5)
    inputs['ffn_conv_w'] = nrm((DEPTH, CONV_W, D_FF), CONV_W ** -0.5)
    inputs['ffn_w_down'] = nrm((DEPTH, D_FF, D_MODEL), D_FF ** -0.5 * DN_BETA)
    inputs['ln2_g'] = 1.0 + nrm((DEPTH, D_MODEL), 0.01)
    inputs['ln2_b'] = nrm((DEPTH, D_MODEL), 0.01)
    return inputs


def reference(x_prompt, x_sample, cache_nsa_cmp, cache_nsa_slc, cache_diff, state_nsa_win,
              state_sconv, state_s5_re, state_s5_im, state_ffn_conv, page_table,
              w_in, nsa_phi_pe, nsa_phi_w, sc_conv_w, diff_lambda, s5_a_re, s5_a_im, s5_log_dt,
              s5_b_re, s5_b_im, s5_c_re, s5_c_im, s5_d, s5_glu_w, s5_glu_b, mix_gain, w_out,
              ln1_g, ln1_b, ffn_w_up, ffn_conv_w, ffn_w_down, ln2_g, ln2_b):
    weights = (w_in, nsa_phi_pe, nsa_phi_w, sc_conv_w, diff_lambda, s5_a_re, s5_a_im, s5_log_dt,
               s5_b_re, s5_b_im, s5_c_re, s5_c_im, s5_d, s5_glu_w, s5_glu_b, mix_gain, w_out,
               ln1_g, ln1_b, ffn_w_up, ffn_conv_w, ffn_w_down, ln2_g, ln2_b)
    dt = x_prompt.dtype
    bp = x_prompt.shape[0]
    empty = (jnp.zeros((bp, 0, 2, NSA_KV_HEADS, HEAD_DIM), dt),
             jnp.zeros((bp, 0, 2, NSA_KV_HEADS, HEAD_DIM), dt),
             jnp.zeros((bp, 0, 2, NSA_KV_HEADS, HEAD_DIM), dt),
             jnp.zeros((bp, 0, 2, DIFF_KV_HEADS, 2 * HEAD_DIM), dt),
             jnp.zeros((bp, CONV_W - 1, SC_WIDTH), dt),
             jnp.zeros((bp, S5_GROUPS, S5_STATE), jnp.float32),
             jnp.zeros((bp, S5_GROUPS, S5_STATE), jnp.float32),
             jnp.zeros((bp, CONV_W - 1, D_FF), dt))
    y_prompt, ps = run_trunk(x_prompt, 0, [empty] * DEPTH, weights)
    (p_cmp, p_slc, p_win, p_diff, p_sc, p_s5r, p_s5i, p_ffn) = ps

    past_len = page_table.shape[1] * cache_nsa_cmp.shape[2]

    def paged(pool):
        g = pool[page_table]
        return g.reshape((g.shape[0], g.shape[1] * g.shape[2]) + g.shape[3:])

    sample_pasts = [(paged(cache_nsa_cmp[l]), paged(cache_nsa_slc[l]), state_nsa_win[l],
                     paged(cache_diff[l]), state_sconv[l], state_s5_re[l], state_s5_im[l],
                     state_ffn_conv[l]) for l in range(DEPTH)]
    y_sample, ss = run_trunk(x_sample, past_len, sample_pasts, weights)
    (s_cmp, s_slc, s_win, s_diff, s_sc, s_s5r, s_s5i, s_ffn) = ss
    return (y_prompt, y_sample,
            p_cmp, p_slc, p_win, p_diff, p_sc, p_s5r, p_s5i, p_ffn,
            s_cmp, s_slc, s_win, s_diff, s_sc, s_s5r, s_s5i, s_ffn)
```

```python
import functools
import math

import jax
import jax.numpy as jnp
from jax import lax
from jax.experimental import pallas as pl
from jax.experimental.pallas import tpu as pltpu

f32 = jnp.float32
bf16 = jnp.bfloat16

D_MODEL = 2048
DEPTH = 2
PAGE_SIZE = 128
HEAD_DIM = 64
GROUP_WIDTH = D_MODEL // 4
NSA_KV_HEADS = 2
NSA_GROUP = 4
CMP_STRIDE = 16
CMP_LEN = 32
SEL_BLOCK = 64
SEL_TOPK = 16
WINDOW = 512
FORCE_BONUS = 1e4
CONV_W = 3
S5_CH = 16
S5_GROUPS = 32
S5_STATE = 64
D_FF = 5632
ROPE_THETA = 10000.0
QBLOCK = 128
LN_EPS = 1e-5
RMS_EPS = 1e-6
NEG_INF = -1e30
DN_ALPHA = (2 * DEPTH) ** 0.25
SCALE = HEAD_DIM ** -0.5

IN_SPLITS = (512, 256, 256, 256, 24, 512, 512, 512, 512, 256, 256, 512)
C_NQ, C_SCB, C_SCC, C_SCH, C_DQ, C_S5U = 0, 512, 1024, 1536, 2048, 2560
C_CMP, C_SLC, C_WIN, C_DK, C_DV, C_GATE = 3072, 3328, 3584, 3840, 4096, 4352
HC = 4480

VMEM_CAP_V7X = 64 * 1024 * 1024
VMEM_LIMIT = 56 * 1024 * 1024
NC_PAD = 640
S5_SLABS = 4


def _cp(sem):
    return pltpu.CompilerParams(dimension_semantics=sem, vmem_limit_bytes=VMEM_LIMIT)


def _dot(a, b):
    return jnp.dot(a, b, preferred_element_type=f32)


def _dot_nt(a, b):
    return lax.dot_general(a, b, (((1,), (1,)), ((), ())), preferred_element_type=f32)


def _lane(shape):
    return lax.broadcasted_iota(jnp.int32, shape, len(shape) - 1)


def _row(shape):
    return lax.broadcasted_iota(jnp.int32, shape, len(shape) - 2)


def _msoftmax(s, mask):
    s = jnp.where(mask, s, NEG_INF)
    m = jnp.max(s, axis=-1, keepdims=True)
    e = jnp.where(mask, jnp.exp(s - m), 0.0)
    return e * (1.0 / jnp.maximum(jnp.sum(e, axis=-1, keepdims=True), 1e-30))


def _sigmoid(x):
    return 1.0 / (1.0 + jnp.exp(-x))


def _rms_unit(x):
    return x * lax.rsqrt(jnp.mean(x * x, axis=-1, keepdims=True) + RMS_EPS)


def _layer_norm(z, g, b):
    mu = jnp.mean(z, axis=-1, keepdims=True)
    d = z - mu
    var = jnp.mean(d * d, axis=-1, keepdims=True)
    return d * lax.rsqrt(var + LN_EPS) * g + b


def _split_hi_lo(x):
    hi = x.astype(bf16)
    lo = (x - hi.astype(f32)).astype(bf16)
    return hi, lo


def _matmul_kernel(x_ref, w_ref, o_ref):
    o_ref[...] = _dot(x_ref[...], w_ref[...])


def _in_proj(xb, w, tm):
    m = xb.shape[0]
    tn = 640
    return pl.pallas_call(
        _matmul_kernel,
        grid=(m // tm, HC // tn),
        in_specs=[pl.BlockSpec((tm, D_MODEL), lambda i, j: (i, 0)),
                  pl.BlockSpec((D_MODEL, tn), lambda i, j: (0, j))],
        out_specs=pl.BlockSpec((tm, tn), lambda i, j: (i, j)),
        out_shape=jax.ShapeDtypeStruct((m, HC), f32),
        compiler_params=_cp(("parallel", "arbitrary")),
        name="in_proj",
    )(xb, w)


def _rope_cols(x, cos, sin):
    outs = []
    first = (_lane((1, 128)) % 64) < 32
    for c in range(x.shape[1] // 128):
        xc = x[:, c * 128:(c + 1) * 128]
        sw = jnp.where(first, pltpu.roll(xc, 96, 1), pltpu.roll(xc, 32, 1))
        outs.append(xc * cos + sw * sin)
    return outs


def _rope_kernel(nq_ref, dq_ref, slc_ref, win_ref, dk_ref, dv_ref, cos_ref, sin_ref,
                 qrot_ref, dqrot_ref, kvslc_ref, kvwin_ref, kvdiff_ref):
    cos = cos_ref[...]
    sin = sin_ref[...]
    for c, v in enumerate(_rope_cols(nq_ref[...], cos, sin)):
        qrot_ref[:, c * 128:(c + 1) * 128] = v
    for c, v in enumerate(_rope_cols(dq_ref[...], cos, sin)):
        dqrot_ref[:, c * 128:(c + 1) * 128] = v
    kvslc_ref[:, 0:128] = _rope_cols(slc_ref[:, 0:128], cos, sin)[0]
    kvslc_ref[:, 128:256] = slc_ref[:, 128:256]
    kvwin_ref[:, 0:128] = _rope_cols(win_ref[:, 0:128], cos, sin)[0]
    kvwin_ref[:, 128:256] = win_ref[:, 128:256]
    for c, v in enumerate(_rope_cols(dk_ref[...], cos, sin)):
        kvdiff_ref[:, c * 128:(c + 1) * 128] = v
    kvdiff_ref[:, 256:512] = dv_ref[...]


def _rope(hcat, cos, sin, tr):
    m = hcat.shape[0]
    nt = cos.shape[0] // tr

    def col(w, off):
        return pl.BlockSpec((tr, w), lambda i: (i, off // w))

    tab = pl.BlockSpec((tr, 128), lambda i: (i % nt, 0))
    return pl.pallas_call(
        _rope_kernel,
        grid=(m // tr,),
        in_specs=[col(512, C_NQ), col(512, C_DQ), col(256, C_SLC), col(256, C_WIN),
                  col(256, C_DK), col(256, C_DV), tab, tab],
        out_specs=[pl.BlockSpec((tr, 512), lambda i: (i, 0)),
                   pl.BlockSpec((tr, 512), lambda i: (i, 0)),
                   pl.BlockSpec((tr, 256), lambda i: (i, 0)),
                   pl.BlockSpec((tr, 256), lambda i: (i, 0)),
                   pl.BlockSpec((tr, 512), lambda i: (i, 0))],
        out_shape=[jax.ShapeDtypeStruct((m, 512), f32), jax.ShapeDtypeStruct((m, 512), f32),
                   jax.ShapeDtypeStruct((m, 256), f32), jax.ShapeDtypeStruct((m, 256), f32),
                   jax.ShapeDtypeStruct((m, 512), f32)],
        compiler_params=_cp(("parallel",)),
        name="rope",
    )(hcat, hcat, hcat, hcat, hcat, hcat, cos, sin)


def _cmp_prompt_kernel(z_ref, pe1_ref, pe2_ref, w1_ref, w2_ref, o_ref, sb_ref):
    z = z_ref[...]
    a = _dot((z + pe1_ref[...]).astype(bf16), w1_ref[...])
    bm = _dot((z + pe2_ref[...]).astype(bf16), w2_ref[...])
    n = z.shape[0]
    sb_ref[0:n, :] = bm
    sb_ref[n:n + 8, :] = jnp.zeros((8, 256), f32)
    o_ref[0] = a + sb_ref[1:n + 1, :]


def _cmp_prompt(z, pe1, pe2, w1, w2, bsz):
    n = z.shape[0] // bsz
    full = lambda shp: pl.BlockSpec(shp, lambda b: (0, 0))
    return pl.pallas_call(
        _cmp_prompt_kernel,
        grid=(bsz,),
        in_specs=[pl.BlockSpec((n, 4096), lambda b: (b, 0)), full((1, 4096)), full((1, 4096)),
                  full((4096, 256)), full((4096, 256))],
        out_specs=pl.BlockSpec((1, n, 256), lambda b: (b, 0, 0)),
        out_shape=jax.ShapeDtypeStruct((bsz, n, 256), f32),
        scratch_shapes=[pltpu.VMEM((n + 8, 256), f32)],
        compiler_params=_cp(("parallel",)),
        name="cmp_prompt",
    )(z, pe1, pe2, w1, w2)


def _nsa_qstack(blk, h):
    halfmask = (_lane((1, 128)) // 64) == h
    parts = []
    for g in range(4):
        c = blk[:, (g // 2) * 128:(g // 2 + 1) * 128]
        if g % 2 != h:
            c = pltpu.roll(c, 64, 1)
        parts.append(jnp.where(halfmask, c, 0.0))
    return jnp.concatenate(parts, axis=0)


def _nsa_assemble(o_list, h):
    lo = _lane((1, 128)) < 64
    chunks = []
    for gp in range(2):
        a, b = o_list[2 * gp], o_list[2 * gp + 1]
        if h == 0:
            b = pltpu.roll(b, 64, 1)
        else:
            a = pltpu.roll(a, 64, 1)
        chunks.append(jnp.where(lo, a, b))
    return chunks


def _nsa_prompt_kernel(qraw_ref, qrot_ref, gate_ref, kvc_ref, slc_ref, win_ref, out_ref, *, t_len):
    qb = QBLOCK
    s0 = pl.program_id(1) * qb
    qpos = s0 + _row((qb, 1))
    gs = _sigmoid(gate_ref[...])
    n_i = _lane((1, 128))
    maskc = ((16 * n_i + 31) <= qpos) & (n_i < 127)
    nn = _row((128, 128))
    j2 = _lane((128, 128))
    cov = jnp.clip(jnp.minimum(16 * nn + 32, 64 * j2 + 64) - jnp.maximum(16 * nn, 64 * j2), 0, 32)
    cov = (jnp.where((nn < 127) & (j2 < 32), cov, 0).astype(f32) * (1.0 / CMP_LEN)).astype(bf16)
    expand = jnp.where((_lane((128, t_len)) // SEL_BLOCK) == _row((128, t_len)), 1.0, 0.0).astype(bf16)
    kpos = _lane((1, t_len))
    causal = kpos <= qpos
    wlen = WINDOW + qb
    start = pl.multiple_of(jnp.clip(s0 - WINDOW, 0, t_len - wlen), 128)
    wpos = start + _lane((1, wlen))
    maskw = (wpos <= qpos) & ((qpos - wpos) < WINDOW)
    jj = _lane((1, 128))
    cur = qpos // SEL_BLOCK
    n_sel = t_len // SEL_BLOCK
    forced = (jj == 0) | (jj == cur) | (jj == cur - 1)

    for h in range(NSA_KV_HEADS):
        kc = kvc_ref[0, :, 0:128].astype(bf16)
        vc = kvc_ref[0, :, 128:256].astype(bf16)
        qr = (_nsa_qstack(qraw_ref[:, h * 256:(h + 1) * 256], h) * SCALE).astype(bf16)
        qo = (_nsa_qstack(qrot_ref[:, h * 256:(h + 1) * 256], h) * SCALE).astype(bf16)
        s_c = _dot_nt(qr, kc).reshape(4, qb, 128)
        p_c = _msoftmax(s_c, maskc[None])
        o_cmp = _dot(p_c.reshape(4 * qb, 128).astype(bf16), vc)
        psum = p_c[0] + p_c[1] + p_c[2] + p_c[3]
        p_hi, p_lo = _split_hi_lo(psum)
        imp = _dot(p_hi, cov) + _dot(p_lo, cov)
        imp = jnp.where(forced, imp + FORCE_BONUS, imp)
        imp = jnp.where(jj <= cur, imp, -FORCE_BONUS)
        imp = jnp.where(jj < n_sel, imp, -3e38)
        rank = jnp.zeros((qb, 128), f32)
        for k in range(n_sel):
            col = imp[:, k:k + 1]
            beats = (col > imp) | ((col == imp) & (jj > k))
            rank = rank + jnp.where(beats, 1.0, 0.0)
        sel = jnp.where((rank < SEL_TOPK) & (jj < n_sel), 1.0, 0.0).astype(bf16)
        masks = (_dot(sel, expand) > 0.5) & causal
        ks = slc_ref[:, 0:128].astype(bf16)
        vs = slc_ref[:, 128:256].astype(bf16)
        s_s = _dot_nt(qo, ks).reshape(4, qb, t_len)
        p_s = _msoftmax(s_s, masks[None])
        o_slc = _dot(p_s.reshape(4 * qb, t_len).astype(bf16), vs)
        kw = win_ref[pl.ds(start, wlen), 0:128].astype(bf16)
        vw = win_ref[pl.ds(start, wlen), 128:256].astype(bf16)
        s_w = _dot_nt(qo, kw).reshape(4, qb, wlen)
        p_w = _msoftmax(s_w, maskw[None])
        o_win = _dot(p_w.reshape(4 * qb, wlen).astype(bf16), vw)
        o_list = []
        for g in range(NSA_GROUP):
            gi = (h * NSA_GROUP + g) * 3
            r = slice(g * qb, (g + 1) * qb)
            o_list.append(gs[:, gi:gi + 1] * o_cmp[r] + gs[:, gi + 1:gi + 2] * o_slc[r]
                          + gs[:, gi + 2:gi + 3] * o_win[r])
        for gp, ch in enumerate(_nsa_assemble(o_list, h)):
            out_ref[:, h * 256 + gp * 128:h * 256 + (gp + 1) * 128] = ch


def _nsa_prompt(hcat, qrot, kvc, kvslc, kvwin, bsz, t_len):
    m = hcat.shape[0]
    nqb = t_len // QBLOCK
    return pl.pallas_call(
        functools.partial(_nsa_prompt_kernel, t_len=t_len),
        grid=(bsz, nqb),
        in_specs=[pl.BlockSpec((QBLOCK, 512), lambda b, i: (b * nqb + i, 0)),
                  pl.BlockSpec((QBLOCK, 512), lambda b, i: (b * nqb + i, 0)),
                  pl.BlockSpec((QBLOCK, 128), lambda b, i: (b * nqb + i, C_GATE // 128)),
                  pl.BlockSpec((1, 128, 256), lambda b, i: (b, 0, 0)),
                  pl.BlockSpec((t_len, 256), lambda b, i: (b, 0)),
                  pl.BlockSpec((t_len, 256), lambda b, i: (b, 0))],
        out_specs=pl.BlockSpec((QBLOCK, 512), lambda b, i: (b * nqb + i, 0)),
        out_shape=jax.ShapeDtypeStruct((m, 512), f32),
        compiler_params=_cp(("parallel", "arbitrary")),
        name="nsa_prompt",
    )(hcat, qrot, hcat, kvc, kvslc, kvwin)


def _diff_lambda(dl, lam_init):
    a = jnp.sum(dl[0:1, :] * dl[1:2, :], axis=-1, keepdims=True)
    b = jnp.sum(dl[2:3, :] * dl[3:4, :], axis=-1, keepdims=True)
    return jnp.exp(a) - jnp.exp(b) + lam_init


def _diff_prompt_kernel(dq_ref, kv_ref, dl_ref, out_ref, *, t_len, lam_init):
    qb = QBLOCK
    s0 = pl.program_id(1) * qb
    qpos = s0 + _row((qb, 1))
    causal = _lane((1, t_len)) <= qpos
    lam = _diff_lambda(dl_ref[...], lam_init)
    lane = _lane((1, 128))
    for h in range(2):
        k = kv_ref[:, h * 128:(h + 1) * 128].astype(bf16)
        v = kv_ref[:, 256 + h * 128:256 + (h + 1) * 128].astype(bf16)
        parts = []
        for g in range(2):
            c = dq_ref[:, h * 256 + g * 128:h * 256 + (g + 1) * 128] * SCALE
            for i in range(2):
                parts.append(jnp.where((lane // 64) == i, c, 0.0))
        q = jnp.concatenate(parts, axis=0).astype(bf16)
        s = _dot_nt(q, k).reshape(4, qb, t_len)
        a = _msoftmax(s, causal[None])
        attn = jnp.concatenate([a[0] - lam * a[1], a[2] - lam * a[3]], axis=0).astype(bf16)
        o = _rms_unit(_dot(attn, v)) * (1.0 - lam_init)
        for g in range(2):
            out_ref[:, (h * 2 + g) * 128:(h * 2 + g + 1) * 128] = o[g * qb:(g + 1) * qb]


def _diff_prompt(dqrot, kvdiff, dl, bsz, t_len, lam_init):
    m = dqrot.shape[0]
    nqb = t_len // QBLOCK
    return pl.pallas_call(
        functools.partial(_diff_prompt_kernel, t_len=t_len, lam_init=lam_init),
        grid=(bsz, nqb),
        in_specs=[pl.BlockSpec((QBLOCK, 512), lambda b, i: (b * nqb + i, 0)),
                  pl.BlockSpec((t_len, 512), lambda b, i: (b, 0)),
                  pl.BlockSpec((4, 64), lambda b, i: (0, 0))],
        out_specs=pl.BlockSpec((QBLOCK, 512), lambda b, i: (b * nqb + i, 0)),
        out_shape=jax.ShapeDtypeStruct((m, 512), f32),
        compiler_params=_cp(("parallel", "arbitrary")),
        name="diff_prompt",
    )(dqrot, kvdiff, dl)


def _sconv_prompt_kernel(b_ref, c_ref, h_ref, w_ref, out_ref, tail_ref, buf_ref, *, tr):
    t = pl.program_id(1)

    @pl.when(t == 0)
    def _():
        buf_ref[0:8, :] = jnp.zeros((8, 512), f32)

    z = c_ref[...] * h_ref[...]
    buf_ref[8:8 + tr, :] = z
    w = w_ref[...]
    y = w[0:1] * buf_ref[6:6 + tr, :] + w[1:2] * buf_ref[7:7 + tr, :] + w[2:3] * z
    out_ref[...] = _rms_unit(b_ref[...] * y)
    tail = z[tr - 8:tr]
    tail_ref[0] = tail
    buf_ref[0:8, :] = tail


def _sconv_prompt(hcat, w, bsz, t_len):
    tr = 512
    nt = t_len // tr
    m = hcat.shape[0]

    def col(off):
        return pl.BlockSpec((tr, 512), lambda b, t: (b * nt + t, off // 512))

    return pl.pallas_call(
        functools.partial(_sconv_prompt_kernel, tr=tr),
        grid=(bsz, nt),
        in_specs=[col(C_SCB), col(C_SCC), col(C_SCH), pl.BlockSpec((3, 512), lambda b, t: (0, 0))],
        out_specs=[pl.BlockSpec((tr, 512), lambda b, t: (b * nt + t, 0)),
                   pl.BlockSpec((1, 8, 512), lambda b, t: (b, 0, 0))],
        out_shape=[jax.ShapeDtypeStruct((m, 512), f32), jax.ShapeDtypeStruct((bsz, 8, 512), f32)],
        scratch_shapes=[pltpu.VMEM((8 + tr, 512), f32)],
        compiler_params=_cp(("parallel", "arbitrary")),
        name="sconv_prompt",
    )(hcat, hcat, hcat, w)


def _cmul(ar, ai, br, bi):
    return ar * br - ai * bi, ar * bi + ai * br


def _s5_prep_kernel(ar_ref, ai_ref, ldt_ref, br_ref, bi_ref, tab_ref, bbr_ref, bbi_ref):
    ar, ai = ar_ref[...], ai_ref[...]
    dt = jnp.exp(ldt_ref[...])
    mag = jnp.exp(ar * dt)
    abr, abi = mag * jnp.cos(ai * dt), mag * jnp.sin(ai * dt)
    den = ar * ar + ai * ai
    nr, ni = abr - 1.0, abi
    cre = (nr * ar + ni * ai) / den
    cim = (ni * ar - nr * ai) / den
    br, bi = br_ref[...], bi_ref[...]
    bbr_ref[...] = cre * br - cim * bi
    bbi_ref[...] = cre * bi + cim * br
    pw = [(abr, abi)]
    for _ in range(7):
        pw.append(_cmul(pw[-1][0], pw[-1][1], abr, abi))
    n = ar.shape[1]
    row = _row((8, n))
    zero = jnp.zeros((8, n), f32)
    for idx, (sh, p) in enumerate(((1, pw[0]), (2, pw[1]), (4, pw[3]))):
        tab_ref[2 * idx] = jnp.where(row >= sh, jnp.broadcast_to(p[0], (8, n)), zero)
        tab_ref[2 * idx + 1] = jnp.where(row >= sh, jnp.broadcast_to(p[1], (8, n)), zero)
    pr, pi = zero, zero
    for i in range(8):
        pr = jnp.where(row == i, jnp.broadcast_to(pw[i][0], (8, n)), pr)
        pi = jnp.where(row == i, jnp.broadcast_to(pw[i][1], (8, n)), pi)
    tab_ref[6] = pr
    tab_ref[7] = pi


def _s5_prep(ar, ai, ldt, br, bi):
    n = S5_GROUPS * S5_STATE
    return pl.pallas_call(
        _s5_prep_kernel,
        out_shape=[jax.ShapeDtypeStruct((8, 8, n), f32), jax.ShapeDtypeStruct((S5_CH, n), f32),
                   jax.ShapeDtypeStruct((S5_CH, n), f32)],
        name="s5_prep",
    )(ar, ai, ldt, br, bi)


def _s5_scan_kernel(u_ref, wb_ref, wc_ref, d_ref, tab_ref, y_ref, hr_ref, hi_ref,
                    xbuf_ref, cr_ref, ci_ref, *, tt):
    t = pl.program_id(2)

    @pl.when(t == 0)
    def _():
        cr_ref[...] = jnp.zeros((8, 512), f32)
        ci_ref[...] = jnp.zeros((8, 512), f32)

    u = u_ref[...]
    xbuf_ref[...] = _dot(u.astype(bf16), wb_ref[0])

    def body(r, carry):
        cr, ci = carry
        rows = pl.ds(pl.multiple_of(r * 8, 8), 8)
        xr = xbuf_ref[rows, 0:512]
        xi = xbuf_ref[rows, 512:1024]
        for idx, sh in enumerate((1, 2, 4)):
            a_r, a_i = tab_ref[2 * idx], tab_ref[2 * idx + 1]
            sr, si = pltpu.roll(xr, sh, 0), pltpu.roll(xi, sh, 0)
            xr, xi = xr + a_r * sr - a_i * si, xi + a_r * si + a_i * sr
        p_r, p_i = tab_ref[6], tab_ref[7]
        hr = xr + p_r * cr - p_i * ci
        hi = xi + p_r * ci + p_i * cr
        xbuf_ref[rows, 0:512] = hr
        xbuf_ref[rows, 512:1024] = hi
        return (jnp.broadcast_to(hr[7:8, :], (8, 512)), jnp.broadcast_to(hi[7:8, :], (8, 512)))

    cr, ci = lax.fori_loop(0, tt // 8, body, (cr_ref[...], ci_ref[...]))
    cr_ref[...] = cr
    ci_ref[...] = ci
    hr_ref[0] = cr[0:1, :]
    hi_ref[0] = ci[0:1, :]
    y_ref[...] = _dot(xbuf_ref[...].astype(bf16), wc_ref[0]) + d_ref[...] * u


def _s5_scan(hcat, wb, wc, d, tabs, bsz, t_len):
    tt = 512
    nt = t_len // tt
    m = hcat.shape[0]
    n = S5_GROUPS * S5_STATE
    return pl.pallas_call(
        functools.partial(_s5_scan_kernel, tt=tt),
        grid=(bsz, S5_SLABS, nt),
        in_specs=[pl.BlockSpec((tt, 128), lambda b, s, t: (b * nt + t, C_S5U // 128 + s)),
                  pl.BlockSpec((1, 128, 1024), lambda b, s, t: (s, 0, 0)),
                  pl.BlockSpec((1, 1024, 128), lambda b, s, t: (s, 0, 0)),
                  pl.BlockSpec((1, 128), lambda b, s, t: (0, s)),
                  pl.BlockSpec((8, 8, 512), lambda b, s, t: (0, 0, s))],
        out_specs=[pl.BlockSpec((tt, 128), lambda b, s, t: (b * nt + t, s)),
                   pl.BlockSpec((1, 1, 512), lambda b, s, t: (b, 0, s)),
                   pl.BlockSpec((1, 1, 512), lambda b, s, t: (b, 0, s))],
        out_shape=[jax.ShapeDtypeStruct((m, 512), f32), jax.ShapeDtypeStruct((bsz, 1, n), f32),
                   jax.ShapeDtypeStruct((bsz, 1, n), f32)],
        scratch_shapes=[pltpu.VMEM((tt, 1024), f32), pltpu.VMEM((8, 512), f32), pltpu.VMEM((8, 512), f32)],
        compiler_params=_cp(("parallel", "parallel", "arbitrary")),
        name="s5_scan",
    )(hcat, wb, wc, d, tabs)


def _sample_small_kernel(scb_ref, scc_ref, sch_ref, scw_ref, scp_ref, u_ref, wb_ref, wc_ref, d_ref,
                         tab_ref, h0r_ref, h0i_ref, sc_ref, z_ref, y_ref, hr_ref, hi_ref):
    z = scc_ref[...] * sch_ref[...]
    w = scw_ref[...]
    y = w[0:1] * scp_ref[0] + w[1:2] * scp_ref[1] + w[2:3] * z
    sc_ref[...] = _rms_unit(scb_ref[...] * y)
    z_ref[...] = z
    u = u_ref[...]
    for s in range(S5_SLABS):
        x = _dot(u[:, s * 128:(s + 1) * 128].astype(bf16), wb_ref[s])
        lanes = slice(s * 512, (s + 1) * 512)
        a_r, a_i = tab_ref[6, 0:1, lanes], tab_ref[7, 0:1, lanes]
        h0r, h0i = h0r_ref[:, lanes], h0i_ref[:, lanes]
        hr = a_r * h0r - a_i * h0i + x[:, 0:512]
        hi = a_r * h0i + a_i * h0r + x[:, 512:1024]
        hr_ref[:, lanes] = hr
        hi_ref[:, lanes] = hi
        hcat = jnp.concatenate([hr, hi], axis=1).astype(bf16)
        cols = slice(s * 128, (s + 1) * 128)
        y_ref[:, cols] = _dot(hcat, wc_ref[s]) + d_ref[:, cols] * u[:, cols]


def _sample_small(scb, scc, sch, scw, scp, u, wb, wc, d, tabs, h0r, h0i):
    bsz = u.shape[0]
    n = S5_GROUPS * S5_STATE
    return pl.pallas_call(
        _sample_small_kernel,
        out_shape=[jax.ShapeDtypeStruct((bsz, 512), f32), jax.ShapeDtypeStruct((bsz, 512), f32),
                   jax.ShapeDtypeStruct((bsz, 512), f32), jax.ShapeDtypeStruct((bsz, n), f32),
                   jax.ShapeDtypeStruct((bsz, n), f32)],
        compiler_params=pltpu.CompilerParams(vmem_limit_bytes=VMEM_LIMIT),
        name="sample_small",
    )(scb, scc, sch, scw, scp, u, wb, wc, d, tabs, h0r, h0i)


def _gelu(x):
    return 0.5 * x * (1.0 + jnp.tanh(math.sqrt(2.0 / math.pi) * (x + 0.044715 * (x * x * x))))


def _mix_kernel(nsa_ref, sc_ref, diff_ref, s5_ref, gw_ref, gb_ref, gain_ref, wo_ref, x_ref,
                g_ref, b_ref, out_ref, outb_ref):
    y = _gelu(s5_ref[...])
    s5o = y * _sigmoid(_dot(y.astype(bf16), gw_ref[...]) + gb_ref[...])
    parts = (_rms_unit(nsa_ref[...]), sc_ref[...], diff_ref[...], _rms_unit(s5o))
    acc = None
    for k, p in enumerate(parts):
        pk = (p * gain_ref[:, k * 512:(k + 1) * 512]).astype(bf16)
        d = _dot(pk, wo_ref[k * 512:(k + 1) * 512, :])
        acc = d if acc is None else acc + d
    o = _layer_norm(DN_ALPHA * x_ref[...] + acc, g_ref[...], b_ref[...])
    out_ref[...] = o
    outb_ref[...] = o.astype(bf16)


def _mix(nsa, sc, diff, s5y, gw, gb, gain, wo, x, g, b, tm):
    m = x.shape[0]
    row = lambda w: pl.BlockSpec((tm, w), lambda i: (i, 0))
    full = lambda shp: pl.BlockSpec(shp, lambda i: (0, 0))
    return pl.pallas_call(
        _mix_kernel,
        grid=(m // tm,),
        in_specs=[row(512), row(512), row(512), row(512), full((512, 512)), full((1, 512)),
                  full((1, D_MODEL)), full((D_MODEL, D_MODEL)), row(D_MODEL), full((1, D_MODEL)),
                  full((1, D_MODEL))],
        out_specs=[row(D_MODEL), row(D_MODEL)],
        out_shape=[jax.ShapeDtypeStruct((m, D_MODEL), f32), jax.ShapeDtypeStruct((m, D_MODEL), bf16)],
        compiler_params=_cp(("parallel",)),
        name="mix_outproj_ln",
    )(nsa, sc, diff, s5y, gw, gb, gain, wo, x, g, b)


def _ffn_tail(acc_ref, xres_ref, g_ref, b_ref, out_ref, outb_ref):
    o = _layer_norm(DN_ALPHA * xres_ref[...] + acc_ref[...], g_ref[...], b_ref[...])
    out_ref[...] = o
    outb_ref[...] = o.astype(bf16)


def _ffn_prompt_kernel(x_ref, halo_ref, wa_ref, wb_ref, wd_ref, cw_ref, xres_ref, g_ref, b_ref,
                       out_ref, outb_ref, tail_ref, acc_ref, abuf_ref, *, tm, nf, tiles_per_seq):
    i = pl.program_id(0)
    f = pl.program_id(1)

    @pl.when(f == 0)
    def _():
        acc_ref[...] = jnp.zeros_like(acc_ref)

    x = x_ref[...]
    a = _dot(x, wa_ref[...])
    bb = _dot(x, wb_ref[...])
    keep = jnp.where(i % tiles_per_seq != 0, 1.0, 0.0)
    abuf_ref[0:16, :] = _dot(halo_ref[...], wa_ref[...]) * keep
    abuf_ref[16:16 + tm, :] = a
    cw = cw_ref[...]
    ac = cw[0:1] * abuf_ref[14:14 + tm, :] + cw[1:2] * abuf_ref[15:15 + tm, :] + cw[2:3] * a
    gate = (ac * _sigmoid(ac) * bb).astype(bf16)
    acc_ref[...] += _dot(gate, wd_ref[...])
    tail_ref[0] = a[tm - 8:tm]

    @pl.when(f == nf - 1)
    def _():
        _ffn_tail(acc_ref, xres_ref, g_ref, b_ref, out_ref, outb_ref)


def _ffn_prompt(xb, x, wup, wd, cw, g, b, t_len):
    m = x.shape[0]
    tm, tf = 512, 512
    nf = D_FF // tf
    full = lambda shp: pl.BlockSpec(shp, lambda i, f: (0, 0))
    return pl.pallas_call(
        functools.partial(_ffn_prompt_kernel, tm=tm, nf=nf, tiles_per_seq=t_len // tm),
        grid=(m // tm, nf),
        in_specs=[pl.BlockSpec((tm, D_MODEL), lambda i, f: (i, 0)),
                  pl.BlockSpec((16, D_MODEL), lambda i, f: (jnp.maximum(i * (tm // 16) - 1, 0), 0)),
                  pl.BlockSpec((D_MODEL, tf), lambda i, f: (0, f)),
                  pl.BlockSpec((D_MODEL, tf), lambda i, f: (0, f + nf)),
                  pl.BlockSpec((tf, D_MODEL), lambda i, f: (f, 0)),
                  pl.BlockSpec((3, tf), lambda i, f: (0, f)),
                  pl.BlockSpec((tm, D_MODEL), lambda i, f: (i, 0)),
                  full((1, D_MODEL)), full((1, D_MODEL))],
        out_specs=[pl.BlockSpec((tm, D_MODEL), lambda i, f: (i, 0)),
                   pl.BlockSpec((tm, D_MODEL), lambda i, f: (i, 0)),
                   pl.BlockSpec((1, 8, tf), lambda i, f: (i, 0, f))],
        out_shape=[jax.ShapeDtypeStruct((m, D_MODEL), f32), jax.ShapeDtypeStruct((m, D_MODEL), bf16),
                   jax.ShapeDtypeStruct((m // tm, 8, D_FF), f32)],
        scratch_shapes=[pltpu.VMEM((tm, D_MODEL), f32), pltpu.VMEM((16 + tm, tf), f32)],
        compiler_params=_cp(("parallel", "arbitrary")),
        name="ffn_prompt",
    )(xb, xb, wup, wup, wd, cw, x, g, b)


def _ffn_sample_kernel(x_ref, p0_ref, p1_ref, wa_ref, wb_ref, wd_ref, cw_ref, xres_ref, g_ref, b_ref,
                       out_ref, outb_ref, aup_ref, acc_ref, *, nf):
    f = pl.program_id(0)

    @pl.when(f == 0)
    def _():
        acc_ref[...] = jnp.zeros_like(acc_ref)

    x = x_ref[...]
    a = _dot(x, wa_ref[...])
    bb = _dot(x, wb_ref[...])
    cw = cw_ref[...]
    ac = cw[0:1] * p0_ref[...] + cw[1:2] * p1_ref[...] + cw[2:3] * a
    gate = (ac * _sigmoid(ac) * bb).astype(bf16)
    acc_ref[...] += _dot(gate, wd_ref[...])
    aup_ref[...] = a

    @pl.when(f == nf - 1)
    def _():
        _ffn_tail(acc_ref, xres_ref, g_ref, b_ref, out_ref, outb_ref)


def _ffn_sample(xb, x, p0, p1, wup, wd, cw, g, b):
    m = x.shape[0]
    tf = 512
    nf = D_FF // tf
    full = lambda shp: pl.BlockSpec(shp, lambda f: (0, 0))
    return pl.pallas_call(
        functools.partial(_ffn_sample_kernel, nf=nf),
        grid=(nf,),
        in_specs=[full((m, D_MODEL)), pl.BlockSpec((m, tf), lambda f: (0, f)),
                  pl.BlockSpec((m, tf), lambda f: (0, f)),
                  pl.BlockSpec((D_MODEL, tf), lambda f: (0, f)),
                  pl.BlockSpec((D_MODEL, tf), lambda f: (0, f + nf)),
                  pl.BlockSpec((tf, D_MODEL), lambda f: (f, 0)),
                  pl.BlockSpec((3, tf), lambda f: (0, f)),
                  full((m, D_MODEL)), full((1, D_MODEL)), full((1, D_MODEL))],
        out_specs=[full((m, D_MODEL)), full((m, D_MODEL)), pl.BlockSpec((m, tf), lambda f: (0, f))],
        out_shape=[jax.ShapeDtypeStruct((m, D_MODEL), f32), jax.ShapeDtypeStruct((m, D_MODEL), bf16),
                   jax.ShapeDtypeStruct((m, D_FF), f32)],
        scratch_shapes=[pltpu.VMEM((m, D_MODEL), f32)],
        compiler_params=_cp(("arbitrary",)),
        name="ffn_sample",
    )(xb, p0, p1, wup, wup, wd, cw, x, g, b)


def _cmp_sample_kernel(pt_ref, new_ref, pe1_ref, pe2_ref, w1_ref, w2_ref, pool_ref, o_ref,
                       buf_ref, sb_ref, sem_ref, *, layer, n_pages, n_batch):
    b = pl.program_id(0)
    slot = b % 2

    def copy(bb, sl, p):
        return pltpu.make_async_copy(pool_ref.at[layer, pt_ref[bb, p]],
                                     buf_ref.at[sl, pl.ds(p * 8, 8), :], sem_ref.at[sl])

    def fetch(bb, sl):
        for p in range(n_pages):
            copy(bb, sl, p).start()

    @pl.when(b == 0)
    def _():
        fetch(0, 0)

    @pl.when(b + 1 < n_batch)
    def _():
        fetch(b + 1, 1 - slot)

    for p in range(n_pages):
        copy(b, slot, p).wait()

    n = n_pages * 8
    z = buf_ref[slot]
    pe1, pe2 = pe1_ref[...], pe2_ref[...]
    a = _dot((z + pe1).astype(bf16), w1_ref[...])
    bm = _dot((z + pe2).astype(bf16), w2_ref[...])
    first = (_row((8, 4096)) == 0) & (_lane((8, 4096)) < 256)
    znew = jnp.where(first, jnp.broadcast_to(jnp.tile(new_ref[0], (1, 16)), (8, 4096)), 0.0)
    a_new = _dot((znew + pe1).astype(bf16), w1_ref[...])
    b_new = _dot((znew + pe2).astype(bf16), w2_ref[...])
    sb_ref[0:n, :] = bm
    sb_ref[n:n + 8, :] = b_new
    sb_ref[n + 8:n + 16, :] = jnp.zeros((8, 256), f32)
    o_ref[0, 0:n, :] = a + sb_ref[1:n + 1, :]
    o_ref[0, n:n + 8, :] = a_new + sb_ref[n + 1:n + 9, :]
    o_ref[0, n + 8:NC_PAD, :] = jnp.zeros((NC_PAD - n - 8, 256), f32)


def _cmp_sample(page_table, new_rows, pe1, pe2, w1, w2, pool, layer):
    n_batch, n_pages = page_table.shape
    n = n_pages * 8
    full = lambda shp: pl.BlockSpec(shp, lambda b, pt: (0,) * len(shp))
    grid_spec = pltpu.PrefetchScalarGridSpec(
        num_scalar_prefetch=1,
        grid=(n_batch,),
        in_specs=[pl.BlockSpec((1, 1, 256), lambda b, pt: (b, 0, 0)), full((1, 4096)), full((1, 4096)),
                  full((4096, 256)), full((4096, 256)), pl.BlockSpec(memory_space=pl.ANY)],
        out_specs=pl.BlockSpec((1, NC_PAD, 256), lambda b, pt: (b, 0, 0)),
        scratch_shapes=[pltpu.VMEM((2, n, 4096), f32), pltpu.VMEM((n + 16, 256), f32),
                        pltpu.SemaphoreType.DMA((2,))],
    )
    return pl.pallas_call(
        functools.partial(_cmp_sample_kernel, layer=layer, n_pages=n_pages, n_batch=n_batch),
        grid_spec=grid_spec,
        out_shape=jax.ShapeDtypeStruct((n_batch, NC_PAD, 256), f32),
        compiler_params=_cp(("arbitrary",)),
        name="cmp_sample",
    )(page_table, new_rows, pe1, pe2, w1, w2, pool)


def _rows8(row_chunks):
    rid = _row((8, 128))
    out = jnp.zeros((8, 128), f32)
    for r, c in enumerate(row_chunks):
        out = jnp.where(rid == r, jnp.broadcast_to(c, (8, 128)), out)
    return out


def _nsa_q8(qrow, h):
    chunks = []
    for g in range(4):
        hd = h * 4 + g
        chunks.append(qrow[:, (hd // 2) * 128:(hd // 2 + 1) * 128])
    q8 = _rows8(chunks)
    sw = pltpu.roll(q8, 64, 1)
    in_place = (_row((8, 128)) % 2) == h
    q8 = jnp.where(in_place, q8, sw)
    return jnp.where((_lane((8, 128)) // 64) == h, q8, 0.0)


def _nsa_sample_kernel(pt_ref, qraw_ref, qrot_ref, gate_ref, kvc_ref, snew_ref, wnew_ref, win_ref,
                       pool_ref, out_ref, nwin_ref, buf_ref, sbuf_ref, exp_ref, sem_ref,
                       *, layer, n_pages, n_batch, past_len):
    b = pl.program_id(0)
    slot = b % 2
    tk = n_pages * PAGE_SIZE

    def copy(bb, sl, p):
        return pltpu.make_async_copy(pool_ref.at[layer, pt_ref[bb, p]], buf_ref.at[sl, p], sem_ref.at[sl])

    def fetch(bb, sl):
        for p in range(n_pages):
            copy(bb, sl, p).start()

    @pl.when(b == 0)
    def _():
        fetch(0, 0)
        exp_ref[...] = jnp.where((_lane((256, tk)) // SEL_BLOCK) == _row((256, tk)), 1.0, 0.0).astype(bf16)

    @pl.when(b + 1 < n_batch)
    def _():
        fetch(b + 1, 1 - slot)

    qpos = past_len
    n_cmp = (past_len // SEL_BLOCK + 1) * SEL_BLOCK // CMP_STRIDE - 1
    n_sel = past_len // SEL_BLOCK + 1
    cur = qpos // SEL_BLOCK
    gs = _sigmoid(gate_ref[0])
    n_i = _lane((1, NC_PAD))
    maskc = ((16 * n_i + 31) <= qpos) & (n_i < n_cmp)
    nn = _row((NC_PAD, 256))
    j2 = _lane((NC_PAD, 256))
    cov = jnp.clip(jnp.minimum(16 * nn + 32, 64 * j2 + 64) - jnp.maximum(16 * nn, 64 * j2), 0, 32)
    cov = (jnp.where((nn < n_cmp) & (j2 < n_sel), cov, 0).astype(f32) * (1.0 / CMP_LEN)).astype(bf16)
    jj = _lane((1, 256))
    forced = (jj == 0) | (jj == cur) | (jj == cur - 1)
    eye = _row((256, 256)) == _lane((256, 256))
    kk = _row((256, 256))
    jjm = _lane((256, 256))
    lane128 = _lane((1, 128))

    qraw = qraw_ref[0]
    qrot = qrot_ref[0]
    o_cmp, sels, qos = [], [], []
    for h in range(NSA_KV_HEADS):
        kc = kvc_ref[0, :, 0:128].astype(bf16)
        vc = kvc_ref[0, :, 128:256].astype(bf16)
        qr = (_nsa_q8(qraw, h) * SCALE).astype(bf16)
        qos.append(_nsa_q8(qrot, h) * SCALE)
        p_c = _msoftmax(_dot_nt(qr, kc), maskc)
        o_cmp.append(_dot(p_c.astype(bf16), vc))
        psum = jnp.broadcast_to(p_c[0:1] + p_c[1:2] + p_c[2:3] + p_c[3:4], (8, NC_PAD))
        p_hi, p_lo = _split_hi_lo(psum)
        imp = (_dot(p_hi, cov) + _dot(p_lo, cov))[0:1]
        imp = jnp.where(forced, imp + FORCE_BONUS, imp)
        imp = jnp.where(jj <= cur, imp, -FORCE_BONUS)
        imp = jnp.where(jj < n_sel, imp, -3e38)
        imp_j = jnp.broadcast_to(imp, (256, 256))
        imp_k = jnp.broadcast_to(jnp.sum(jnp.where(eye, imp_j, 0.0), axis=1, keepdims=True), (256, 256))
        beats = (imp_k > imp_j) | ((imp_k == imp_j) & (kk < jjm))
        rank = jnp.sum(jnp.where(beats, 1.0, 0.0), axis=0, keepdims=True)
        sels.append(jnp.where((rank < SEL_TOPK) & (jj < n_sel), 1.0, 0.0))

    for p in range(n_pages):
        copy(b, slot, p).wait()

    snew = snew_ref[0]
    wnew = wnew_ref[0]
    o_all = []
    for h in range(NSA_KV_HEADS):
        qo = qos[h]
        qob = qo.astype(bf16)
        halfmask = (lane128 // 64) == h
        def score_body(p, _):
            kt = buf_ref[slot, p, 0].reshape(128, 128).astype(bf16)
            sbuf_ref[:, pl.ds(pl.multiple_of(p * 128, 128), 128)] = _dot(qob, kt)
            return 0
        lax.fori_loop(0, n_pages, score_body, 0)
        sel8 = jnp.broadcast_to(sels[h], (8, 256)).astype(bf16)
        mask = _dot(sel8, exp_ref[...]) > 0.5
        s = jnp.where(mask, sbuf_ref[...], NEG_INF)
        mask_new = sels[h][:, n_sel - 1:n_sel] > 0.5
        s_new = jnp.sum(qo * snew[:, 0:128], axis=-1, keepdims=True)
        s_new = jnp.where(mask_new, s_new, NEG_INF)
        mx = jnp.maximum(jnp.max(s, axis=-1, keepdims=True), s_new)
        e = jnp.where(mask, jnp.exp(s - mx), 0.0)
        e_new = jnp.where(mask_new, jnp.exp(s_new - mx), 0.0)
        inv = 1.0 / jnp.maximum(jnp.sum(e, axis=-1, keepdims=True) + e_new, 1e-30)
        sbuf_ref[...] = e

        def pv_body(p, acc):
            vt = buf_ref[slot, p, 1].reshape(128, 128).astype(bf16)
            pe = sbuf_ref[:, pl.ds(pl.multiple_of(p * 128, 128), 128)].astype(bf16)
            return acc + _dot_nt(pe, vt)
        acc = lax.fori_loop(0, n_pages, pv_body, jnp.zeros((8, 128), f32))
        o_slc = (acc + e_new * snew[:, 128:256]) * inv
        wt = win_ref[0, 0]
        kt = wt[0].reshape(128, WINDOW).astype(bf16)
        vt = wt[1].reshape(128, WINDOW).astype(bf16)
        maskw = _lane((1, WINDOW)) >= 1
        s_w = jnp.where(maskw, _dot(qob, kt), NEG_INF)
        sw_new = jnp.sum(qo * wnew[:, 0:128], axis=-1, keepdims=True)
        mx = jnp.maximum(jnp.max(s_w, axis=-1, keepdims=True), sw_new)
        e = jnp.where(maskw, jnp.exp(s_w - mx), 0.0)
        e_new = jnp.exp(sw_new - mx)
        inv = 1.0 / jnp.maximum(jnp.sum(e, axis=-1, keepdims=True) + e_new, 1e-30)
        o_win = (_dot_nt(e.astype(bf16), vt) + e_new * wnew[:, 128:256]) * inv
        gate_rows = []
        for c in range(3):
            gate_rows.append(_rows8([jnp.broadcast_to(gs[:, (h * 4 + g) * 3 + c:(h * 4 + g) * 3 + c + 1], (1, 128))
                                     for g in range(4)]))
        o8 = gate_rows[0] * o_cmp[h] + gate_rows[1] * o_slc + gate_rows[2] * o_win
        o_all.append(jnp.where(halfmask, o8, 0.0))

    lo = lane128 < 64
    for h in range(NSA_KV_HEADS):
        o8 = o_all[h]
        o8s = pltpu.roll(o8, 64, 1)
        low_src, high_src = (o8, o8s) if h == 0 else (o8s, o8)
        for gp in range(2):
            ch = jnp.where(lo, low_src[2 * gp:2 * gp + 1], high_src[2 * gp + 1:2 * gp + 2])
            out_ref[0, :, h * 256 + gp * 128:h * 256 + (gp + 1) * 128] = ch

    last = _lane((1, WINDOW)) == WINDOW - 1
    eye64 = _row((64, 64)) == _lane((64, 64))
    for kv in range(2):
        for h in range(NSA_KV_HEADS):
            c = kv * 2 + h
            newc = jnp.broadcast_to(wnew[:, c * 64:(c + 1) * 64], (64, 64))
            colv = jnp.sum(jnp.where(eye64, newc, 0.0), axis=1, keepdims=True)
            old = win_ref[0, 0, kv, h]
            nwin_ref[0, kv, h] = jnp.where(last, colv, pltpu.roll(old, WINDOW - 1, 1))


def _nsa_sample(page_table, qraw, qrot, gates, kvc, snew, wnew, win_t, pool_t, layer, past_len):
    n_batch, n_pages = page_table.shape
    tk = n_pages * PAGE_SIZE
    row = lambda w: pl.BlockSpec((1, 1, w), lambda b, pt: (b, 0, 0))
    grid_spec = pltpu.PrefetchScalarGridSpec(
        num_scalar_prefetch=1,
        grid=(n_batch,),
        in_specs=[row(512), row(512), row(128),
                  pl.BlockSpec((1, NC_PAD, 256), lambda b, pt: (b, 0, 0)), row(256), row(256),
                  pl.BlockSpec((1, 1, 2, 2, 64, WINDOW), lambda b, pt: (layer, b, 0, 0, 0, 0)),
                  pl.BlockSpec(memory_space=pl.ANY)],
        out_specs=[row(512), pl.BlockSpec((1, 2, 2, 64, WINDOW), lambda b, pt: (b, 0, 0, 0, 0))],
        scratch_shapes=[pltpu.VMEM((2, n_pages, 2, 2, 64, 128), f32), pltpu.VMEM((8, tk), f32),
                        pltpu.VMEM((256, tk), bf16), pltpu.SemaphoreType.DMA((2,))],
    )
    return pl.pallas_call(
        functools.partial(_nsa_sample_kernel, layer=layer, n_pages=n_pages, n_batch=n_batch,
                          past_len=past_len),
        grid_spec=grid_spec,
        out_shape=[jax.ShapeDtypeStruct((n_batch, 1, 512), f32),
                   jax.ShapeDtypeStruct((n_batch, 2, 2, 64, WINDOW), f32)],
        compiler_params=_cp(("arbitrary",)),
        name="nsa_sample",
    )(page_table, qraw, qrot, gates, kvc, snew, wnew, win_t, pool_t)


def _diff_sample_kernel(pt_ref, q_ref, new_ref, dl_ref, pool_ref, out_ref, buf_ref, m_ref, l_ref,
                        acc_ref, sem_ref, *, layer, n_pages, n_batch, n_split, lam_init):
    b = pl.program_id(0)
    hf = pl.program_id(1)
    step = b * n_split + hf
    slot = step % 2
    pps = n_pages // n_split
    rows = pps * PAGE_SIZE

    def copy(bb, hh, sl, p):
        return pltpu.make_async_copy(pool_ref.at[layer, pt_ref[bb, hh * pps + p]],
                                     buf_ref.at[sl, pl.ds(p * 4 * PAGE_SIZE, 4 * PAGE_SIZE), :],
                                     sem_ref.at[sl])

    def fetch(bb, hh, sl):
        for p in range(pps):
            copy(bb, hh, sl, p).start()

    @pl.when(step == 0)
    def _():
        fetch(0, 0, 0)

    @pl.when(step + 1 < n_batch * n_split)
    def _():
        nxt = step + 1
        fetch(nxt // n_split, nxt % n_split, 1 - slot)

    @pl.when(hf == 0)
    def _():
        m_ref[...] = jnp.full(m_ref.shape, NEG_INF, f32)
        l_ref[...] = jnp.zeros(l_ref.shape, f32)
        acc_ref[...] = jnp.zeros(acc_ref.shape, f32)

    for p in range(pps):
        copy(b, hf, slot, p).wait()

    qrow = q_ref[0]
    lane = _lane((8, 128))
    rid = _row((8, 128))
    q8s = []
    for h in range(2):
        q8 = _rows8([qrow[:, h * 256 + (r // 2) * 128:h * 256 + (r // 2 + 1) * 128] for r in range(4)])
        q8s.append(jnp.where((lane // 64) == (rid % 2), q8, 0.0) * SCALE)
    for h in range(2):
        k = buf_ref[slot, pl.ds(h, rows, stride=4), :].astype(bf16)
        v = buf_ref[slot, pl.ds(2 + h, rows, stride=4), :].astype(bf16)
        s = _dot_nt(q8s[h].astype(bf16), k)
        m_old = m_ref[h]
        m_new = jnp.maximum(m_old, jnp.max(s, axis=-1, keepdims=True))
        alpha = jnp.exp(m_old - m_new)
        p_ = jnp.exp(s - m_new)
        l_ref[h] = alpha * l_ref[h] + jnp.sum(p_, axis=-1, keepdims=True)
        acc_ref[h] = alpha * acc_ref[h] + _dot(p_.astype(bf16), v)
        m_ref[h] = m_new

    @pl.when(hf == n_split - 1)
    def _():
        lam = _diff_lambda(dl_ref[...], lam_init)
        new = new_ref[0]
        for h in range(2):
            s_new = jnp.sum(q8s[h] * new[:, h * 128:(h + 1) * 128], axis=-1, keepdims=True)
            m_old = m_ref[h]
            m_new = jnp.maximum(m_old, s_new)
            alpha = jnp.exp(m_old - m_new)
            p_new = jnp.exp(s_new - m_new)
            l_ = alpha * l_ref[h] + p_new
            acc = alpha * acc_ref[h] + p_new * new[:, 256 + h * 128:256 + (h + 1) * 128]
            o = acc * (1.0 / jnp.maximum(l_, 1e-30))
            for g in range(2):
                og = o[2 * g:2 * g + 1] - lam * o[2 * g + 1:2 * g + 2]
                out_ref[0, :, (h * 2 + g) * 128:(h * 2 + g + 1) * 128] = _rms_unit(og) * (1.0 - lam_init)


def _diff_sample(page_table, dqrot, new_rows, dl, pool, layer, lam_init):
    n_batch, n_pages = page_table.shape
    n_split = 2
    pps = n_pages // n_split
    row = lambda w: pl.BlockSpec((1, 1, w), lambda b, s, pt: (b, 0, 0))
    grid_spec = pltpu.PrefetchScalarGridSpec(
        num_scalar_prefetch=1,
        grid=(n_batch, n_split),
        in_specs=[row(512), row(512), pl.BlockSpec((4, 64), lambda b, s, pt: (0, 0)),
                  pl.BlockSpec(memory_space=pl.ANY)],
        out_specs=row(512),
        scratch_shapes=[pltpu.VMEM((2, pps * 4 * PAGE_SIZE, 128), f32), pltpu.VMEM((2, 8, 1), f32),
                        pltpu.VMEM((2, 8, 1), f32), pltpu.VMEM((2, 8, 128), f32),
                        pltpu.SemaphoreType.DMA((2,))],
    )
    return pl.pallas_call(
        functools.partial(_diff_sample_kernel, layer=layer, n_pages=n_pages, n_batch=n_batch,
                          n_split=n_split, lam_init=lam_init),
        grid_spec=grid_spec,
        out_shape=jax.ShapeDtypeStruct((n_batch, 1, 512), f32),
        compiler_params=_cp(("arbitrary", "arbitrary")),
        name="diff_sample",
    )(page_table, dqrot, new_rows, dl, pool)


def _prep_w_in(w):
    parts = jnp.split(w, [sum(IN_SPLITS[:i + 1]) for i in range(len(IN_SPLITS) - 1)], axis=-1)
    nq, ncmp, nslc, nwin, ngate, scb, scc, sch, dq, dk, dv, s5u = parts
    gate = jnp.pad(ngate, ((0, 0), (0, 128 - ngate.shape[1])))
    return jnp.concatenate([nq, scb, scc, sch, dq, s5u, ncmp, nslc, nwin, dk, dv, gate], axis=-1).astype(bf16)


def _prep_phi(pe, w):
    w2 = w.reshape(2, 2, CMP_STRIDE, HEAD_DIM, HEAD_DIM)
    wc = jnp.repeat(w2, 2, axis=0)
    eye = jnp.eye(4, dtype=f32)
    ws, pes = [], []
    for half in range(2):
        ws.append(jnp.einsum('csde,cf->scdfe', wc[:, half], eye).reshape(4096, 256).astype(bf16))
        pc = jnp.repeat(pe[:, half * CMP_STRIDE:(half + 1) * CMP_STRIDE], 2, axis=0)
        pes.append(jnp.transpose(pc, (1, 0, 2)).reshape(1, 4096))
    return pes[0], pes[1], ws[0], ws[1]


def _prep_s5(bbr, bbi, c_re, c_im):
    eye = jnp.eye(8, dtype=f32)

    def wb_of(bb):
        x = bb.reshape(S5_CH, S5_SLABS, 8, S5_STATE)
        return jnp.einsum('csgp,hg->shcgp', x, eye).reshape(S5_SLABS, 128, 512)

    wb = jnp.concatenate([wb_of(bbr), wb_of(bbi)], axis=-1).astype(bf16)

    def wc_of(c):
        x = c.reshape(S5_SLABS, 8, S5_CH, S5_STATE)
        return jnp.einsum('sgcp,hg->shpgc', x, eye).reshape(S5_SLABS, 512, 128)

    wc = jnp.concatenate([wc_of(c_re), -wc_of(c_im)], axis=1).astype(bf16)
    return wb, wc


def _rope_tables(pos):
    half = HEAD_DIM // 2
    inv = ROPE_THETA ** (-jnp.arange(half, dtype=f32) / half)
    ang = pos.astype(f32)[:, None] * inv[None, :]
    c, s = jnp.cos(ang), jnp.sin(ang)
    return jnp.tile(c, (1, 4)), jnp.tile(jnp.concatenate([-s, s], axis=1), (1, 2))


def kernel(x_prompt, x_sample, cache_nsa_cmp, cache_nsa_slc, cache_diff, state_nsa_win, state_sconv, state_s5_re, state_s5_im, state_ffn_conv, page_table, w_in, nsa_phi_pe, nsa_phi_w, sc_conv_w, diff_lambda, s5_a_re, s5_a_im, s5_log_dt, s5_b_re, s5_b_im, s5_c_re, s5_c_im, s5_d, s5_glu_w, s5_glu_b, mix_gain, w_out, ln1_g, ln1_b, ffn_w_up, ffn_conv_w, ffn_w_down, ln2_g, ln2_b):
    bp, t_len, _ = x_prompt.shape
    bs = x_sample.shape[0]
    n_pool = cache_nsa_cmp.shape[1]
    n_pages = page_table.shape[1]
    past_len = n_pages * PAGE_SIZE
    n_state = S5_GROUPS * S5_STATE
    mp = bp * t_len

    cos_p, sin_p = _rope_tables(jnp.arange(t_len))
    cos_s, sin_s = _rope_tables(jnp.full((bs,), past_len))

    pool_cmp = cache_nsa_cmp.reshape(DEPTH, n_pool, PAGE_SIZE // CMP_STRIDE, CMP_STRIDE * 256)
    pool_slc = jnp.transpose(cache_nsa_slc, (0, 1, 3, 4, 5, 2))
    pool_diff = cache_diff.reshape(DEPTH, n_pool, PAGE_SIZE * 4, 128)
    win_t = jnp.transpose(state_nsa_win, (0, 1, 3, 4, 5, 2))

    xp = x_prompt.reshape(mp, D_MODEL)
    xs = x_sample.reshape(bs, D_MODEL)
    xp_b, xs_b = xp.astype(bf16), xs.astype(bf16)

    outs_p = {k: [] for k in ('cmp', 'slc', 'win', 'diff', 'sc', 's5r', 's5i', 'ffn')}
    outs_s = {k: [] for k in ('cmp', 'slc', 'win', 'diff', 'sc', 's5r', 's5i', 'ffn')}

    for l in range(DEPTH):
        lam_init = 0.8 - 0.6 * math.exp(-0.3 * l)
        w_in_l = _prep_w_in(w_in[l])
        pe1, pe2, w1, w2 = _prep_phi(nsa_phi_pe[l], nsa_phi_w[l])
        tabs, bbr, bbi = _s5_prep(s5_a_re[l].reshape(1, n_state), s5_a_im[l].reshape(1, n_state),
                                  jnp.repeat(s5_log_dt[l], S5_STATE).reshape(1, n_state),
                                  jnp.transpose(s5_b_re[l], (2, 0, 1)).reshape(S5_CH, n_state),
                                  jnp.transpose(s5_b_im[l], (2, 0, 1)).reshape(S5_CH, n_state))
        wb5, wc5 = _prep_s5(bbr, bbi, s5_c_re[l], s5_c_im[l])
        d5 = s5_d[l].reshape(1, 512)
        gw = s5_glu_w[l].astype(bf16)
        gb = s5_glu_b[l].reshape(1, 512)
        gain = mix_gain[l].reshape(1, D_MODEL)
        wo = w_out[l].astype(bf16)
        g1, b1 = ln1_g[l].reshape(1, D_MODEL), ln1_b[l].reshape(1, D_MODEL)
        g2, b2 = ln2_g[l].reshape(1, D_MODEL), ln2_b[l].reshape(1, D_MODEL)
        wup = ffn_w_up[l].astype(bf16)
        wdn = ffn_w_down[l].astype(bf16)
        cwf = ffn_conv_w[l]
        scw = sc_conv_w[l]
        dl = diff_lambda[l]

        hcat = _in_proj(xp_b, w_in_l, 512)
        qrot, dqrot, kvslc, kvwin, kvdiff = _rope(hcat, cos_p, sin_p, 512)
        kvcmp = hcat[:, C_CMP:C_CMP + 256]
        kvc = _cmp_prompt(kvcmp.reshape(mp // CMP_STRIDE, CMP_STRIDE * 256), pe1, pe2, w1, w2, bp)
        nsa = _nsa_prompt(hcat, qrot, kvc, kvslc, kvwin, bp, t_len)
        dif = _diff_prompt(dqrot, kvdiff, dl, bp, t_len, lam_init)
        sc, sc_tail = _sconv_prompt(hcat, scw, bp, t_len)
        s5y, s5r, s5i = _s5_scan(hcat, wb5, wc5, d5, tabs, bp, t_len)
        x1, x1b = _mix(nsa, sc, dif, s5y, gw, gb, gain, wo, xp, g1, b1, 256)
        xp, xp_b, ffn_tail = _ffn_prompt(x1b, x1, wup, wdn, cwf, g2, b2, t_len)

        outs_p['cmp'].append(kvcmp.reshape(bp, t_len, 2, 2, HEAD_DIM))
        outs_p['slc'].append(kvslc.reshape(bp, t_len, 2, 2, HEAD_DIM))
        outs_p['win'].append(kvwin.reshape(bp, t_len, 2, 2, HEAD_DIM)[:, t_len - WINDOW:])
        outs_p['diff'].append(kvdiff.reshape(bp, t_len, 2, 2, 2 * HEAD_DIM))
        outs_p['sc'].append(sc_tail[:, 6:8])
        outs_p['s5r'].append(s5r.reshape(bp, S5_GROUPS, S5_STATE))
        outs_p['s5i'].append(s5i.reshape(bp, S5_GROUPS, S5_STATE))
        tiles_per_seq = ffn_tail.shape[0] // bp
        outs_p['ffn'].append(ffn_tail.reshape(bp, tiles_per_seq, 8, D_FF)[:, -1, 6:8])

        hs = _in_proj(xs_b, w_in_l, bs)
        qrot_s, dqrot_s, kvslc_s, kvwin_s, kvdiff_s = _rope(hs, cos_s, sin_s, bs)
        kvcmp_s = hs[:, C_CMP:C_CMP + 256]
        kvc_s = _cmp_sample(page_table, kvcmp_s.reshape(bs, 1, 256), pe1, pe2, w1, w2, pool_cmp, l)
        nsa_s, nwin_t = _nsa_sample(page_table, hs[:, C_NQ:C_NQ + 512].reshape(bs, 1, 512),
                                    qrot_s.reshape(bs, 1, 512),
                                    hs[:, C_GATE:C_GATE + 128].reshape(bs, 1, 128), kvc_s,
                                    kvslc_s.reshape(bs, 1, 256), kvwin_s.reshape(bs, 1, 256),
                                    win_t, pool_slc, l, past_len)
        dif_s = _diff_sample(page_table, dqrot_s.reshape(bs, 1, 512), kvdiff_s.reshape(bs, 1, 512),
                             dl, pool_diff, l, lam_init)
        scp = jnp.transpose(state_sconv[l], (1, 0, 2))
        sc_s, z_s, s5y_s, s5r_s, s5i_s = _sample_small(
            hs[:, C_SCB:C_SCB + 512], hs[:, C_SCC:C_SCC + 512], hs[:, C_SCH:C_SCH + 512], scw, scp,
            hs[:, C_S5U:C_S5U + 512], wb5, wc5, d5, tabs,
            state_s5_re[l].reshape(bs, n_state), state_s5_im[l].reshape(bs, n_state))
        x1s, x1sb = _mix(nsa_s.reshape(bs, 512), sc_s, dif_s.reshape(bs, 512), s5y_s, gw, gb, gain, wo,
                         xs, g1, b1, bs)
        prev_ffn = state_ffn_conv[l]
        xs, xs_b, aup_s = _ffn_sample(x1sb, x1s, prev_ffn[:, 0], prev_ffn[:, 1], wup, wdn, cwf, g2, b2)

        outs_s['cmp'].append(kvcmp_s.reshape(bs, 1, 2, 2, HEAD_DIM))
        outs_s['slc'].append(kvslc_s.reshape(bs, 1, 2, 2, HEAD_DIM))
        outs_s['win'].append(jnp.transpose(nwin_t, (0, 4, 1, 2, 3)))
        outs_s['diff'].append(kvdiff_s.reshape(bs, 1, 2, 2, 2 * HEAD_DIM))
        outs_s['sc'].append(jnp.stack([state_sconv[l][:, 1], z_s], axis=1))
        outs_s['s5r'].append(s5r_s.reshape(bs, S5_GROUPS, S5_STATE))
        outs_s['s5i'].append(s5i_s.reshape(bs, S5_GROUPS, S5_STATE))
        outs_s['ffn'].append(jnp.stack([prev_ffn[:, 1], aup_s], axis=1))

    order = ('cmp', 'slc', 'win', 'diff', 'sc', 's5r', 's5i', 'ffn')
    res = [xp.reshape(bp, t_len, D_MODEL), xs.reshape(bs, 1, D_MODEL)]
    res += [jnp.stack(outs_p[k], axis=0) for k in order]
    res += [jnp.stack(outs_s[k], axis=0) for k in order]
    return tuple(res)
```

```python
import functools
import math

import jax
import jax.numpy as jnp
from jax import lax
from jax.experimental import pallas as pl
from jax.experimental.pallas import tpu as pltpu

f32 = jnp.float32
bf16 = jnp.bfloat16

D_MODEL = 2048
DEPTH = 2
PAGE_SIZE = 128
HEAD_DIM = 64
GROUP_WIDTH = D_MODEL // 4
NSA_KV_HEADS = 2
NSA_GROUP = 4
CMP_STRIDE = 16
CMP_LEN = 32
SEL_BLOCK = 64
SEL_TOPK = 16
WINDOW = 512
FORCE_BONUS = 1e4
CONV_W = 3
S5_CH = 16
S5_GROUPS = 32
S5_STATE = 64
D_FF = 5632
ROPE_THETA = 10000.0
QBLOCK = 128
LN_EPS = 1e-5
RMS_EPS = 1e-6
NEG_INF = -1e30
DN_ALPHA = (2 * DEPTH) ** 0.25
SCALE = HEAD_DIM ** -0.5
LOG2E = math.log2(math.e)

IN_SPLITS = (512, 256, 256, 256, 24, 512, 512, 512, 512, 256, 256, 512)
C_NQ, C_SCB, C_SCC, C_SCH, C_DQ, C_S5U = 0, 512, 1024, 1536, 2048, 2560
C_CMP, C_SLC, C_WIN, C_DK, C_DV, C_GATE = 3072, 3328, 3584, 3840, 4096, 4352
HC = 4480

VMEM_CAP_V7X = 64 * 1024 * 1024
VMEM_LIMIT = 56 * 1024 * 1024
NC_PAD = 640
S5_SLABS = 4
KEY_CLASS_BLOCKS = 4
FFN_CHUNK = 256


def _cp(sem):
    return pltpu.CompilerParams(dimension_semantics=sem, vmem_limit_bytes=VMEM_LIMIT)


def _dot(a, b):
    return jnp.dot(a, b, preferred_element_type=f32)


def _dot_nt(a, b):
    return lax.dot_general(a, b, (((1,), (1,)), ((), ())), preferred_element_type=f32)


def _lane(shape):
    return lax.broadcasted_iota(jnp.int32, shape, len(shape) - 1)


def _row(shape):
    return lax.broadcasted_iota(jnp.int32, shape, len(shape) - 2)


def _msoftmax(s, mask):
    s = jnp.where(mask, s, NEG_INF)
    m = jnp.max(s, axis=-1, keepdims=True)
    e = jnp.where(mask, jnp.exp(s - m), 0.0)
    return e * (1.0 / jnp.maximum(jnp.sum(e, axis=-1, keepdims=True), 1e-30))


def _exp2_softmax(s, bias):
    s = s + bias[None]
    e = jnp.exp2(s - jnp.max(s, axis=-1, keepdims=True))
    return e, 1.0 / jnp.maximum(jnp.sum(e, axis=-1, keepdims=True), 1e-30)


def _sigmoid(x):
    return 1.0 / (1.0 + jnp.exp(-x))


def _rms_unit(x):
    return x * lax.rsqrt(jnp.mean(x * x, axis=-1, keepdims=True) + RMS_EPS)


def _layer_norm(z, g, b):
    mu = jnp.mean(z, axis=-1, keepdims=True)
    d = z - mu
    var = jnp.mean(d * d, axis=-1, keepdims=True)
    return d * lax.rsqrt(var + LN_EPS) * g + b


def _split_hi_lo(x):
    hi = x.astype(bf16)
    lo = (x - hi.astype(f32)).astype(bf16)
    return hi, lo


def _matmul_kernel(x_ref, w_ref, o_ref):
    o_ref[...] = _dot(x_ref[...], w_ref[...])


def _in_proj(xb, w, tm):
    m = xb.shape[0]
    tn = 640
    return pl.pallas_call(
        _matmul_kernel,
        grid=(m // tm, HC // tn),
        in_specs=[pl.BlockSpec((tm, D_MODEL), lambda i, j: (i, 0)),
                  pl.BlockSpec((D_MODEL, tn), lambda i, j: (0, j))],
        out_specs=pl.BlockSpec((tm, tn), lambda i, j: (i, j)),
        out_shape=jax.ShapeDtypeStruct((m, HC), f32),
        compiler_params=_cp(("parallel", "arbitrary")),
        name="in_proj",
    )(xb, w)


def _rope_cols(x, cos, sin):
    outs = []
    first = (_lane((1, 128)) % 64) < 32
    for c in range(x.shape[1] // 128):
        xc = x[:, c * 128:(c + 1) * 128]
        sw = jnp.where(first, pltpu.roll(xc, 96, 1), pltpu.roll(xc, 32, 1))
        outs.append(xc * cos + sw * sin)
    return outs


def _rope_kernel(nq_ref, dq_ref, slc_ref, win_ref, dk_ref, dv_ref, cos_ref, sin_ref,
                 qrot_ref, dqrot_ref, kvslc_ref, kvwin_ref, kvdiff_ref):
    cos = cos_ref[...]
    sin = sin_ref[...]
    for c, v in enumerate(_rope_cols(nq_ref[...], cos, sin)):
        qrot_ref[:, c * 128:(c + 1) * 128] = v
    for c, v in enumerate(_rope_cols(dq_ref[...], cos, sin)):
        dqrot_ref[:, c * 128:(c + 1) * 128] = v
    kvslc_ref[:, 0:128] = _rope_cols(slc_ref[:, 0:128], cos, sin)[0]
    kvslc_ref[:, 128:256] = slc_ref[:, 128:256]
    kvwin_ref[:, 0:128] = _rope_cols(win_ref[:, 0:128], cos, sin)[0]
    kvwin_ref[:, 128:256] = win_ref[:, 128:256]
    for c, v in enumerate(_rope_cols(dk_ref[...], cos, sin)):
        kvdiff_ref[:, c * 128:(c + 1) * 128] = v
    kvdiff_ref[:, 256:512] = dv_ref[...]


def _rope(hcat, cos, sin, tr):
    m = hcat.shape[0]
    nt = cos.shape[0] // tr

    def col(w, off):
        return pl.BlockSpec((tr, w), lambda i: (i, off // w))

    tab = pl.BlockSpec((tr, 128), lambda i: (i % nt, 0))
    return pl.pallas_call(
        _rope_kernel,
        grid=(m // tr,),
        in_specs=[col(512, C_NQ), col(512, C_DQ), col(256, C_SLC), col(256, C_WIN),
                  col(256, C_DK), col(256, C_DV), tab, tab],
        out_specs=[pl.BlockSpec((tr, 512), lambda i: (i, 0)),
                   pl.BlockSpec((tr, 512), lambda i: (i, 0)),
                   pl.BlockSpec((tr, 256), lambda i: (i, 0)),
                   pl.BlockSpec((tr, 256), lambda i: (i, 0)),
                   pl.BlockSpec((tr, 512), lambda i: (i, 0))],
        out_shape=[jax.ShapeDtypeStruct((m, 512), f32), jax.ShapeDtypeStruct((m, 512), f32),
                   jax.ShapeDtypeStruct((m, 256), f32), jax.ShapeDtypeStruct((m, 256), f32),
                   jax.ShapeDtypeStruct((m, 512), f32)],
        compiler_params=_cp(("parallel",)),
        name="rope",
    )(hcat, hcat, hcat, hcat, hcat, hcat, cos, sin)


def _cmp_prompt_kernel(z_ref, pe1_ref, pe2_ref, w1_ref, w2_ref, o_ref, sb_ref):
    z = z_ref[...]
    a = _dot((z + pe1_ref[...]).astype(bf16), w1_ref[...])
    bm = _dot((z + pe2_ref[...]).astype(bf16), w2_ref[...])
    n = z.shape[0]
    sb_ref[0:n, :] = bm
    sb_ref[n:n + 8, :] = jnp.zeros((8, 256), f32)
    o_ref[0] = a + sb_ref[1:n + 1, :]


def _cmp_prompt(z, pe1, pe2, w1, w2, bsz):
    n = z.shape[0] // bsz
    full = lambda shp: pl.BlockSpec(shp, lambda b: (0, 0))
    return pl.pallas_call(
        _cmp_prompt_kernel,
        grid=(bsz,),
        in_specs=[pl.BlockSpec((n, 4096), lambda b: (b, 0)), full((1, 4096)), full((1, 4096)),
                  full((4096, 256)), full((4096, 256))],
        out_specs=pl.BlockSpec((1, n, 256), lambda b: (b, 0, 0)),
        out_shape=jax.ShapeDtypeStruct((bsz, n, 256), f32),
        scratch_shapes=[pltpu.VMEM((n + 8, 256), f32)],
        compiler_params=_cp(("parallel",)),
        name="cmp_prompt",
    )(z, pe1, pe2, w1, w2)


def _nsa_qstack(blk, h):
    halfmask = (_lane((1, 128)) // 64) == h
    parts = []
    for g in range(4):
        c = blk[:, (g // 2) * 128:(g // 2 + 1) * 128]
        if g % 2 != h:
            c = pltpu.roll(c, 64, 1)
        parts.append(jnp.where(halfmask, c, 0.0))
    return jnp.concatenate(parts, axis=0)


def _nsa_assemble(o_list, h):
    lo = _lane((1, 128)) < 64
    chunks = []
    for gp in range(2):
        a, b = o_list[2 * gp], o_list[2 * gp + 1]
        if h == 0:
            b = pltpu.roll(b, 64, 1)
        else:
            a = pltpu.roll(a, 64, 1)
        chunks.append(jnp.where(lo, a, b))
    return chunks


def _nsa_prompt_body(qraw_ref, qrot_ref, gate_ref, kvc_ref, slc_ref, win_ref, out_ref, *, s0, kmax, t_len):
    qb = QBLOCK
    qpos = s0 + _row((qb, 1))
    gs = _sigmoid(gate_ref[...])
    n_i = _lane((1, 128))
    maskc = ((16 * n_i + 31) <= qpos) & (n_i < 127)
    nn = _row((128, 128))
    j2 = _lane((128, 128))
    cov = jnp.clip(jnp.minimum(16 * nn + 32, 64 * j2 + 64) - jnp.maximum(16 * nn, 64 * j2), 0, 32)
    cov = (jnp.where((nn < 127) & (j2 < 32), cov, 0).astype(f32) * (1.0 / CMP_LEN)).astype(bf16)
    expand = jnp.where((_lane((128, kmax)) // SEL_BLOCK) == _row((128, kmax)), 1.0, 0.0).astype(bf16)
    causal = _lane((1, kmax)) <= qpos
    wlen = WINDOW + qb
    start = pl.multiple_of(jnp.clip(s0 - WINDOW, 0, t_len - wlen), 128)
    wpos = start + _lane((1, wlen))
    bias_w = jnp.where((wpos <= qpos) & ((qpos - wpos) < WINDOW), 0.0, NEG_INF)
    jj = _lane((1, 128))
    cur = qpos // SEL_BLOCK
    n_sel = kmax // SEL_BLOCK
    forced = (jj == 0) | (jj == cur) | (jj == cur - 1)

    for h in range(NSA_KV_HEADS):
        kc = kvc_ref[0, :, 0:128].astype(bf16)
        vc = kvc_ref[0, :, 128:256].astype(bf16)
        qr = (_nsa_qstack(qraw_ref[:, h * 256:(h + 1) * 256], h) * SCALE).astype(bf16)
        qo = (_nsa_qstack(qrot_ref[:, h * 256:(h + 1) * 256], h) * (SCALE * LOG2E)).astype(bf16)
        s_c = _dot_nt(qr, kc).reshape(4, qb, 128)
        p_c = _msoftmax(s_c, maskc[None])
        o_cmp = _dot(p_c.reshape(4 * qb, 128).astype(bf16), vc)
        psum = p_c[0] + p_c[1] + p_c[2] + p_c[3]
        p_hi, p_lo = _split_hi_lo(psum)
        imp = _dot(p_hi, cov) + _dot(p_lo, cov)
        imp = jnp.where(forced, imp + FORCE_BONUS, imp)
        imp = jnp.where(jj <= cur, imp, -FORCE_BONUS)
        imp = jnp.where(jj < n_sel, imp, -3e38)
        rank = jnp.zeros((qb, 128), f32)
        for k in range(n_sel):
            col = imp[:, k:k + 1]
            beats = (col > imp) | ((col == imp) & (jj > k))
            rank = rank + jnp.where(beats, 1.0, 0.0)
        sel = jnp.where((rank < SEL_TOPK) & (jj < n_sel), 1.0, 0.0).astype(bf16)
        bias_s = jnp.where((_dot(sel, expand) > 0.5) & causal, 0.0, NEG_INF)
        ks = slc_ref[0:kmax, 0:128].astype(bf16)
        vs = slc_ref[0:kmax, 128:256].astype(bf16)
        e_s, inv_s = _exp2_softmax(_dot_nt(qo, ks).reshape(4, qb, kmax), bias_s)
        o_slc = _dot(e_s.reshape(4 * qb, kmax).astype(bf16), vs) * inv_s.reshape(4 * qb, 1)
        kw = win_ref[pl.ds(start, wlen), 0:128].astype(bf16)
        vw = win_ref[pl.ds(start, wlen), 128:256].astype(bf16)
        e_w, inv_w = _exp2_softmax(_dot_nt(qo, kw).reshape(4, qb, wlen), bias_w)
        o_win = _dot(e_w.reshape(4 * qb, wlen).astype(bf16), vw) * inv_w.reshape(4 * qb, 1)
        o_list = []
        for g in range(NSA_GROUP):
            gi = (h * NSA_GROUP + g) * 3
            r = slice(g * qb, (g + 1) * qb)
            o_list.append(gs[:, gi:gi + 1] * o_cmp[r] + gs[:, gi + 1:gi + 2] * o_slc[r]
                          + gs[:, gi + 2:gi + 3] * o_win[r])
        for gp, ch in enumerate(_nsa_assemble(o_list, h)):
            out_ref[:, h * 256 + gp * 128:h * 256 + (gp + 1) * 128] = ch


def _by_key_class(body, t_len):
    i = pl.program_id(1)
    span = KEY_CLASS_BLOCKS * QBLOCK
    for c in range(t_len // span):
        @pl.when(i // KEY_CLASS_BLOCKS == c)
        def _(c=c):
            body(s0=i * QBLOCK, kmax=(c + 1) * span)


def _nsa_prompt_kernel(*refs, t_len):
    _by_key_class(functools.partial(_nsa_prompt_body, *refs, t_len=t_len), t_len)


def _nsa_prompt(hcat, qrot, kvc, kvslc, kvwin, bsz, t_len):
    m = hcat.shape[0]
    nqb = t_len // QBLOCK
    return pl.pallas_call(
        functools.partial(_nsa_prompt_kernel, t_len=t_len),
        grid=(bsz, nqb),
        in_specs=[pl.BlockSpec((QBLOCK, 512), lambda b, i: (b * nqb + i, 0)),
                  pl.BlockSpec((QBLOCK, 512), lambda b, i: (b * nqb + i, 0)),
                  pl.BlockSpec((QBLOCK, 128), lambda b, i: (b * nqb + i, C_GATE // 128)),
                  pl.BlockSpec((1, 128, 256), lambda b, i: (b, 0, 0)),
                  pl.BlockSpec((t_len, 256), lambda b, i: (b, 0)),
                  pl.BlockSpec((t_len, 256), lambda b, i: (b, 0))],
        out_specs=pl.BlockSpec((QBLOCK, 512), lambda b, i: (b * nqb + i, 0)),
        out_shape=jax.ShapeDtypeStruct((m, 512), f32),
        compiler_params=_cp(("parallel", "arbitrary")),
        name="nsa_prompt",
    )(hcat, qrot, hcat, kvc, kvslc, kvwin)


def _diff_lambda(dl, lam_init):
    a = jnp.sum(dl[0:1, :] * dl[1:2, :], axis=-1, keepdims=True)
    b = jnp.sum(dl[2:3, :] * dl[3:4, :], axis=-1, keepdims=True)
    return jnp.exp(a) - jnp.exp(b) + lam_init


def _diff_prompt_body(dq_ref, kv_ref, dl_ref, out_ref, *, s0, kmax, lam_init):
    qb = QBLOCK
    qpos = s0 + _row((qb, 1))
    bias = jnp.where(_lane((1, kmax)) <= qpos, 0.0, NEG_INF)
    lam = _diff_lambda(dl_ref[...], lam_init)
    lane = _lane((1, 128))
    for h in range(2):
        k = kv_ref[0:kmax, h * 128:(h + 1) * 128].astype(bf16)
        v = kv_ref[0:kmax, 256 + h * 128:256 + (h + 1) * 128].astype(bf16)
        parts = []
        for g in range(2):
            c = dq_ref[:, h * 256 + g * 128:h * 256 + (g + 1) * 128] * (SCALE * LOG2E)
            for i in range(2):
                parts.append(jnp.where((lane // 64) == i, c, 0.0))
        q = jnp.concatenate(parts, axis=0).astype(bf16)
        e, inv = _exp2_softmax(_dot_nt(q, k).reshape(4, qb, kmax), bias)
        o = _dot(e.reshape(4 * qb, kmax).astype(bf16), v) * inv.reshape(4 * qb, 1)
        for g in range(2):
            og = o[2 * g * qb:(2 * g + 1) * qb] - lam * o[(2 * g + 1) * qb:(2 * g + 2) * qb]
            out_ref[:, (h * 2 + g) * 128:(h * 2 + g + 1) * 128] = _rms_unit(og) * (1.0 - lam_init)


def _diff_prompt_kernel(*refs, t_len, lam_init):
    _by_key_class(functools.partial(_diff_prompt_body, *refs, lam_init=lam_init), t_len)


def _diff_prompt(dqrot, kvdiff, dl, bsz, t_len, lam_init):
    m = dqrot.shape[0]
    nqb = t_len // QBLOCK
    return pl.pallas_call(
        functools.partial(_diff_prompt_kernel, t_len=t_len, lam_init=lam_init),
        grid=(bsz, nqb),
        in_specs=[pl.BlockSpec((QBLOCK, 512), lambda b, i: (b * nqb + i, 0)),
                  pl.BlockSpec((t_len, 512), lambda b, i: (b, 0)),
                  pl.BlockSpec((4, 64), lambda b, i: (0, 0))],
        out_specs=pl.BlockSpec((QBLOCK, 512), lambda b, i: (b * nqb + i, 0)),
        out_shape=jax.ShapeDtypeStruct((m, 512), f32),
        compiler_params=_cp(("parallel", "arbitrary")),
        name="diff_prompt",
    )(dqrot, kvdiff, dl)


def _sconv_prompt_kernel(b_ref, c_ref, h_ref, w_ref, out_ref, tail_ref, buf_ref, *, tr):
    t = pl.program_id(1)

    @pl.when(t == 0)
    def _():
        buf_ref[0:8, :] = jnp.zeros((8, 512), f32)

    z = c_ref[...] * h_ref[...]
    buf_ref[8:8 + tr, :] = z
    w = w_ref[...]
    y = w[0:1] * buf_ref[6:6 + tr, :] + w[1:2] * buf_ref[7:7 + tr, :] + w[2:3] * z
    out_ref[...] = _rms_unit(b_ref[...] * y)
    tail = z[tr - 8:tr]
    tail_ref[0] = tail
    buf_ref[0:8, :] = tail


def _sconv_prompt(hcat, w, bsz, t_len):
    tr = 512
    nt = t_len // tr
    m = hcat.shape[0]

    def col(off):
        return pl.BlockSpec((tr, 512), lambda b, t: (b * nt + t, off // 512))

    return pl.pallas_call(
        functools.partial(_sconv_prompt_kernel, tr=tr),
        grid=(bsz, nt),
        in_specs=[col(C_SCB), col(C_SCC), col(C_SCH), pl.BlockSpec((3, 512), lambda b, t: (0, 0))],
        out_specs=[pl.BlockSpec((tr, 512), lambda b, t: (b * nt + t, 0)),
                   pl.BlockSpec((1, 8, 512), lambda b, t: (b, 0, 0))],
        out_shape=[jax.ShapeDtypeStruct((m, 512), f32), jax.ShapeDtypeStruct((bsz, 8, 512), f32)],
        scratch_shapes=[pltpu.VMEM((8 + tr, 512), f32)],
        compiler_params=_cp(("parallel", "arbitrary")),
        name="sconv_prompt",
    )(hcat, hcat, hcat, w)


def _cmul(ar, ai, br, bi):
    return ar * br - ai * bi, ar * bi + ai * br


def _s5_prep_kernel(ar_ref, ai_ref, ldt_ref, br_ref, bi_ref, tab_ref, bbr_ref, bbi_ref):
    ar, ai = ar_ref[...], ai_ref[...]
    dt = jnp.exp(ldt_ref[...])
    mag = jnp.exp(ar * dt)
    abr, abi = mag * jnp.cos(ai * dt), mag * jnp.sin(ai * dt)
    den = ar * ar + ai * ai
    nr, ni = abr - 1.0, abi
    cre = (nr * ar + ni * ai) / den
    cim = (ni * ar - nr * ai) / den
    br, bi = br_ref[...], bi_ref[...]
    bbr_ref[...] = cre * br - cim * bi
    bbi_ref[...] = cre * bi + cim * br
    pw = [(abr, abi)]
    for _ in range(7):
        pw.append(_cmul(pw[-1][0], pw[-1][1], abr, abi))
    n = ar.shape[1]
    row = _row((8, n))
    zero = jnp.zeros((8, n), f32)
    for idx, (sh, p) in enumerate(((1, pw[0]), (2, pw[1]), (4, pw[3]))):
        tab_ref[2 * idx] = jnp.where(row >= sh, jnp.broadcast_to(p[0], (8, n)), zero)
        tab_ref[2 * idx + 1] = jnp.where(row >= sh, jnp.broadcast_to(p[1], (8, n)), zero)
    pr, pi = zero, zero
    for i in range(8):
        pr = jnp.where(row == i, jnp.broadcast_to(pw[i][0], (8, n)), pr)
        pi = jnp.where(row == i, jnp.broadcast_to(pw[i][1], (8, n)), pi)
    tab_ref[6] = pr
    tab_ref[7] = pi


def _s5_prep(ar, ai, ldt, br, bi):
    n = S5_GROUPS * S5_STATE
    return pl.pallas_call(
        _s5_prep_kernel,
        out_shape=[jax.ShapeDtypeStruct((8, 8, n), f32), jax.ShapeDtypeStruct((S5_CH, n), f32),
                   jax.ShapeDtypeStruct((S5_CH, n), f32)],
        name="s5_prep",
    )(ar, ai, ldt, br, bi)


def _s5_scan_kernel(u_ref, wb_ref, wc_ref, d_ref, tab_ref, y_ref, hr_ref, hi_ref,
                    xbuf_ref, cr_ref, ci_ref, *, tt):
    t = pl.program_id(2)

    @pl.when(t == 0)
    def _():
        cr_ref[...] = jnp.zeros((8, 512), f32)
        ci_ref[...] = jnp.zeros((8, 512), f32)

    u = u_ref[...]
    xbuf_ref[...] = _dot(u.astype(bf16), wb_ref[0])

    def body(r, carry):
        cr, ci = carry
        rows = pl.ds(pl.multiple_of(r * 8, 8), 8)
        xr = xbuf_ref[rows, 0:512]
        xi = xbuf_ref[rows, 512:1024]
        for idx, sh in enumerate((1, 2, 4)):
            a_r, a_i = tab_ref[2 * idx], tab_ref[2 * idx + 1]
            sr, si = pltpu.roll(xr, sh, 0), pltpu.roll(xi, sh, 0)
            xr, xi = xr + a_r * sr - a_i * si, xi + a_r * si + a_i * sr
        p_r, p_i = tab_ref[6], tab_ref[7]
        hr = xr + p_r * cr - p_i * ci
        hi = xi + p_r * ci + p_i * cr
        xbuf_ref[rows, 0:512] = hr
        xbuf_ref[rows, 512:1024] = hi
        return (jnp.broadcast_to(hr[7:8, :], (8, 512)), jnp.broadcast_to(hi[7:8, :], (8, 512)))

    cr, ci = lax.fori_loop(0, tt // 8, body, (cr_ref[...], ci_ref[...]))
    cr_ref[...] = cr
    ci_ref[...] = ci
    hr_ref[0] = cr[0:1, :]
    hi_ref[0] = ci[0:1, :]
    y_ref[...] = _dot(xbuf_ref[...].astype(bf16), wc_ref[0]) + d_ref[...] * u


def _s5_scan(hcat, wb, wc, d, tabs, bsz, t_len):
    tt = 512
    nt = t_len // tt
    m = hcat.shape[0]
    n = S5_GROUPS * S5_STATE
    return pl.pallas_call(
        functools.partial(_s5_scan_kernel, tt=tt),
        grid=(bsz, S5_SLABS, nt),
        in_specs=[pl.BlockSpec((tt, 128), lambda b, s, t: (b * nt + t, C_S5U // 128 + s)),
                  pl.BlockSpec((1, 128, 1024), lambda b, s, t: (s, 0, 0)),
                  pl.BlockSpec((1, 1024, 128), lambda b, s, t: (s, 0, 0)),
                  pl.BlockSpec((1, 128), lambda b, s, t: (0, s)),
                  pl.BlockSpec((8, 8, 512), lambda b, s, t: (0, 0, s))],
        out_specs=[pl.BlockSpec((tt, 128), lambda b, s, t: (b * nt + t, s)),
                   pl.BlockSpec((1, 1, 512), lambda b, s, t: (b, 0, s)),
                   pl.BlockSpec((1, 1, 512), lambda b, s, t: (b, 0, s))],
        out_shape=[jax.ShapeDtypeStruct((m, 512), f32), jax.ShapeDtypeStruct((bsz, 1, n), f32),
                   jax.ShapeDtypeStruct((bsz, 1, n), f32)],
        scratch_shapes=[pltpu.VMEM((tt, 1024), f32), pltpu.VMEM((8, 512), f32), pltpu.VMEM((8, 512), f32)],
        compiler_params=_cp(("parallel", "parallel", "arbitrary")),
        name="s5_scan",
    )(hcat, wb, wc, d, tabs)


def _sample_small_kernel(scb_ref, scc_ref, sch_ref, scw_ref, scp_ref, u_ref, wb_ref, wc_ref, d_ref,
                         tab_ref, h0r_ref, h0i_ref, sc_ref, z_ref, y_ref, hr_ref, hi_ref):
    z = scc_ref[...] * sch_ref[...]
    w = scw_ref[...]
    y = w[0:1] * scp_ref[0] + w[1:2] * scp_ref[1] + w[2:3] * z
    sc_ref[...] = _rms_unit(scb_ref[...] * y)
    z_ref[...] = z
    u = u_ref[...]
    for s in range(S5_SLABS):
        x = _dot(u[:, s * 128:(s + 1) * 128].astype(bf16), wb_ref[s])
        lanes = slice(s * 512, (s + 1) * 512)
        a_r, a_i = tab_ref[6, 0:1, lanes], tab_ref[7, 0:1, lanes]
        h0r, h0i = h0r_ref[:, lanes], h0i_ref[:, lanes]
        hr = a_r * h0r - a_i * h0i + x[:, 0:512]
        hi = a_r * h0i + a_i * h0r + x[:, 512:1024]
        hr_ref[:, lanes] = hr
        hi_ref[:, lanes] = hi
        hcat = jnp.concatenate([hr, hi], axis=1).astype(bf16)
        cols = slice(s * 128, (s + 1) * 128)
        y_ref[:, cols] = _dot(hcat, wc_ref[s]) + d_ref[:, cols] * u[:, cols]


def _sample_small(scb, scc, sch, scw, scp, u, wb, wc, d, tabs, h0r, h0i):
    bsz = u.shape[0]
    n = S5_GROUPS * S5_STATE
    return pl.pallas_call(
        _sample_small_kernel,
        out_shape=[jax.ShapeDtypeStruct((bsz, 512), f32), jax.ShapeDtypeStruct((bsz, 512), f32),
                   jax.ShapeDtypeStruct((bsz, 512), f32), jax.ShapeDtypeStruct((bsz, n), f32),
                   jax.ShapeDtypeStruct((bsz, n), f32)],
        compiler_params=pltpu.CompilerParams(vmem_limit_bytes=VMEM_LIMIT),
        name="sample_small",
    )(scb, scc, sch, scw, scp, u, wb, wc, d, tabs, h0r, h0i)


def _gelu(x):
    return 0.5 * x * (1.0 + jnp.tanh(math.sqrt(2.0 / math.pi) * (x + 0.044715 * (x * x * x))))


def _mix_kernel(nsa_ref, sc_ref, diff_ref, s5_ref, gw_ref, gb_ref, gain_ref, wo_ref, x_ref,
                g_ref, b_ref, out_ref, outb_ref):
    y = _gelu(s5_ref[...])
    s5o = y * _sigmoid(_dot(y.astype(bf16), gw_ref[...]) + gb_ref[...])
    parts = (_rms_unit(nsa_ref[...]), sc_ref[...], diff_ref[...], _rms_unit(s5o))
    acc = None
    for k, p in enumerate(parts):
        pk = (p * gain_ref[:, k * 512:(k + 1) * 512]).astype(bf16)
        d = _dot(pk, wo_ref[k * 512:(k + 1) * 512, :])
        acc = d if acc is None else acc + d
    o = _layer_norm(DN_ALPHA * x_ref[...] + acc, g_ref[...], b_ref[...])
    out_ref[...] = o
    outb_ref[...] = o.astype(bf16)


def _mix(nsa, sc, diff, s5y, gw, gb, gain, wo, x, g, b, tm, layer):
    m = x.shape[0]
    row = lambda w: pl.BlockSpec((tm, w), lambda i: (i, 0))
    full = lambda shp: pl.BlockSpec(shp, lambda i: (0, 0))
    stacked = lambda shp: pl.BlockSpec((None,) + shp, lambda i: (layer, 0, 0))
    return pl.pallas_call(
        _mix_kernel,
        grid=(m // tm,),
        in_specs=[row(512), row(512), row(512), row(512), stacked((512, 512)), full((1, 512)),
                  full((1, D_MODEL)), stacked((D_MODEL, D_MODEL)), row(D_MODEL), full((1, D_MODEL)),
                  full((1, D_MODEL))],
        out_specs=[row(D_MODEL), row(D_MODEL)],
        out_shape=[jax.ShapeDtypeStruct((m, D_MODEL), f32), jax.ShapeDtypeStruct((m, D_MODEL), bf16)],
        compiler_params=_cp(("parallel",)),
        name="mix_outproj_ln",
    )(nsa, sc, diff, s5y, gw, gb, gain, wo, x, g, b)


def _ffn_tail(acc_ref, xres_ref, g_ref, b_ref, out_ref, outb_ref):
    o = _layer_norm(DN_ALPHA * xres_ref[...] + acc_ref[...], g_ref[...], b_ref[...])
    out_ref[...] = o
    outb_ref[...] = o.astype(bf16)


def _ffn_prompt_kernel(x_ref, halo_ref, wa_ref, wb_ref, wd_ref, cw_ref, xres_ref, g_ref, b_ref,
                       out_ref, outb_ref, tail_ref, acc_ref, abuf_ref, *, tm, nf, tiles_per_seq):
    i = pl.program_id(0)
    f = pl.program_id(1)

    @pl.when(f == 0)
    def _():
        acc_ref[...] = jnp.zeros_like(acc_ref)

    x = x_ref[...]
    halo = halo_ref[...]
    keep = jnp.where(i % tiles_per_seq != 0, 1.0, 0.0)
    cw = cw_ref[...]
    down = None
    for c in range(abuf_ref.shape[0]):
        cols = slice(c * FFN_CHUNK, (c + 1) * FFN_CHUNK)
        a = _dot(x, wa_ref[:, cols])
        bb = _dot(x, wb_ref[:, cols])
        abuf_ref[c, 0:16, :] = _dot(halo, wa_ref[:, cols]) * keep
        abuf_ref[c, 16:16 + tm, :] = a
        ac = (cw[0:1, cols] * abuf_ref[c, 14:14 + tm, :] + cw[1:2, cols] * abuf_ref[c, 15:15 + tm, :]
              + cw[2:3, cols] * a)
        gate = (ac * _sigmoid(ac) * bb).astype(bf16)
        d = _dot(gate, wd_ref[cols, :])
        down = d if down is None else down + d
        tail_ref[0, :, cols] = a[tm - 8:tm]
    acc_ref[...] += down

    @pl.when(f == nf - 1)
    def _():
        _ffn_tail(acc_ref, xres_ref, g_ref, b_ref, out_ref, outb_ref)


def _ffn_prompt(xb, x, wup, wd, cw, g, b, t_len, layer):
    m = x.shape[0]
    tm, tf = 512, 512
    nf = D_FF // tf
    full = lambda shp: pl.BlockSpec(shp, lambda i, f: (0, 0))
    return pl.pallas_call(
        functools.partial(_ffn_prompt_kernel, tm=tm, nf=nf, tiles_per_seq=t_len // tm),
        grid=(m // tm, nf),
        in_specs=[pl.BlockSpec((tm, D_MODEL), lambda i, f: (i, 0)),
                  pl.BlockSpec((16, D_MODEL), lambda i, f: (jnp.maximum(i * (tm // 16) - 1, 0), 0)),
                  pl.BlockSpec((None, D_MODEL, tf), lambda i, f: (layer, 0, f)),
                  pl.BlockSpec((None, D_MODEL, tf), lambda i, f: (layer, 0, f + nf)),
                  pl.BlockSpec((None, tf, D_MODEL), lambda i, f: (layer, f, 0)),
                  pl.BlockSpec((3, tf), lambda i, f: (0, f)),
                  pl.BlockSpec((tm, D_MODEL), lambda i, f: (i, 0)),
                  full((1, D_MODEL)), full((1, D_MODEL))],
        out_specs=[pl.BlockSpec((tm, D_MODEL), lambda i, f: (i, 0)),
                   pl.BlockSpec((tm, D_MODEL), lambda i, f: (i, 0)),
                   pl.BlockSpec((1, 8, tf), lambda i, f: (i, 0, f))],
        out_shape=[jax.ShapeDtypeStruct((m, D_MODEL), f32), jax.ShapeDtypeStruct((m, D_MODEL), bf16),
                   jax.ShapeDtypeStruct((m // tm, 8, D_FF), f32)],
        scratch_shapes=[pltpu.VMEM((tm, D_MODEL), f32), pltpu.VMEM((tf // FFN_CHUNK, 16 + tm, FFN_CHUNK), f32)],
        compiler_params=_cp(("parallel", "arbitrary")),
        name="ffn_prompt",
    )(xb, xb, wup, wup, wd, cw, x, g, b)


def _ffn_sample_kernel(x_ref, p0_ref, p1_ref, wa_ref, wb_ref, wd_ref, cw_ref, xres_ref, g_ref, b_ref,
                       out_ref, outb_ref, aup_ref, acc_ref, *, nf):
    f = pl.program_id(0)

    @pl.when(f == 0)
    def _():
        acc_ref[...] = jnp.zeros_like(acc_ref)

    x = x_ref[...]
    a = _dot(x, wa_ref[...])
    bb = _dot(x, wb_ref[...])
    cw = cw_ref[...]
    ac = cw[0:1] * p0_ref[...] + cw[1:2] * p1_ref[...] + cw[2:3] * a
    gate = (ac * _sigmoid(ac) * bb).astype(bf16)
    acc_ref[...] += _dot(gate, wd_ref[...])
    aup_ref[...] = a

    @pl.when(f == nf - 1)
    def _():
        _ffn_tail(acc_ref, xres_ref, g_ref, b_ref, out_ref, outb_ref)


def _ffn_sample(xb, x, p0, p1, wup, wd, cw, g, b, layer):
    m = x.shape[0]
    tf = 512
    nf = D_FF // tf
    full = lambda shp: pl.BlockSpec(shp, lambda f: (0, 0))
    return pl.pallas_call(
        functools.partial(_ffn_sample_kernel, nf=nf),
        grid=(nf,),
        in_specs=[full((m, D_MODEL)), pl.BlockSpec((m, tf), lambda f: (0, f)),
                  pl.BlockSpec((m, tf), lambda f: (0, f)),
                  pl.BlockSpec((None, D_MODEL, tf), lambda f: (layer, 0, f)),
                  pl.BlockSpec((None, D_MODEL, tf), lambda f: (layer, 0, f + nf)),
                  pl.BlockSpec((None, tf, D_MODEL), lambda f: (layer, f, 0)),
                  pl.BlockSpec((3, tf), lambda f: (0, f)),
                  full((m, D_MODEL)), full((1, D_MODEL)), full((1, D_MODEL))],
        out_specs=[full((m, D_MODEL)), full((m, D_MODEL)), pl.BlockSpec((m, tf), lambda f: (0, f))],
        out_shape=[jax.ShapeDtypeStruct((m, D_MODEL), f32), jax.ShapeDtypeStruct((m, D_MODEL), bf16),
                   jax.ShapeDtypeStruct((m, D_FF), f32)],
        scratch_shapes=[pltpu.VMEM((m, D_MODEL), f32)],
        compiler_params=_cp(("arbitrary",)),
        name="ffn_sample",
    )(xb, p0, p1, wup, wup, wd, cw, x, g, b)


def _cmp_sel_sample_kernel(pt_ref, new_ref, qraw_ref, pe1_ref, pe2_ref, w1_ref, w2_ref, pool_ref,
                           ocmp_ref, idx_ref, buf_ref, x_ref, sb_ref, kvc_ref, cst_ref, sem_ref,
                           *, layer, n_pages, n_batch, past_len):
    b = pl.program_id(0)
    slot = b % 2
    n = n_pages * (PAGE_SIZE // CMP_STRIDE)

    def copy(bb, sl, p):
        return pltpu.make_async_copy(pool_ref.at[layer, pt_ref[bb, p]], buf_ref.at[sl, p], sem_ref.at[sl])

    def fetch(bb, sl):
        for p in range(n_pages):
            copy(bb, sl, p).start()

    def pe_of(ref, s):
        return ref[:, s * 256:(s + 1) * 256]

    @pl.when(b == 0)
    def _():
        fetch(0, 0)
        for idx, (pe_ref, w_ref) in enumerate(((pe1_ref, w1_ref), (pe2_ref, w2_ref))):
            tot = None
            for s in range(CMP_STRIDE):
                pes = jnp.broadcast_to(pe_of(pe_ref, s), (8, 256)).astype(bf16)
                d = _dot(pes, w_ref[s * 256:(s + 1) * 256, :])
                if s == 0:
                    cst_ref[2 * idx + 1] = d
                tot = d if tot is None else tot + d
            cst_ref[2 * idx] = tot

    @pl.when(b + 1 < n_batch)
    def _():
        fetch(b + 1, 1 - slot)

    for p in range(n_pages):
        copy(b, slot, p).wait()

    def tr_body(p, carry):
        for kv in range(2):
            tile = buf_ref[slot, p, kv].reshape(128, 128)
            x_ref[kv, pl.ds(pl.multiple_of(p * PAGE_SIZE, PAGE_SIZE), PAGE_SIZE), :] = tile.T
        return carry

    lax.fori_loop(0, n_pages, tr_body, 0)

    a = bm = None
    for s in range(CMP_STRIDE):
        xs = jnp.concatenate([x_ref[kv, pl.ds(s, n, stride=CMP_STRIDE), :] for kv in range(2)], axis=1)
        da = _dot((xs + pe_of(pe1_ref, s)).astype(bf16), w1_ref[s * 256:(s + 1) * 256, :])
        db = _dot((xs + pe_of(pe2_ref, s)).astype(bf16), w2_ref[s * 256:(s + 1) * 256, :])
        a = da if a is None else a + da
        bm = db if bm is None else bm + db
    new8 = jnp.broadcast_to(new_ref[0], (8, 256))
    row0 = _row((8, 256)) == 0
    a_new0 = _dot((new8 + pe_of(pe1_ref, 0)).astype(bf16), w1_ref[0:256, :]) - cst_ref[1]
    b_new0 = _dot((new8 + pe_of(pe2_ref, 0)).astype(bf16), w2_ref[0:256, :]) - cst_ref[3]
    a_new = cst_ref[0] + jnp.where(row0, a_new0, 0.0)
    b_new = cst_ref[2] + jnp.where(row0, b_new0, 0.0)
    sb_ref[0:n, :] = bm
    sb_ref[n:n + 8, :] = b_new
    sb_ref[n + 8:n + 16, :] = jnp.zeros((8, 256), f32)
    kvc_ref[0:n, :] = a + sb_ref[1:n + 1, :]
    kvc_ref[n:n + 8, :] = a_new + sb_ref[n + 1:n + 9, :]
    kvc_ref[n + 8:NC_PAD, :] = jnp.zeros((NC_PAD - n - 8, 256), f32)

    qpos = past_len
    n_sel = past_len // SEL_BLOCK + 1
    n_cmp = n_sel * SEL_BLOCK // CMP_STRIDE - 1
    cur = qpos // SEL_BLOCK
    n_i = _lane((1, NC_PAD))
    maskc = ((16 * n_i + 31) <= qpos) & (n_i < n_cmp)
    nn = _row((NC_PAD, 256))
    j2 = _lane((NC_PAD, 256))
    cov = jnp.clip(jnp.minimum(16 * nn + 32, 64 * j2 + 64) - jnp.maximum(16 * nn, 64 * j2), 0, 32)
    cov = (jnp.where((nn < n_cmp) & (j2 < n_sel), cov, 0).astype(f32) * (1.0 / CMP_LEN)).astype(bf16)
    jj = _lane((1, 256))
    forced = (jj == 0) | (jj == cur) | (jj == cur - 1)
    kk = _row((256, 256))
    jjm = _lane((256, 256))
    eye = kk == jjm
    before = jnp.where(kk < jjm, 1.0, 0.0).astype(bf16)
    slot_id = _row((SEL_TOPK, 256))
    jj16 = _lane((SEL_TOPK, 256)).astype(f32)
    qraw = qraw_ref[0]
    for h in range(NSA_KV_HEADS):
        kc = kvc_ref[:, 0:128].astype(bf16)
        vc = kvc_ref[:, 128:256].astype(bf16)
        qr = (_nsa_q8(qraw, h, h) * SCALE).astype(bf16)
        p_c = _msoftmax(_dot_nt(qr, kc), maskc)
        ocmp_ref[0, h] = _dot(p_c.astype(bf16), vc)
        psum = jnp.broadcast_to(p_c[0:1] + p_c[1:2] + p_c[2:3] + p_c[3:4], (8, NC_PAD))
        p_hi, p_lo = _split_hi_lo(psum)
        imp = (_dot(p_hi, cov) + _dot(p_lo, cov))[0:1]
        imp = jnp.where(forced, imp + FORCE_BONUS, imp)
        imp = jnp.where(jj <= cur, imp, -FORCE_BONUS)
        imp = jnp.where(jj < n_sel, imp, -3e38)
        imp_j = jnp.broadcast_to(imp, (256, 256))
        imp_k = jnp.broadcast_to(jnp.sum(jnp.where(eye, imp_j, 0.0), axis=1, keepdims=True), (256, 256))
        beats = (imp_k > imp_j) | ((imp_k == imp_j) & (kk < jjm))
        rank = jnp.sum(jnp.where(beats, 1.0, 0.0), axis=0, keepdims=True)
        sel = jnp.where((rank < SEL_TOPK) & (jj < n_sel), 1.0, 0.0)
        pos = _dot(jnp.broadcast_to(sel, (8, 256)).astype(bf16), before)[0:1]
        hit = (jnp.broadcast_to(pos, (SEL_TOPK, 256)) == slot_id.astype(f32)) & (jnp.broadcast_to(sel, (SEL_TOPK, 256)) > 0.5)
        ids = jnp.sum(jnp.where(hit, jj16, 0.0), axis=1, keepdims=True)
        idx_ref[0, h * SEL_TOPK:(h + 1) * SEL_TOPK, :] = jnp.broadcast_to(ids, (SEL_TOPK, 128)).astype(jnp.int32)


def _cmp_sel_sample(page_table, new_rows, qraw, pe1, pe2, w1, w2, pool_t, layer, past_len):
    n_batch, n_pages = page_table.shape
    n = n_pages * (PAGE_SIZE // CMP_STRIDE)
    full = lambda shp: pl.BlockSpec(shp, lambda b, pt: (0,) * len(shp))
    grid_spec = pltpu.PrefetchScalarGridSpec(
        num_scalar_prefetch=1,
        grid=(n_batch,),
        in_specs=[pl.BlockSpec((1, 1, 256), lambda b, pt: (b, 0, 0)),
                  pl.BlockSpec((1, 1, 512), lambda b, pt: (b, 0, 0)), full((1, 4096)), full((1, 4096)),
                  full((4096, 256)), full((4096, 256)), pl.BlockSpec(memory_space=pl.ANY)],
        out_specs=[pl.BlockSpec((1, 2, 8, 128), lambda b, pt: (b, 0, 0, 0)),
                   pl.BlockSpec((1, 2 * SEL_TOPK, 128), lambda b, pt: (b, 0, 0))],
        scratch_shapes=[pltpu.VMEM((2, n_pages, 2, 2, HEAD_DIM, PAGE_SIZE), f32),
                        pltpu.VMEM((2, n_pages * PAGE_SIZE, 128), f32), pltpu.VMEM((n + 16, 256), f32),
                        pltpu.VMEM((NC_PAD, 256), f32), pltpu.VMEM((4, 8, 256), f32),
                        pltpu.SemaphoreType.DMA((2,))],
    )
    return pl.pallas_call(
        functools.partial(_cmp_sel_sample_kernel, layer=layer, n_pages=n_pages, n_batch=n_batch,
                          past_len=past_len),
        grid_spec=grid_spec,
        out_shape=[jax.ShapeDtypeStruct((n_batch, 2, 8, 128), f32),
                   jax.ShapeDtypeStruct((n_batch, 2 * SEL_TOPK, 128), jnp.int32)],
        compiler_params=_cp(("arbitrary",)),
        name="cmp_sel_sample",
    )(page_table, new_rows, qraw, pe1, pe2, w1, w2, pool_t)


def _rows8(row_chunks):
    rid = _row((8, 128))
    out = jnp.zeros((8, 128), f32)
    for r, c in enumerate(row_chunks):
        out = jnp.where(rid == r, jnp.broadcast_to(c, (8, 128)), out)
    return out


def _nsa_q8(qrow, h, half):
    chunks = []
    for g in range(4):
        hd = h * 4 + g
        chunks.append(qrow[:, (hd // 2) * 128:(hd // 2 + 1) * 128])
    q8 = _rows8(chunks)
    sw = pltpu.roll(q8, 64, 1)
    in_place = (_row((8, 128)) % 2) == half
    q8 = jnp.where(in_place, q8, sw)
    return jnp.where((_lane((8, 128)) // 64) == half, q8, 0.0)


def _nsa_sel_sample_kernel(pt_ref, idx_ref, qrot_ref, gate_ref, ocmp_ref, snew_ref, wnew_ref, win_ref,
                           pool_ref, out_ref, nwin_ref, buf_ref, sem_ref, *, layer, n_batch, n_past_blocks):
    b = pl.program_id(0)
    slot = b % 2

    def copy(bb, sl, h, s, kv):
        j = jnp.minimum(idx_ref[bb, h * SEL_TOPK + s], n_past_blocks - 1)
        page = pt_ref[bb, j // (PAGE_SIZE // SEL_BLOCK)]
        return pltpu.make_async_copy(pool_ref.at[layer, page, kv], buf_ref.at[sl, h, kv, s], sem_ref.at[sl])

    def for_all(bb, sl, fn):
        for h in range(NSA_KV_HEADS):
            for s in range(SEL_TOPK):
                for kv in range(2):
                    fn(copy(bb, sl, h, s, kv))

    @pl.when(b == 0)
    def _():
        for_all(0, 0, lambda c: c.start())

    @pl.when(b + 1 < n_batch)
    def _():
        for_all(b + 1, 1 - slot, lambda c: c.start())

    gs = _sigmoid(gate_ref[0])
    lane128 = _lane((1, 128))
    qrot = qrot_ref[0]

    for_all(b, slot, lambda c: c.wait())

    snew = snew_ref[0]
    wnew = wnew_ref[0]
    o_all = []
    for h in range(NSA_KV_HEADS):
        qo = _nsa_q8(qrot, h, h) * SCALE
        qob = qo.astype(bf16)
        halfmask = (lane128 // 64) == h
        scores, valids = [], []
        has_new = False
        for s in range(SEL_TOPK):
            j = idx_ref[b, h * SEL_TOPK + s]
            kt = buf_ref[slot, h, 0, s].reshape(128, 128).astype(bf16)
            valid = ((lane128 // SEL_BLOCK) == (j % (PAGE_SIZE // SEL_BLOCK))) & (j < n_past_blocks)
            scores.append(jnp.where(valid, _dot(qob, kt), NEG_INF))
            valids.append(valid)
            has_new = jnp.logical_or(has_new, j == n_past_blocks)
        s_new = jnp.sum(qo * snew[:, 0:128], axis=-1, keepdims=True)
        s_new = jnp.where(has_new, s_new, NEG_INF)
        smax = scores[0]
        for sc in scores[1:]:
            smax = jnp.maximum(smax, sc)
        mx = jnp.maximum(jnp.max(smax, axis=-1, keepdims=True), s_new)
        e_new = jnp.where(has_new, jnp.exp(s_new - mx), 0.0)
        esum = jnp.zeros((8, 128), f32)
        acc = jnp.zeros((8, 128), f32)
        for s in range(SEL_TOPK):
            e = jnp.where(valids[s], jnp.exp(scores[s] - mx), 0.0)
            esum = esum + e
            vt = buf_ref[slot, h, 1, s].reshape(128, 128).astype(bf16)
            acc = acc + _dot_nt(e.astype(bf16), vt)
        inv = 1.0 / jnp.maximum(jnp.sum(esum, axis=-1, keepdims=True) + e_new, 1e-30)
        o_slc = (acc + e_new * snew[:, 128:256]) * inv
        wt = win_ref[0, 0]
        kt = wt[0].reshape(128, WINDOW).astype(bf16)
        vt = wt[1].reshape(128, WINDOW).astype(bf16)
        maskw = _lane((1, WINDOW)) >= 1
        s_w = jnp.where(maskw, _dot(qob, kt), NEG_INF)
        sw_new = jnp.sum(qo * wnew[:, 0:128], axis=-1, keepdims=True)
        mx = jnp.maximum(jnp.max(s_w, axis=-1, keepdims=True), sw_new)
        e = jnp.where(maskw, jnp.exp(s_w - mx), 0.0)
        e_new = jnp.exp(sw_new - mx)
        inv = 1.0 / jnp.maximum(jnp.sum(e, axis=-1, keepdims=True) + e_new, 1e-30)
        o_win = (_dot_nt(e.astype(bf16), vt) + e_new * wnew[:, 128:256]) * inv
        gate_rows = []
        for c in range(3):
            gate_rows.append(_rows8([jnp.broadcast_to(gs[:, (h * 4 + g) * 3 + c:(h * 4 + g) * 3 + c + 1], (1, 128))
                                     for g in range(4)]))
        o8 = gate_rows[0] * ocmp_ref[0, h] + gate_rows[1] * o_slc + gate_rows[2] * o_win
        o_all.append(jnp.where(halfmask, o8, 0.0))

    lo = lane128 < 64
    for h in range(NSA_KV_HEADS):
        o8 = o_all[h]
        o8s = pltpu.roll(o8, 64, 1)
        low_src, high_src = (o8, o8s) if h == 0 else (o8s, o8)
        for gp in range(2):
            ch = jnp.where(lo, low_src[2 * gp:2 * gp + 1], high_src[2 * gp + 1:2 * gp + 2])
            out_ref[0, :, h * 256 + gp * 128:h * 256 + (gp + 1) * 128] = ch

    last = _lane((1, WINDOW)) == WINDOW - 1
    eye64 = _row((64, 64)) == _lane((64, 64))
    for kv in range(2):
        for h in range(NSA_KV_HEADS):
            c = kv * 2 + h
            newc = jnp.broadcast_to(wnew[:, c * 64:(c + 1) * 64], (64, 64))
            colv = jnp.sum(jnp.where(eye64, newc, 0.0), axis=1, keepdims=True)
            old = win_ref[0, 0, kv, h]
            nwin_ref[0, kv, h] = jnp.where(last, colv, pltpu.roll(old, WINDOW - 1, 1))


def _nsa_sel_sample(page_table, sel_idx, qrot, gates, ocmp, snew, wnew, win_t, pool_t, layer, past_len):
    n_batch = page_table.shape[0]
    row = lambda w: pl.BlockSpec((1, 1, w), lambda b, pt, ix: (b, 0, 0))
    grid_spec = pltpu.PrefetchScalarGridSpec(
        num_scalar_prefetch=2,
        grid=(n_batch,),
        in_specs=[row(512), row(128), pl.BlockSpec((1, 2, 8, 128), lambda b, pt, ix: (b, 0, 0, 0)),
                  row(256), row(256),
                  pl.BlockSpec((1, 1, 2, 2, HEAD_DIM, WINDOW), lambda b, pt, ix: (layer, b, 0, 0, 0, 0)),
                  pl.BlockSpec(memory_space=pl.ANY)],
        out_specs=[row(512), pl.BlockSpec((1, 2, 2, HEAD_DIM, WINDOW), lambda b, pt, ix: (b, 0, 0, 0, 0))],
        scratch_shapes=[pltpu.VMEM((2, NSA_KV_HEADS, 2, SEL_TOPK, 2, HEAD_DIM, PAGE_SIZE), f32),
                        pltpu.SemaphoreType.DMA((2,))],
    )
    return pl.pallas_call(
        functools.partial(_nsa_sel_sample_kernel, layer=layer, n_batch=n_batch,
                          n_past_blocks=past_len // SEL_BLOCK),
        grid_spec=grid_spec,
        out_shape=[jax.ShapeDtypeStruct((n_batch, 1, 512), f32),
                   jax.ShapeDtypeStruct((n_batch, 2, 2, HEAD_DIM, WINDOW), f32)],
        compiler_params=_cp(("arbitrary",)),
        name="nsa_sel_sample",
    )(page_table, sel_idx, qrot, gates, ocmp, snew, wnew, win_t, pool_t)


def _diff_sample_kernel(pt_ref, q_ref, new_ref, dl_ref, pool_ref, out_ref, buf_ref, m_ref, l_ref,
                        acc_ref, sem_ref, *, layer, n_pages, n_batch, n_split, lam_init):
    b = pl.program_id(0)
    hf = pl.program_id(1)
    step = b * n_split + hf
    slot = step % 2
    pps = n_pages // n_split
    rows = pps * PAGE_SIZE

    def copy(bb, hh, sl, p):
        return pltpu.make_async_copy(pool_ref.at[layer, pt_ref[bb, hh * pps + p]],
                                     buf_ref.at[sl, pl.ds(p * 4 * PAGE_SIZE, 4 * PAGE_SIZE), :],
                                     sem_ref.at[sl])

    def fetch(bb, hh, sl):
        for p in range(pps):
            copy(bb, hh, sl, p).start()

    @pl.when(step == 0)
    def _():
        fetch(0, 0, 0)

    @pl.when(step + 1 < n_batch * n_split)
    def _():
        nxt = step + 1
        fetch(nxt // n_split, nxt % n_split, 1 - slot)

    @pl.when(hf == 0)
    def _():
        m_ref[...] = jnp.full(m_ref.shape, NEG_INF, f32)
        l_ref[...] = jnp.zeros(l_ref.shape, f32)
        acc_ref[...] = jnp.zeros(acc_ref.shape, f32)

    for p in range(pps):
        copy(b, hf, slot, p).wait()

    qrow = q_ref[0]
    lane = _lane((8, 128))
    rid = _row((8, 128))
    q8s = []
    for h in range(2):
        q8 = _rows8([qrow[:, h * 256 + (r // 2) * 128:h * 256 + (r // 2 + 1) * 128] for r in range(4)])
        q8s.append(jnp.where((lane // 64) == (rid % 2), q8, 0.0) * SCALE)
    for h in range(2):
        k = buf_ref[slot, pl.ds(h, rows, stride=4), :].astype(bf16)
        v = buf_ref[slot, pl.ds(2 + h, rows, stride=4), :].astype(bf16)
        s = _dot_nt(q8s[h].astype(bf16), k)
        m_old = m_ref[h]
        m_new = jnp.maximum(m_old, jnp.max(s, axis=-1, keepdims=True))
        alpha = jnp.exp(m_old - m_new)
        p_ = jnp.exp(s - m_new)
        l_ref[h] = alpha * l_ref[h] + jnp.sum(p_, axis=-1, keepdims=True)
        acc_ref[h] = alpha * acc_ref[h] + _dot(p_.astype(bf16), v)
        m_ref[h] = m_new

    @pl.when(hf == n_split - 1)
    def _():
        lam = _diff_lambda(dl_ref[...], lam_init)
        new = new_ref[0]
        for h in range(2):
            s_new = jnp.sum(q8s[h] * new[:, h * 128:(h + 1) * 128], axis=-1, keepdims=True)
            m_old = m_ref[h]
            m_new = jnp.maximum(m_old, s_new)
            alpha = jnp.exp(m_old - m_new)
            p_new = jnp.exp(s_new - m_new)
            l_ = alpha * l_ref[h] + p_new
            acc = alpha * acc_ref[h] + p_new * new[:, 256 + h * 128:256 + (h + 1) * 128]
            o = acc * (1.0 / jnp.maximum(l_, 1e-30))
            for g in range(2):
                og = o[2 * g:2 * g + 1] - lam * o[2 * g + 1:2 * g + 2]
                out_ref[0, :, (h * 2 + g) * 128:(h * 2 + g + 1) * 128] = _rms_unit(og) * (1.0 - lam_init)


def _diff_sample(page_table, dqrot, new_rows, dl, pool, layer, lam_init):
    n_batch, n_pages = page_table.shape
    n_split = 2
    pps = n_pages // n_split
    row = lambda w: pl.BlockSpec((1, 1, w), lambda b, s, pt: (b, 0, 0))
    grid_spec = pltpu.PrefetchScalarGridSpec(
        num_scalar_prefetch=1,
        grid=(n_batch, n_split),
        in_specs=[row(512), row(512), pl.BlockSpec((4, 64), lambda b, s, pt: (0, 0)),
                  pl.BlockSpec(memory_space=pl.ANY)],
        out_specs=row(512),
        scratch_shapes=[pltpu.VMEM((2, pps * 4 * PAGE_SIZE, 128), f32), pltpu.VMEM((2, 8, 1), f32),
                        pltpu.VMEM((2, 8, 1), f32), pltpu.VMEM((2, 8, 128), f32),
                        pltpu.SemaphoreType.DMA((2,))],
    )
    return pl.pallas_call(
        functools.partial(_diff_sample_kernel, layer=layer, n_pages=n_pages, n_batch=n_batch,
                          n_split=n_split, lam_init=lam_init),
        grid_spec=grid_spec,
        out_shape=jax.ShapeDtypeStruct((n_batch, 1, 512), f32),
        compiler_params=_cp(("arbitrary", "arbitrary")),
        name="diff_sample",
    )(page_table, dqrot, new_rows, dl, pool)


def _prep_w_in(w):
    parts = jnp.split(w, [sum(IN_SPLITS[:i + 1]) for i in range(len(IN_SPLITS) - 1)], axis=-1)
    nq, ncmp, nslc, nwin, ngate, scb, scc, sch, dq, dk, dv, s5u = parts
    gate = jnp.pad(ngate, ((0, 0), (0, 128 - ngate.shape[1])))
    return jnp.concatenate([nq, scb, scc, sch, dq, s5u, ncmp, nslc, nwin, dk, dv, gate], axis=-1).astype(bf16)


def _prep_phi(pe, w):
    w2 = w.reshape(2, 2, CMP_STRIDE, HEAD_DIM, HEAD_DIM)
    wc = jnp.repeat(w2, 2, axis=0)
    eye = jnp.eye(4, dtype=f32)
    ws, pes = [], []
    for half in range(2):
        ws.append(jnp.einsum('csde,cf->scdfe', wc[:, half], eye).reshape(4096, 256).astype(bf16))
        pc = jnp.repeat(pe[:, half * CMP_STRIDE:(half + 1) * CMP_STRIDE], 2, axis=0)
        pes.append(jnp.transpose(pc, (1, 0, 2)).reshape(1, 4096))
    return pes[0], pes[1], ws[0], ws[1]


def _prep_s5(bbr, bbi, c_re, c_im):
    eye = jnp.eye(8, dtype=f32)

    def wb_of(bb):
        x = bb.reshape(S5_CH, S5_SLABS, 8, S5_STATE)
        return jnp.einsum('csgp,hg->shcgp', x, eye).reshape(S5_SLABS, 128, 512)

    wb = jnp.concatenate([wb_of(bbr), wb_of(bbi)], axis=-1).astype(bf16)

    def wc_of(c):
        x = c.reshape(S5_SLABS, 8, S5_CH, S5_STATE)
        return jnp.einsum('sgcp,hg->shpgc', x, eye).reshape(S5_SLABS, 512, 128)

    wc = jnp.concatenate([wc_of(c_re), -wc_of(c_im)], axis=1).astype(bf16)
    return wb, wc


def _rope_tables(pos):
    half = HEAD_DIM // 2
    inv = ROPE_THETA ** (-jnp.arange(half, dtype=f32) / half)
    ang = pos.astype(f32)[:, None] * inv[None, :]
    c, s = jnp.cos(ang), jnp.sin(ang)
    return jnp.tile(c, (1, 4)), jnp.tile(jnp.concatenate([-s, s], axis=1), (1, 2))


def kernel(x_prompt, x_sample, cache_nsa_cmp, cache_nsa_slc, cache_diff, state_nsa_win, state_sconv, state_s5_re, state_s5_im, state_ffn_conv, page_table, w_in, nsa_phi_pe, nsa_phi_w, sc_conv_w, diff_lambda, s5_a_re, s5_a_im, s5_log_dt, s5_b_re, s5_b_im, s5_c_re, s5_c_im, s5_d, s5_glu_w, s5_glu_b, mix_gain, w_out, ln1_g, ln1_b, ffn_w_up, ffn_conv_w, ffn_w_down, ln2_g, ln2_b):
    bp, t_len, _ = x_prompt.shape
    bs = x_sample.shape[0]
    n_pool = cache_nsa_cmp.shape[1]
    n_pages = page_table.shape[1]
    past_len = n_pages * PAGE_SIZE
    n_state = S5_GROUPS * S5_STATE
    mp = bp * t_len

    cos_p, sin_p = _rope_tables(jnp.arange(t_len))
    cos_s, sin_s = _rope_tables(jnp.full((bs,), past_len))

    pool_cmp = jnp.transpose(cache_nsa_cmp, (0, 1, 3, 4, 5, 2))
    pool_slc = jnp.transpose(cache_nsa_slc, (0, 1, 3, 4, 5, 2))
    pool_diff = cache_diff.reshape(DEPTH, n_pool, PAGE_SIZE * 4, 128)
    win_t = jnp.transpose(state_nsa_win, (0, 1, 3, 4, 5, 2))

    xp = x_prompt.reshape(mp, D_MODEL)
    xs = x_sample.reshape(bs, D_MODEL)
    xp_b, xs_b = xp.astype(bf16), xs.astype(bf16)

    outs_p = {k: [] for k in ('cmp', 'slc', 'win', 'diff', 'sc', 's5r', 's5i', 'ffn')}
    outs_s = {k: [] for k in ('cmp', 'slc', 'win', 'diff', 'sc', 's5r', 's5i', 'ffn')}

    gw = s5_glu_w.astype(bf16)
    wo = w_out.astype(bf16)
    wup = ffn_w_up.astype(bf16)
    wdn = ffn_w_down.astype(bf16)

    for l in range(DEPTH):
        lam_init = 0.8 - 0.6 * math.exp(-0.3 * l)
        w_in_l = _prep_w_in(w_in[l])
        pe1, pe2, w1, w2 = _prep_phi(nsa_phi_pe[l], nsa_phi_w[l])
        tabs, bbr, bbi = _s5_prep(s5_a_re[l].reshape(1, n_state), s5_a_im[l].reshape(1, n_state),
                                  jnp.repeat(s5_log_dt[l], S5_STATE).reshape(1, n_state),
                                  jnp.transpose(s5_b_re[l], (2, 0, 1)).reshape(S5_CH, n_state),
                                  jnp.transpose(s5_b_im[l], (2, 0, 1)).reshape(S5_CH, n_state))
        wb5, wc5 = _prep_s5(bbr, bbi, s5_c_re[l], s5_c_im[l])
        d5 = s5_d[l].reshape(1, 512)
        gb = s5_glu_b[l].reshape(1, 512)
        gain = mix_gain[l].reshape(1, D_MODEL)
        g1, b1 = ln1_g[l].reshape(1, D_MODEL), ln1_b[l].reshape(1, D_MODEL)
        g2, b2 = ln2_g[l].reshape(1, D_MODEL), ln2_b[l].reshape(1, D_MODEL)
        cwf = ffn_conv_w[l]
        scw = sc_conv_w[l]
        dl = diff_lambda[l]

        hcat = _in_proj(xp_b, w_in_l, 1024)
        qrot, dqrot, kvslc, kvwin, kvdiff = _rope(hcat, cos_p, sin_p, 512)
        kvcmp = hcat[:, C_CMP:C_CMP + 256]
        kvc = _cmp_prompt(kvcmp.reshape(mp // CMP_STRIDE, CMP_STRIDE * 256), pe1, pe2, w1, w2, bp)
        nsa = _nsa_prompt(hcat, qrot, kvc, kvslc, kvwin, bp, t_len)
        dif = _diff_prompt(dqrot, kvdiff, dl, bp, t_len, lam_init)
        sc, sc_tail = _sconv_prompt(hcat, scw, bp, t_len)
        s5y, s5r, s5i = _s5_scan(hcat, wb5, wc5, d5, tabs, bp, t_len)
        x1, x1b = _mix(nsa, sc, dif, s5y, gw, gb, gain, wo, xp, g1, b1, 256, l)
        xp, xp_b, ffn_tail = _ffn_prompt(x1b, x1, wup, wdn, cwf, g2, b2, t_len, l)

        outs_p['cmp'].append(kvcmp.reshape(bp, t_len, 2, 2, HEAD_DIM))
        outs_p['slc'].append(kvslc.reshape(bp, t_len, 2, 2, HEAD_DIM))
        outs_p['win'].append(kvwin.reshape(bp, t_len, 2, 2, HEAD_DIM)[:, t_len - WINDOW:])
        outs_p['diff'].append(kvdiff.reshape(bp, t_len, 2, 2, 2 * HEAD_DIM))
        outs_p['sc'].append(sc_tail[:, 6:8])
        outs_p['s5r'].append(s5r.reshape(bp, S5_GROUPS, S5_STATE))
        outs_p['s5i'].append(s5i.reshape(bp, S5_GROUPS, S5_STATE))
        tiles_per_seq = ffn_tail.shape[0] // bp
        outs_p['ffn'].append(ffn_tail.reshape(bp, tiles_per_seq, 8, D_FF)[:, -1, 6:8])

        hs = _in_proj(xs_b, w_in_l, bs)
        qrot_s, dqrot_s, kvslc_s, kvwin_s, kvdiff_s = _rope(hs, cos_s, sin_s, bs)
        kvcmp_s = hs[:, C_CMP:C_CMP + 256]
        ocmp_s, sel_s = _cmp_sel_sample(page_table, kvcmp_s.reshape(bs, 1, 256),
                                        hs[:, C_NQ:C_NQ + 512].reshape(bs, 1, 512), pe1, pe2, w1, w2,
                                        pool_cmp, l, past_len)
        nsa_s, nwin_t = _nsa_sel_sample(page_table, sel_s[:, :, 0], qrot_s.reshape(bs, 1, 512),
                                        hs[:, C_GATE:C_GATE + 128].reshape(bs, 1, 128), ocmp_s,
                                        kvslc_s.reshape(bs, 1, 256), kvwin_s.reshape(bs, 1, 256),
                                        win_t, pool_slc, l, past_len)
        dif_s = _diff_sample(page_table, dqrot_s.reshape(bs, 1, 512), kvdiff_s.reshape(bs, 1, 512),
                             dl, pool_diff, l, lam_init)
        scp = jnp.transpose(state_sconv[l], (1, 0, 2))
        sc_s, z_s, s5y_s, s5r_s, s5i_s = _sample_small(
            hs[:, C_SCB:C_SCB + 512], hs[:, C_SCC:C_SCC + 512], hs[:, C_SCH:C_SCH + 512], scw, scp,
            hs[:, C_S5U:C_S5U + 512], wb5, wc5, d5, tabs,
            state_s5_re[l].reshape(bs, n_state), state_s5_im[l].reshape(bs, n_state))
        x1s, x1sb = _mix(nsa_s.reshape(bs, 512), sc_s, dif_s.reshape(bs, 512), s5y_s, gw, gb, gain, wo,
                         xs, g1, b1, bs, l)
        prev_ffn = state_ffn_conv[l]
        xs, xs_b, aup_s = _ffn_sample(x1sb, x1s, prev_ffn[:, 0], prev_ffn[:, 1], wup, wdn, cwf, g2, b2, l)

        outs_s['cmp'].append(kvcmp_s.reshape(bs, 1, 2, 2, HEAD_DIM))
        outs_s['slc'].append(kvslc_s.reshape(bs, 1, 2, 2, HEAD_DIM))
        outs_s['win'].append(jnp.transpose(nwin_t, (0, 4, 1, 2, 3)))
        outs_s['diff'].append(kvdiff_s.reshape(bs, 1, 2, 2, 2 * HEAD_DIM))
        outs_s['sc'].append(jnp.stack([state_sconv[l][:, 1], z_s], axis=1))
        outs_s['s5r'].append(s5r_s.reshape(bs, S5_GROUPS, S5_STATE))
        outs_s['s5i'].append(s5i_s.reshape(bs, S5_GROUPS, S5_STATE))
        outs_s['ffn'].append(jnp.stack([prev_ffn[:, 1], aup_s], axis=1))

    order = ('cmp', 'slc', 'win', 'diff', 'sc', 's5r', 's5i', 'ffn')
    res = [xp.reshape(bp, t_len, D_MODEL), xs.reshape(bs, 1, D_MODEL)]
    res += [jnp.stack(outs_p[k], axis=0) for k in order]
    res += [jnp.stack(outs_s[k], axis=0) for k in order]
    return tuple(res)
```

```python
import functools
import math

import jax
import jax.numpy as jnp
from jax import lax
from jax.experimental import pallas as pl
from jax.experimental.pallas import tpu as pltpu

f32 = jnp.float32
bf16 = jnp.bfloat16

D_MODEL = 2048
DEPTH = 2
PAGE_SIZE = 128
HEAD_DIM = 64
GROUP_WIDTH = D_MODEL // 4
NSA_KV_HEADS = 2
NSA_GROUP = 4
CMP_STRIDE = 16
CMP_LEN = 32
SEL_BLOCK = 64
SEL_TOPK = 16
WINDOW = 512
FORCE_BONUS = 1e4
CONV_W = 3
S5_CH = 16
S5_GROUPS = 32
S5_STATE = 64
D_FF = 5632
ROPE_THETA = 10000.0
QBLOCK = 128
LN_EPS = 1e-5
RMS_EPS = 1e-6
NEG_INF = -1e30
DN_ALPHA = (2 * DEPTH) ** 0.25
SCALE = HEAD_DIM ** -0.5
LOG2E = math.log2(math.e)

IN_SPLITS = (512, 256, 256, 256, 24, 512, 512, 512, 512, 256, 256, 512)
C_NQ, C_SCB, C_SCC, C_SCH, C_DQ, C_S5U = 0, 512, 1024, 1536, 2048, 2560
C_CMP, C_SLC, C_WIN, C_DK, C_DV, C_GATE = 3072, 3328, 3584, 3840, 4096, 4352
HC = 4608
IN_TN = 768
FFN_TF = 512

VMEM_CAP_V7X = 64 * 1024 * 1024
VMEM_LIMIT = 56 * 1024 * 1024
NC_PAD = 640
S5_SLABS = 4
KEY_CLASS_BLOCKS = 4
FFN_CHUNK = 256


def _cp(sem):
    return pltpu.CompilerParams(dimension_semantics=sem, vmem_limit_bytes=VMEM_LIMIT)


def _dot(a, b):
    return jnp.dot(a, b, preferred_element_type=f32)


def _dot_nt(a, b):
    return lax.dot_general(a, b, (((1,), (1,)), ((), ())), preferred_element_type=f32)


def _lane(shape):
    return lax.broadcasted_iota(jnp.int32, shape, len(shape) - 1)


def _row(shape):
    return lax.broadcasted_iota(jnp.int32, shape, len(shape) - 2)


def _msoftmax(s, mask):
    s = jnp.where(mask, s, NEG_INF)
    m = jnp.max(s, axis=-1, keepdims=True)
    e = jnp.where(mask, jnp.exp(s - m), 0.0)
    return e * (1.0 / jnp.maximum(jnp.sum(e, axis=-1, keepdims=True), 1e-30))


def _exp2_softmax(s, bias):
    s = s + bias[None]
    e = jnp.exp2(s - jnp.max(s, axis=-1, keepdims=True))
    return e, 1.0 / jnp.maximum(jnp.sum(e, axis=-1, keepdims=True), 1e-30)


def _sigmoid(x):
    return 1.0 / (1.0 + jnp.exp(-x))


def _rms_unit(x):
    return x * lax.rsqrt(jnp.mean(x * x, axis=-1, keepdims=True) + RMS_EPS)


def _layer_norm(z, g, b):
    mu = jnp.mean(z, axis=-1, keepdims=True)
    d = z - mu
    var = jnp.mean(d * d, axis=-1, keepdims=True)
    return d * lax.rsqrt(var + LN_EPS) * g + b


def _split_hi_lo(x):
    hi = x.astype(bf16)
    lo = (x - hi.astype(f32)).astype(bf16)
    return hi, lo


def _matmul_kernel(x_ref, w_ref, o_ref):
    o_ref[...] = _dot(x_ref[...], w_ref[...])


def _in_proj(xb, w, tm):
    m = xb.shape[0]
    tn = IN_TN
    return pl.pallas_call(
        _matmul_kernel,
        grid=(m // tm, HC // tn),
        in_specs=[pl.BlockSpec((tm, D_MODEL), lambda i, j: (i, 0)),
                  pl.BlockSpec((None, D_MODEL, tn), lambda i, j: (j, 0, 0))],
        out_specs=pl.BlockSpec((tm, tn), lambda i, j: (i, j)),
        out_shape=jax.ShapeDtypeStruct((m, HC), f32),
        compiler_params=_cp(("parallel", "arbitrary")),
        name="in_proj",
    )(xb, w)


def _rope_cols(x, cos, sin):
    outs = []
    first = (_lane((1, 128)) % 64) < 32
    for c in range(x.shape[1] // 128):
        xc = x[:, c * 128:(c + 1) * 128]
        sw = jnp.where(first, pltpu.roll(xc, 96, 1), pltpu.roll(xc, 32, 1))
        outs.append(xc * cos + sw * sin)
    return outs


def _rope_kernel(nq_ref, dq_ref, slc_ref, win_ref, dk_ref, dv_ref, cos_ref, sin_ref,
                 qrot_ref, dqrot_ref, kvslc_ref, kvwin_ref, kvdiff_ref):
    cos = cos_ref[...]
    sin = sin_ref[...]
    for c, v in enumerate(_rope_cols(nq_ref[...], cos, sin)):
        qrot_ref[:, c * 128:(c + 1) * 128] = v
    for c, v in enumerate(_rope_cols(dq_ref[...], cos, sin)):
        dqrot_ref[:, c * 128:(c + 1) * 128] = v
    kvslc_ref[:, 0:128] = _rope_cols(slc_ref[:, 0:128], cos, sin)[0]
    kvslc_ref[:, 128:256] = slc_ref[:, 128:256]
    kvwin_ref[:, 0:128] = _rope_cols(win_ref[:, 0:128], cos, sin)[0]
    kvwin_ref[:, 128:256] = win_ref[:, 128:256]
    for c, v in enumerate(_rope_cols(dk_ref[...], cos, sin)):
        kvdiff_ref[:, c * 128:(c + 1) * 128] = v
    kvdiff_ref[:, 256:512] = dv_ref[...]


def _rope(hcat, cos, sin, tr):
    m = hcat.shape[0]
    nt = cos.shape[0] // tr

    def col(w, off):
        return pl.BlockSpec((tr, w), lambda i: (i, off // w))

    tab = pl.BlockSpec((tr, 128), lambda i: (i % nt, 0))
    return pl.pallas_call(
        _rope_kernel,
        grid=(m // tr,),
        in_specs=[col(512, C_NQ), col(512, C_DQ), col(256, C_SLC), col(256, C_WIN),
                  col(256, C_DK), col(256, C_DV), tab, tab],
        out_specs=[pl.BlockSpec((tr, 512), lambda i: (i, 0)),
                   pl.BlockSpec((tr, 512), lambda i: (i, 0)),
                   pl.BlockSpec((tr, 256), lambda i: (i, 0)),
                   pl.BlockSpec((tr, 256), lambda i: (i, 0)),
                   pl.BlockSpec((tr, 512), lambda i: (i, 0))],
        out_shape=[jax.ShapeDtypeStruct((m, 512), f32), jax.ShapeDtypeStruct((m, 512), f32),
                   jax.ShapeDtypeStruct((m, 256), f32), jax.ShapeDtypeStruct((m, 256), f32),
                   jax.ShapeDtypeStruct((m, 512), f32)],
        compiler_params=_cp(("parallel",)),
        name="rope",
    )(hcat, hcat, hcat, hcat, hcat, hcat, cos, sin)


def _cmp_prompt_kernel(z_ref, pe1_ref, pe2_ref, w1_ref, w2_ref, o_ref, sb_ref):
    z = z_ref[...]
    a = _dot((z + pe1_ref[...]).astype(bf16), w1_ref[...])
    bm = _dot((z + pe2_ref[...]).astype(bf16), w2_ref[...])
    n = z.shape[0]
    sb_ref[0:n, :] = bm
    sb_ref[n:n + 8, :] = jnp.zeros((8, 256), f32)
    o_ref[0] = a + sb_ref[1:n + 1, :]


def _cmp_prompt(z, pe1, pe2, w1, w2, bsz):
    n = z.shape[0] // bsz
    full = lambda shp: pl.BlockSpec(shp, lambda b: (0, 0))
    return pl.pallas_call(
        _cmp_prompt_kernel,
        grid=(bsz,),
        in_specs=[pl.BlockSpec((n, 4096), lambda b: (b, 0)), full((1, 4096)), full((1, 4096)),
                  full((4096, 256)), full((4096, 256))],
        out_specs=pl.BlockSpec((1, n, 256), lambda b: (b, 0, 0)),
        out_shape=jax.ShapeDtypeStruct((bsz, n, 256), f32),
        scratch_shapes=[pltpu.VMEM((n + 8, 256), f32)],
        compiler_params=_cp(("parallel",)),
        name="cmp_prompt",
    )(z, pe1, pe2, w1, w2)


def _nsa_qstack(blk, h):
    halfmask = (_lane((1, 128)) // 64) == h
    parts = []
    for g in range(4):
        c = blk[:, (g // 2) * 128:(g // 2 + 1) * 128]
        if g % 2 != h:
            c = pltpu.roll(c, 64, 1)
        parts.append(jnp.where(halfmask, c, 0.0))
    return jnp.concatenate(parts, axis=0)


def _nsa_assemble(o_list, h):
    lo = _lane((1, 128)) < 64
    chunks = []
    for gp in range(2):
        a, b = o_list[2 * gp], o_list[2 * gp + 1]
        if h == 0:
            b = pltpu.roll(b, 64, 1)
        else:
            a = pltpu.roll(a, 64, 1)
        chunks.append(jnp.where(lo, a, b))
    return chunks


def _nsa_prompt_body(qraw_ref, qrot_ref, gate_ref, kvc_ref, slc_ref, win_ref, out_ref, *, s0, kmax, t_len):
    qb = QBLOCK
    qpos = s0 + _row((qb, 1))
    gs = _sigmoid(gate_ref[...])
    n_i = _lane((1, 128))
    maskc = ((16 * n_i + 31) <= qpos) & (n_i < 127)
    nn = _row((128, 128))
    j2 = _lane((128, 128))
    cov = jnp.clip(jnp.minimum(16 * nn + 32, 64 * j2 + 64) - jnp.maximum(16 * nn, 64 * j2), 0, 32)
    cov = (jnp.where((nn < 127) & (j2 < 32), cov, 0).astype(f32) * (1.0 / CMP_LEN)).astype(bf16)
    expand = jnp.where((_lane((128, kmax)) // SEL_BLOCK) == _row((128, kmax)), 1.0, 0.0).astype(bf16)
    causal = _lane((1, kmax)) <= qpos
    wlen = WINDOW + qb
    start = pl.multiple_of(jnp.clip(s0 - WINDOW, 0, t_len - wlen), 128)
    wpos = start + _lane((1, wlen))
    bias_w = jnp.where((wpos <= qpos) & ((qpos - wpos) < WINDOW), 0.0, NEG_INF)
    jj = _lane((1, 128))
    cur = qpos // SEL_BLOCK
    n_sel = kmax // SEL_BLOCK
    forced = (jj == 0) | (jj == cur) | (jj == cur - 1)

    for h in range(NSA_KV_HEADS):
        kc = kvc_ref[0, :, 0:128].astype(bf16)
        vc = kvc_ref[0, :, 128:256].astype(bf16)
        qr = (_nsa_qstack(qraw_ref[:, h * 256:(h + 1) * 256], h) * SCALE).astype(bf16)
        qo = (_nsa_qstack(qrot_ref[:, h * 256:(h + 1) * 256], h) * (SCALE * LOG2E)).astype(bf16)
        s_c = _dot_nt(qr, kc).reshape(4, qb, 128)
        p_c = _msoftmax(s_c, maskc[None])
        o_cmp = _dot(p_c.reshape(4 * qb, 128).astype(bf16), vc)
        psum = p_c[0] + p_c[1] + p_c[2] + p_c[3]
        p_hi, p_lo = _split_hi_lo(psum)
        imp = _dot(p_hi, cov) + _dot(p_lo, cov)
        imp = jnp.where(forced, imp + FORCE_BONUS, imp)
        imp = jnp.where(jj <= cur, imp, -FORCE_BONUS)
        imp = jnp.where(jj < n_sel, imp, -3e38)
        rank = jnp.zeros((qb, 128), f32)
        for k in range(n_sel):
            col = imp[:, k:k + 1]
            beats = (col > imp) | ((col == imp) & (jj > k))
            rank = rank + jnp.where(beats, 1.0, 0.0)
        sel = jnp.where((rank < SEL_TOPK) & (jj < n_sel), 1.0, 0.0).astype(bf16)
        bias_s = jnp.where((_dot(sel, expand) > 0.5) & causal, 0.0, NEG_INF)
        ks = slc_ref[0:kmax, 0:128].astype(bf16)
        vs = slc_ref[0:kmax, 128:256].astype(bf16)
        e_s, inv_s = _exp2_softmax(_dot_nt(qo, ks).reshape(4, qb, kmax), bias_s)
        o_slc = _dot(e_s.reshape(4 * qb, kmax).astype(bf16), vs) * inv_s.reshape(4 * qb, 1)
        kw = win_ref[pl.ds(start, wlen), 0:128].astype(bf16)
        vw = win_ref[pl.ds(start, wlen), 128:256].astype(bf16)
        e_w, inv_w = _exp2_softmax(_dot_nt(qo, kw).reshape(4, qb, wlen), bias_w)
        o_win = _dot(e_w.reshape(4 * qb, wlen).astype(bf16), vw) * inv_w.reshape(4 * qb, 1)
        o_list = []
        for g in range(NSA_GROUP):
            gi = (h * NSA_GROUP + g) * 3
            r = slice(g * qb, (g + 1) * qb)
            o_list.append(gs[:, gi:gi + 1] * o_cmp[r] + gs[:, gi + 1:gi + 2] * o_slc[r]
                          + gs[:, gi + 2:gi + 3] * o_win[r])
        for gp, ch in enumerate(_nsa_assemble(o_list, h)):
            out_ref[:, h * 256 + gp * 128:h * 256 + (gp + 1) * 128] = ch


def _by_key_class(body, t_len):
    i = pl.program_id(1)
    span = KEY_CLASS_BLOCKS * QBLOCK
    for c in range(t_len // span):
        @pl.when(i // KEY_CLASS_BLOCKS == c)
        def _(c=c):
            body(s0=i * QBLOCK, kmax=(c + 1) * span)


def _nsa_prompt_kernel(*refs, t_len):
    _by_key_class(functools.partial(_nsa_prompt_body, *refs, t_len=t_len), t_len)


def _nsa_prompt(hcat, qrot, kvc, kvslc, kvwin, bsz, t_len):
    m = hcat.shape[0]
    nqb = t_len // QBLOCK
    return pl.pallas_call(
        functools.partial(_nsa_prompt_kernel, t_len=t_len),
        grid=(bsz, nqb),
        in_specs=[pl.BlockSpec((QBLOCK, 512), lambda b, i: (b * nqb + i, 0)),
                  pl.BlockSpec((QBLOCK, 512), lambda b, i: (b * nqb + i, 0)),
                  pl.BlockSpec((QBLOCK, 128), lambda b, i: (b * nqb + i, C_GATE // 128)),
                  pl.BlockSpec((1, 128, 256), lambda b, i: (b, 0, 0)),
                  pl.BlockSpec((t_len, 256), lambda b, i: (b, 0)),
                  pl.BlockSpec((t_len, 256), lambda b, i: (b, 0))],
        out_specs=pl.BlockSpec((QBLOCK, 512), lambda b, i: (b * nqb + i, 0)),
        out_shape=jax.ShapeDtypeStruct((m, 512), f32),
        compiler_params=_cp(("parallel", "arbitrary")),
        name="nsa_prompt",
    )(hcat, qrot, hcat, kvc, kvslc, kvwin)


def _diff_lambda(dl, lam_init):
    a = jnp.sum(dl[0:1, :] * dl[1:2, :], axis=-1, keepdims=True)
    b = jnp.sum(dl[2:3, :] * dl[3:4, :], axis=-1, keepdims=True)
    return jnp.exp(a) - jnp.exp(b) + lam_init


def _diff_prompt_body(dq_ref, kv_ref, dl_ref, out_ref, *, s0, kmax, lam_init):
    qb = QBLOCK
    qpos = s0 + _row((qb, 1))
    bias = jnp.where(_lane((1, kmax)) <= qpos, 0.0, NEG_INF)
    lam = _diff_lambda(dl_ref[...], lam_init)
    lane = _lane((1, 128))
    for h in range(2):
        k = kv_ref[0:kmax, h * 128:(h + 1) * 128].astype(bf16)
        v = kv_ref[0:kmax, 256 + h * 128:256 + (h + 1) * 128].astype(bf16)
        parts = []
        for g in range(2):
            c = dq_ref[:, h * 256 + g * 128:h * 256 + (g + 1) * 128] * (SCALE * LOG2E)
            for i in range(2):
                parts.append(jnp.where((lane // 64) == i, c, 0.0))
        q = jnp.concatenate(parts, axis=0).astype(bf16)
        e, inv = _exp2_softmax(_dot_nt(q, k).reshape(4, qb, kmax), bias)
        o = _dot(e.reshape(4 * qb, kmax).astype(bf16), v) * inv.reshape(4 * qb, 1)
        for g in range(2):
            og = o[2 * g * qb:(2 * g + 1) * qb] - lam * o[(2 * g + 1) * qb:(2 * g + 2) * qb]
            out_ref[:, (h * 2 + g) * 128:(h * 2 + g + 1) * 128] = _rms_unit(og) * (1.0 - lam_init)


def _diff_prompt_kernel(*refs, t_len, lam_init):
    _by_key_class(functools.partial(_diff_prompt_body, *refs, lam_init=lam_init), t_len)


def _diff_prompt(dqrot, kvdiff, dl, bsz, t_len, lam_init):
    m = dqrot.shape[0]
    nqb = t_len // QBLOCK
    return pl.pallas_call(
        functools.partial(_diff_prompt_kernel, t_len=t_len, lam_init=lam_init),
        grid=(bsz, nqb),
        in_specs=[pl.BlockSpec((QBLOCK, 512), lambda b, i: (b * nqb + i, 0)),
                  pl.BlockSpec((t_len, 512), lambda b, i: (b, 0)),
                  pl.BlockSpec((4, 64), lambda b, i: (0, 0))],
        out_specs=pl.BlockSpec((QBLOCK, 512), lambda b, i: (b * nqb + i, 0)),
        out_shape=jax.ShapeDtypeStruct((m, 512), f32),
        compiler_params=_cp(("parallel", "arbitrary")),
        name="diff_prompt",
    )(dqrot, kvdiff, dl)


def _sconv_prompt_kernel(b_ref, c_ref, h_ref, w_ref, out_ref, tail_ref, buf_ref, *, tr):
    t = pl.program_id(1)

    @pl.when(t == 0)
    def _():
        buf_ref[0:8, :] = jnp.zeros((8, 512), f32)

    z = c_ref[...] * h_ref[...]
    buf_ref[8:8 + tr, :] = z
    w = w_ref[...]
    y = w[0:1] * buf_ref[6:6 + tr, :] + w[1:2] * buf_ref[7:7 + tr, :] + w[2:3] * z
    out_ref[...] = _rms_unit(b_ref[...] * y)
    tail = z[tr - 8:tr]
    tail_ref[0] = tail
    buf_ref[0:8, :] = tail


def _sconv_prompt(hcat, w, bsz, t_len):
    tr = 512
    nt = t_len // tr
    m = hcat.shape[0]

    def col(off):
        return pl.BlockSpec((tr, 512), lambda b, t: (b * nt + t, off // 512))

    return pl.pallas_call(
        functools.partial(_sconv_prompt_kernel, tr=tr),
        grid=(bsz, nt),
        in_specs=[col(C_SCB), col(C_SCC), col(C_SCH), pl.BlockSpec((3, 512), lambda b, t: (0, 0))],
        out_specs=[pl.BlockSpec((tr, 512), lambda b, t: (b * nt + t, 0)),
                   pl.BlockSpec((1, 8, 512), lambda b, t: (b, 0, 0))],
        out_shape=[jax.ShapeDtypeStruct((m, 512), f32), jax.ShapeDtypeStruct((bsz, 8, 512), f32)],
        scratch_shapes=[pltpu.VMEM((8 + tr, 512), f32)],
        compiler_params=_cp(("parallel", "arbitrary")),
        name="sconv_prompt",
    )(hcat, hcat, hcat, w)


def _cmul(ar, ai, br, bi):
    return ar * br - ai * bi, ar * bi + ai * br


def _s5_prep_kernel(ar_ref, ai_ref, ldt_ref, br_ref, bi_ref, tab_ref, bbr_ref, bbi_ref):
    ar, ai = ar_ref[...], ai_ref[...]
    dt = jnp.exp(ldt_ref[...])
    mag = jnp.exp(ar * dt)
    abr, abi = mag * jnp.cos(ai * dt), mag * jnp.sin(ai * dt)
    den = ar * ar + ai * ai
    nr, ni = abr - 1.0, abi
    cre = (nr * ar + ni * ai) / den
    cim = (ni * ar - nr * ai) / den
    br, bi = br_ref[...], bi_ref[...]
    bbr_ref[...] = cre * br - cim * bi
    bbi_ref[...] = cre * bi + cim * br
    pw = [(abr, abi)]
    for _ in range(7):
        pw.append(_cmul(pw[-1][0], pw[-1][1], abr, abi))
    n = ar.shape[1]
    row = _row((8, n))
    zero = jnp.zeros((8, n), f32)
    for idx, (sh, p) in enumerate(((1, pw[0]), (2, pw[1]), (4, pw[3]))):
        tab_ref[2 * idx] = jnp.where(row >= sh, jnp.broadcast_to(p[0], (8, n)), zero)
        tab_ref[2 * idx + 1] = jnp.where(row >= sh, jnp.broadcast_to(p[1], (8, n)), zero)
    pr, pi = zero, zero
    for i in range(8):
        pr = jnp.where(row == i, jnp.broadcast_to(pw[i][0], (8, n)), pr)
        pi = jnp.where(row == i, jnp.broadcast_to(pw[i][1], (8, n)), pi)
    tab_ref[6] = pr
    tab_ref[7] = pi


def _s5_prep(ar, ai, ldt, br, bi):
    n = S5_GROUPS * S5_STATE
    return pl.pallas_call(
        _s5_prep_kernel,
        out_shape=[jax.ShapeDtypeStruct((8, 8, n), f32), jax.ShapeDtypeStruct((S5_CH, n), f32),
                   jax.ShapeDtypeStruct((S5_CH, n), f32)],
        name="s5_prep",
    )(ar, ai, ldt, br, bi)


def _s5_scan_kernel(u_ref, wb_ref, wc_ref, d_ref, tab_ref, y_ref, hr_ref, hi_ref,
                    xbuf_ref, cr_ref, ci_ref, *, tt):
    t = pl.program_id(2)

    @pl.when(t == 0)
    def _():
        cr_ref[...] = jnp.zeros((8, 512), f32)
        ci_ref[...] = jnp.zeros((8, 512), f32)

    u = u_ref[...]
    xbuf_ref[...] = _dot(u.astype(bf16), wb_ref[0])

    def body(r, carry):
        cr, ci = carry
        rows = pl.ds(pl.multiple_of(r * 8, 8), 8)
        xr = xbuf_ref[rows, 0:512]
        xi = xbuf_ref[rows, 512:1024]
        for idx, sh in enumerate((1, 2, 4)):
            a_r, a_i = tab_ref[2 * idx], tab_ref[2 * idx + 1]
            sr, si = pltpu.roll(xr, sh, 0), pltpu.roll(xi, sh, 0)
            xr, xi = xr + a_r * sr - a_i * si, xi + a_r * si + a_i * sr
        p_r, p_i = tab_ref[6], tab_ref[7]
        hr = xr + p_r * cr - p_i * ci
        hi = xi + p_r * ci + p_i * cr
        xbuf_ref[rows, 0:512] = hr
        xbuf_ref[rows, 512:1024] = hi
        return (jnp.broadcast_to(hr[7:8, :], (8, 512)), jnp.broadcast_to(hi[7:8, :], (8, 512)))

    cr, ci = lax.fori_loop(0, tt // 8, body, (cr_ref[...], ci_ref[...]), unroll=4)
    cr_ref[...] = cr
    ci_ref[...] = ci
    hr_ref[0] = cr[0:1, :]
    hi_ref[0] = ci[0:1, :]
    y_ref[...] = _dot(xbuf_ref[...].astype(bf16), wc_ref[0]) + d_ref[...] * u


def _s5_scan(hcat, wb, wc, d, tabs, bsz, t_len):
    tt = 512
    nt = t_len // tt
    m = hcat.shape[0]
    n = S5_GROUPS * S5_STATE
    return pl.pallas_call(
        functools.partial(_s5_scan_kernel, tt=tt),
        grid=(bsz, S5_SLABS, nt),
        in_specs=[pl.BlockSpec((tt, 128), lambda b, s, t: (b * nt + t, C_S5U // 128 + s)),
                  pl.BlockSpec((1, 128, 1024), lambda b, s, t: (s, 0, 0)),
                  pl.BlockSpec((1, 1024, 128), lambda b, s, t: (s, 0, 0)),
                  pl.BlockSpec((1, 128), lambda b, s, t: (0, s)),
                  pl.BlockSpec((8, 8, 512), lambda b, s, t: (0, 0, s))],
        out_specs=[pl.BlockSpec((tt, 128), lambda b, s, t: (b * nt + t, s)),
                   pl.BlockSpec((1, 1, 512), lambda b, s, t: (b, 0, s)),
                   pl.BlockSpec((1, 1, 512), lambda b, s, t: (b, 0, s))],
        out_shape=[jax.ShapeDtypeStruct((m, 512), f32), jax.ShapeDtypeStruct((bsz, 1, n), f32),
                   jax.ShapeDtypeStruct((bsz, 1, n), f32)],
        scratch_shapes=[pltpu.VMEM((tt, 1024), f32), pltpu.VMEM((8, 512), f32), pltpu.VMEM((8, 512), f32)],
        compiler_params=_cp(("parallel", "parallel", "arbitrary")),
        name="s5_scan",
    )(hcat, wb, wc, d, tabs)


def _sample_small_kernel(scb_ref, scc_ref, sch_ref, scw_ref, scp_ref, u_ref, wb_ref, wc_ref, d_ref,
                         tab_ref, h0r_ref, h0i_ref, sc_ref, z_ref, y_ref, hr_ref, hi_ref):
    z = scc_ref[...] * sch_ref[...]
    w = scw_ref[...]
    y = w[0:1] * scp_ref[0] + w[1:2] * scp_ref[1] + w[2:3] * z
    sc_ref[...] = _rms_unit(scb_ref[...] * y)
    z_ref[...] = z
    u = u_ref[...]
    for s in range(S5_SLABS):
        x = _dot(u[:, s * 128:(s + 1) * 128].astype(bf16), wb_ref[s])
        lanes = slice(s * 512, (s + 1) * 512)
        a_r, a_i = tab_ref[6, 0:1, lanes], tab_ref[7, 0:1, lanes]
        h0r, h0i = h0r_ref[:, lanes], h0i_ref[:, lanes]
        hr = a_r * h0r - a_i * h0i + x[:, 0:512]
        hi = a_r * h0i + a_i * h0r + x[:, 512:1024]
        hr_ref[:, lanes] = hr
        hi_ref[:, lanes] = hi
        hcat = jnp.concatenate([hr, hi], axis=1).astype(bf16)
        cols = slice(s * 128, (s + 1) * 128)
        y_ref[:, cols] = _dot(hcat, wc_ref[s]) + d_ref[:, cols] * u[:, cols]


def _sample_small(scb, scc, sch, scw, scp, u, wb, wc, d, tabs, h0r, h0i):
    bsz = u.shape[0]
    n = S5_GROUPS * S5_STATE
    return pl.pallas_call(
        _sample_small_kernel,
        out_shape=[jax.ShapeDtypeStruct((bsz, 512), f32), jax.ShapeDtypeStruct((bsz, 512), f32),
                   jax.ShapeDtypeStruct((bsz, 512), f32), jax.ShapeDtypeStruct((bsz, n), f32),
                   jax.ShapeDtypeStruct((bsz, n), f32)],
        compiler_params=pltpu.CompilerParams(vmem_limit_bytes=VMEM_LIMIT),
        name="sample_small",
    )(scb, scc, sch, scw, scp, u, wb, wc, d, tabs, h0r, h0i)


def _gelu(x):
    return 0.5 * x * (1.0 + jnp.tanh(math.sqrt(2.0 / math.pi) * (x + 0.044715 * (x * x * x))))


def _mix_kernel(nsa_ref, sc_ref, diff_ref, s5_ref, gw_ref, gb_ref, gain_ref, wo_ref, x_ref,
                g_ref, b_ref, out_ref, outb_ref):
    y = _gelu(s5_ref[...])
    s5o = y * _sigmoid(_dot(y.astype(bf16), gw_ref[...]) + gb_ref[...])
    parts = (_rms_unit(nsa_ref[...]), sc_ref[...], diff_ref[...], _rms_unit(s5o))
    acc = None
    for k, p in enumerate(parts):
        pk = (p * gain_ref[:, k * 512:(k + 1) * 512]).astype(bf16)
        d = _dot(pk, wo_ref[k * 512:(k + 1) * 512, :])
        acc = d if acc is None else acc + d
    o = _layer_norm(DN_ALPHA * x_ref[...] + acc, g_ref[...], b_ref[...])
    out_ref[...] = o
    outb_ref[...] = o.astype(bf16)


def _mix(nsa, sc, diff, s5y, gw, gb, gain, wo, x, g, b, tm, layer):
    m = x.shape[0]
    row = lambda w: pl.BlockSpec((tm, w), lambda i: (i, 0))
    full = lambda shp: pl.BlockSpec(shp, lambda i: (0, 0))
    stacked = lambda shp: pl.BlockSpec((None,) + shp, lambda i: (layer, 0, 0))
    return pl.pallas_call(
        _mix_kernel,
        grid=(m // tm,),
        in_specs=[row(512), row(512), row(512), row(512), stacked((512, 512)), full((1, 512)),
                  full((1, D_MODEL)), stacked((D_MODEL, D_MODEL)), row(D_MODEL), full((1, D_MODEL)),
                  full((1, D_MODEL))],
        out_specs=[row(D_MODEL), row(D_MODEL)],
        out_shape=[jax.ShapeDtypeStruct((m, D_MODEL), f32), jax.ShapeDtypeStruct((m, D_MODEL), bf16)],
        compiler_params=_cp(("parallel",)),
        name="mix_outproj_ln",
    )(nsa, sc, diff, s5y, gw, gb, gain, wo, x, g, b)


def _ffn_tail(acc_ref, xres_ref, g_ref, b_ref, out_ref, outb_ref):
    o = _layer_norm(DN_ALPHA * xres_ref[...] + acc_ref[...], g_ref[...], b_ref[...])
    out_ref[...] = o
    outb_ref[...] = o.astype(bf16)


def _ffn_prompt_kernel(x_ref, halo_ref, wa_ref, wb_ref, wd_ref, cw_ref, xres_ref, g_ref, b_ref,
                       out_ref, outb_ref, tail_ref, acc_ref, abuf_ref, *, tm, nf, tiles_per_seq):
    i = pl.program_id(0)
    f = pl.program_id(1)

    @pl.when(f == 0)
    def _():
        acc_ref[...] = jnp.zeros_like(acc_ref)

    x = x_ref[...]
    halo = halo_ref[...]
    keep = jnp.where(i % tiles_per_seq != 0, 1.0, 0.0)
    cw = cw_ref[...]
    down = None
    for c in range(abuf_ref.shape[0]):
        cols = slice(c * FFN_CHUNK, (c + 1) * FFN_CHUNK)
        a = _dot(x, wa_ref[:, cols])
        bb = _dot(x, wb_ref[:, cols])
        abuf_ref[c, 0:16, :] = _dot(halo, wa_ref[:, cols]) * keep
        abuf_ref[c, 16:16 + tm, :] = a
        ac = (cw[0:1, cols] * abuf_ref[c, 14:14 + tm, :] + cw[1:2, cols] * abuf_ref[c, 15:15 + tm, :]
              + cw[2:3, cols] * a)
        gate = (ac * _sigmoid(ac) * bb).astype(bf16)
        d = _dot(gate, wd_ref[cols, :])
        down = d if down is None else down + d
        tail_ref[0, :, cols] = a[tm - 8:tm]
    acc_ref[...] += down

    @pl.when(f == nf - 1)
    def _():
        _ffn_tail(acc_ref, xres_ref, g_ref, b_ref, out_ref, outb_ref)


def _ffn_prompt(xb, x, wup, wd, cw, g, b, t_len, layer):
    m = x.shape[0]
    tm, tf = 512, FFN_TF
    nf = D_FF // tf
    full = lambda shp: pl.BlockSpec(shp, lambda i, f: (0, 0))
    return pl.pallas_call(
        functools.partial(_ffn_prompt_kernel, tm=tm, nf=nf, tiles_per_seq=t_len // tm),
        grid=(m // tm, nf),
        in_specs=[pl.BlockSpec((tm, D_MODEL), lambda i, f: (i, 0)),
                  pl.BlockSpec((16, D_MODEL), lambda i, f: (jnp.maximum(i * (tm // 16) - 1, 0), 0)),
                  pl.BlockSpec((None, None, D_MODEL, tf), lambda i, f: (layer, f, 0, 0)),
                  pl.BlockSpec((None, None, D_MODEL, tf), lambda i, f: (layer, f + nf, 0, 0)),
                  pl.BlockSpec((None, tf, D_MODEL), lambda i, f: (layer, f, 0)),
                  pl.BlockSpec((3, tf), lambda i, f: (0, f)),
                  pl.BlockSpec((tm, D_MODEL), lambda i, f: (i, 0)),
                  full((1, D_MODEL)), full((1, D_MODEL))],
        out_specs=[pl.BlockSpec((tm, D_MODEL), lambda i, f: (i, 0)),
                   pl.BlockSpec((tm, D_MODEL), lambda i, f: (i, 0)),
                   pl.BlockSpec((1, 8, tf), lambda i, f: (i, 0, f))],
        out_shape=[jax.ShapeDtypeStruct((m, D_MODEL), f32), jax.ShapeDtypeStruct((m, D_MODEL), bf16),
                   jax.ShapeDtypeStruct((m // tm, 8, D_FF), f32)],
        scratch_shapes=[pltpu.VMEM((tm, D_MODEL), f32), pltpu.VMEM((tf // FFN_CHUNK, 16 + tm, FFN_CHUNK), f32)],
        compiler_params=_cp(("parallel", "arbitrary")),
        name="ffn_prompt",
    )(xb, xb, wup, wup, wd, cw, x, g, b)


def _ffn_sample_kernel(x_ref, p0_ref, p1_ref, wa_ref, wb_ref, wd_ref, cw_ref, xres_ref, g_ref, b_ref,
                       out_ref, outb_ref, aup_ref, acc_ref, *, nf):
    f = pl.program_id(0)

    @pl.when(f == 0)
    def _():
        acc_ref[...] = jnp.zeros_like(acc_ref)

    x = x_ref[...]
    a = _dot(x, wa_ref[...])
    bb = _dot(x, wb_ref[...])
    cw = cw_ref[...]
    ac = cw[0:1] * p0_ref[...] + cw[1:2] * p1_ref[...] + cw[2:3] * a
    gate = (ac * _sigmoid(ac) * bb).astype(bf16)
    acc_ref[...] += _dot(gate, wd_ref[...])
    aup_ref[...] = a

    @pl.when(f == nf - 1)
    def _():
        _ffn_tail(acc_ref, xres_ref, g_ref, b_ref, out_ref, outb_ref)


def _ffn_sample(xb, x, p0, p1, wup, wd, cw, g, b, layer):
    m = x.shape[0]
    tf = FFN_TF
    nf = D_FF // tf
    full = lambda shp: pl.BlockSpec(shp, lambda f: (0, 0))
    return pl.pallas_call(
        functools.partial(_ffn_sample_kernel, nf=nf),
        grid=(nf,),
        in_specs=[full((m, D_MODEL)), pl.BlockSpec((m, tf), lambda f: (0, f)),
                  pl.BlockSpec((m, tf), lambda f: (0, f)),
                  pl.BlockSpec((None, None, D_MODEL, tf), lambda f: (layer, f, 0, 0)),
                  pl.BlockSpec((None, None, D_MODEL, tf), lambda f: (layer, f + nf, 0, 0)),
                  pl.BlockSpec((None, tf, D_MODEL), lambda f: (layer, f, 0)),
                  pl.BlockSpec((3, tf), lambda f: (0, f)),
                  full((m, D_MODEL)), full((1, D_MODEL)), full((1, D_MODEL))],
        out_specs=[full((m, D_MODEL)), full((m, D_MODEL)), pl.BlockSpec((m, tf), lambda f: (0, f))],
        out_shape=[jax.ShapeDtypeStruct((m, D_MODEL), f32), jax.ShapeDtypeStruct((m, D_MODEL), bf16),
                   jax.ShapeDtypeStruct((m, D_FF), f32)],
        scratch_shapes=[pltpu.VMEM((m, D_MODEL), f32)],
        compiler_params=_cp(("arbitrary",)),
        name="ffn_sample",
    )(xb, p0, p1, wup, wup, wd, cw, x, g, b)


def _cmp_sel_sample_kernel(pt_ref, new_ref, qraw_ref, pe1_ref, pe2_ref, w1_ref, w2_ref, pool_ref,
                           ocmp_ref, idx_ref, buf_ref, x_ref, sb_ref, kvc_ref, cst_ref, sem_ref,
                           *, layer, n_pages, n_batch, past_len):
    b = pl.program_id(0)
    slot = b % 2
    n = n_pages * (PAGE_SIZE // CMP_STRIDE)

    def copy(bb, sl, p):
        return pltpu.make_async_copy(pool_ref.at[layer, pt_ref[bb, p]], buf_ref.at[sl, p], sem_ref.at[sl])

    def fetch(bb, sl):
        for p in range(n_pages):
            copy(bb, sl, p).start()

    @pl.when(b == 0)
    def _():
        fetch(0, 0)
        for idx, (pe_ref, w_ref) in enumerate(((pe1_ref, w1_ref), (pe2_ref, w2_ref))):
            tot = None
            for s in range(CMP_STRIDE):
                pes = jnp.broadcast_to(pe_ref[:, s * 256:(s + 1) * 256], (8, 256)).astype(bf16)
                d = _dot(pes, w_ref[s * 256:(s + 1) * 256, :])
                tot = d if tot is None else tot + d
            cst_ref[idx] = tot

    @pl.when(b + 1 < n_batch)
    def _():
        fetch(b + 1, 1 - slot)

    for p in range(n_pages):
        copy(b, slot, p).wait()

    r_i = _row((PAGE_SIZE, PAGE_SIZE))
    pick = jnp.where(_lane((PAGE_SIZE, PAGE_SIZE)) == CMP_STRIDE * (r_i % 8) + r_i // 8, 1.0, 0.0).astype(bf16)

    def regroup(p, carry):
        rows = pl.ds(pl.multiple_of(p * 8, 8), 8)
        for kv in range(2):
            tile = buf_ref[slot, p, kv].reshape(128, 128).astype(bf16)
            r = _dot_nt(pick, tile)
            for s in range(CMP_STRIDE):
                x_ref[s, rows, kv * 128:(kv + 1) * 128] = r[s * 8:(s + 1) * 8]
        return carry

    lax.fori_loop(0, n_pages, regroup, 0, unroll=8)

    a = bm = None
    for s in range(CMP_STRIDE):
        xs = x_ref[s].astype(bf16)
        da = _dot(xs, w1_ref[s * 256:(s + 1) * 256, :])
        db = _dot(xs, w2_ref[s * 256:(s + 1) * 256, :])
        a = da if a is None else a + da
        bm = db if bm is None else bm + db
    a = a + cst_ref[0, 0:1, :]
    bm = bm + cst_ref[1, 0:1, :]
    new8 = jnp.broadcast_to(new_ref[0], (8, 256)).astype(bf16)
    row0 = _row((8, 256)) == 0
    a_new = cst_ref[0] + jnp.where(row0, _dot(new8, w1_ref[0:256, :]), 0.0)
    b_new = cst_ref[1] + jnp.where(row0, _dot(new8, w2_ref[0:256, :]), 0.0)
    sb_ref[0:n, :] = bm
    sb_ref[n:n + 8, :] = b_new
    sb_ref[n + 8:n + 16, :] = jnp.zeros((8, 256), f32)
    kvc_ref[0:n, :] = a + sb_ref[1:n + 1, :]
    kvc_ref[n:n + 8, :] = a_new + sb_ref[n + 1:n + 9, :]
    kvc_ref[n + 8:NC_PAD, :] = jnp.zeros((NC_PAD - n - 8, 256), f32)

    qpos = past_len
    n_sel = past_len // SEL_BLOCK + 1
    n_cmp = n_sel * SEL_BLOCK // CMP_STRIDE - 1
    cur = qpos // SEL_BLOCK
    n_i = _lane((1, NC_PAD))
    maskc = ((16 * n_i + 31) <= qpos) & (n_i < n_cmp)
    nn = _row((NC_PAD, 256))
    j2 = _lane((NC_PAD, 256))
    cov = jnp.clip(jnp.minimum(16 * nn + 32, 64 * j2 + 64) - jnp.maximum(16 * nn, 64 * j2), 0, 32)
    cov = (jnp.where((nn < n_cmp) & (j2 < n_sel), cov, 0).astype(f32) * (1.0 / CMP_LEN)).astype(bf16)
    jj = _lane((1, 256))
    forced = (jj == 0) | (jj == cur) | (jj == cur - 1)
    kk = _row((256, 256))
    jjm = _lane((256, 256))
    eye = kk == jjm
    before = jnp.where(kk < jjm, 1.0, 0.0).astype(bf16)
    slot_id = _row((SEL_TOPK, 256))
    jj16 = _lane((SEL_TOPK, 256)).astype(f32)
    qraw = qraw_ref[0]
    for h in range(NSA_KV_HEADS):
        kc = kvc_ref[:, 0:128].astype(bf16)
        vc = kvc_ref[:, 128:256].astype(bf16)
        qr = (_nsa_q8(qraw, h, h) * SCALE).astype(bf16)
        p_c = _msoftmax(_dot_nt(qr, kc), maskc)
        ocmp_ref[0, h] = _dot(p_c.astype(bf16), vc)
        psum = jnp.broadcast_to(p_c[0:1] + p_c[1:2] + p_c[2:3] + p_c[3:4], (8, NC_PAD))
        p_hi, p_lo = _split_hi_lo(psum)
        imp = (_dot(p_hi, cov) + _dot(p_lo, cov))[0:1]
        imp = jnp.where(forced, imp + FORCE_BONUS, imp)
        imp = jnp.where(jj <= cur, imp, -FORCE_BONUS)
        imp = jnp.where(jj < n_sel, imp, -3e38)
        imp_j = jnp.broadcast_to(imp, (256, 256))
        imp_k = jnp.broadcast_to(jnp.sum(jnp.where(eye, imp_j, 0.0), axis=1, keepdims=True), (256, 256))
        beats = (imp_k > imp_j) | ((imp_k == imp_j) & (kk < jjm))
        rank = jnp.sum(jnp.where(beats, 1.0, 0.0), axis=0, keepdims=True)
        sel = jnp.where((rank < SEL_TOPK) & (jj < n_sel), 1.0, 0.0)
        pos = _dot(jnp.broadcast_to(sel, (8, 256)).astype(bf16), before)[0:1]
        hit = (jnp.broadcast_to(pos, (SEL_TOPK, 256)) == slot_id.astype(f32)) & (jnp.broadcast_to(sel, (SEL_TOPK, 256)) > 0.5)
        ids = jnp.sum(jnp.where(hit, jj16, 0.0), axis=1, keepdims=True)
        idx_ref[0, h * SEL_TOPK:(h + 1) * SEL_TOPK, :] = jnp.broadcast_to(ids, (SEL_TOPK, 128)).astype(jnp.int32)


def _cmp_sel_sample(page_table, new_rows, qraw, pe1, pe2, w1, w2, pool_t, layer, past_len):
    n_batch, n_pages = page_table.shape
    n = n_pages * (PAGE_SIZE // CMP_STRIDE)
    full = lambda shp: pl.BlockSpec(shp, lambda b, pt: (0,) * len(shp))
    grid_spec = pltpu.PrefetchScalarGridSpec(
        num_scalar_prefetch=1,
        grid=(n_batch,),
        in_specs=[pl.BlockSpec((1, 1, 256), lambda b, pt: (b, 0, 0)),
                  pl.BlockSpec((1, 1, 512), lambda b, pt: (b, 0, 0)), full((1, 4096)), full((1, 4096)),
                  full((4096, 256)), full((4096, 256)), pl.BlockSpec(memory_space=pl.ANY)],
        out_specs=[pl.BlockSpec((1, 2, 8, 128), lambda b, pt: (b, 0, 0, 0)),
                   pl.BlockSpec((1, 2 * SEL_TOPK, 128), lambda b, pt: (b, 0, 0))],
        scratch_shapes=[pltpu.VMEM((2, n_pages, 2, 2, HEAD_DIM, PAGE_SIZE), f32),
                        pltpu.VMEM((CMP_STRIDE, n, 256), f32), pltpu.VMEM((n + 16, 256), f32),
                        pltpu.VMEM((NC_PAD, 256), f32), pltpu.VMEM((2, 8, 256), f32),
                        pltpu.SemaphoreType.DMA((2,))],
    )
    return pl.pallas_call(
        functools.partial(_cmp_sel_sample_kernel, layer=layer, n_pages=n_pages, n_batch=n_batch,
                          past_len=past_len),
        grid_spec=grid_spec,
        out_shape=[jax.ShapeDtypeStruct((n_batch, 2, 8, 128), f32),
                   jax.ShapeDtypeStruct((n_batch, 2 * SEL_TOPK, 128), jnp.int32)],
        compiler_params=_cp(("arbitrary",)),
        name="cmp_sel_sample",
    )(page_table, new_rows, qraw, pe1, pe2, w1, w2, pool_t)


def _rows8(row_chunks):
    rid = _row((8, 128))
    out = jnp.zeros((8, 128), f32)
    for r, c in enumerate(row_chunks):
        out = jnp.where(rid == r, jnp.broadcast_to(c, (8, 128)), out)
    return out


def _nsa_q8(qrow, h, half):
    chunks = []
    for g in range(4):
        hd = h * 4 + g
        chunks.append(qrow[:, (hd // 2) * 128:(hd // 2 + 1) * 128])
    q8 = _rows8(chunks)
    sw = pltpu.roll(q8, 64, 1)
    in_place = (_row((8, 128)) % 2) == half
    q8 = jnp.where(in_place, q8, sw)
    return jnp.where((_lane((8, 128)) // 64) == half, q8, 0.0)


def _nsa_sel_sample_kernel(pt_ref, idx_ref, qrot_ref, gate_ref, ocmp_ref, snew_ref, wnew_ref, win_ref,
                           pool_ref, out_ref, nwin_ref, buf_ref, sem_ref, *, layer, n_batch, n_past_blocks):
    b = pl.program_id(0)
    slot = b % 2

    def copy(bb, sl, h, s, kv):
        j = jnp.minimum(idx_ref[bb, h * SEL_TOPK + s], n_past_blocks - 1)
        page = pt_ref[bb, j // (PAGE_SIZE // SEL_BLOCK)]
        return pltpu.make_async_copy(pool_ref.at[layer, page, kv], buf_ref.at[sl, h, kv, s], sem_ref.at[sl])

    def for_all(bb, sl, fn):
        for h in range(NSA_KV_HEADS):
            for s in range(SEL_TOPK):
                for kv in range(2):
                    fn(copy(bb, sl, h, s, kv))

    @pl.when(b == 0)
    def _():
        for_all(0, 0, lambda c: c.start())

    @pl.when(b + 1 < n_batch)
    def _():
        for_all(b + 1, 1 - slot, lambda c: c.start())

    gs = _sigmoid(gate_ref[0])
    lane128 = _lane((1, 128))
    qrot = qrot_ref[0]

    for_all(b, slot, lambda c: c.wait())

    snew = snew_ref[0]
    wnew = wnew_ref[0]
    o_all = []
    for h in range(NSA_KV_HEADS):
        qo = _nsa_q8(qrot, h, h) * SCALE
        qob = qo.astype(bf16)
        halfmask = (lane128 // 64) == h
        scores, valids = [], []
        has_new = False
        for s in range(SEL_TOPK):
            j = idx_ref[b, h * SEL_TOPK + s]
            kt = buf_ref[slot, h, 0, s].reshape(128, 128).astype(bf16)
            valid = ((lane128 // SEL_BLOCK) == (j % (PAGE_SIZE // SEL_BLOCK))) & (j < n_past_blocks)
            scores.append(jnp.where(valid, _dot(qob, kt), NEG_INF))
            valids.append(valid)
            has_new = jnp.logical_or(has_new, j == n_past_blocks)
        s_new = jnp.sum(qo * snew[:, 0:128], axis=-1, keepdims=True)
        s_new = jnp.where(has_new, s_new, NEG_INF)
        smax = scores[0]
        for sc in scores[1:]:
            smax = jnp.maximum(smax, sc)
        mx = jnp.maximum(jnp.max(smax, axis=-1, keepdims=True), s_new)
        e_new = jnp.where(has_new, jnp.exp(s_new - mx), 0.0)
        esum = jnp.zeros((8, 128), f32)
        acc = jnp.zeros((8, 128), f32)
        for s in range(SEL_TOPK):
            e = jnp.where(valids[s], jnp.exp(scores[s] - mx), 0.0)
            esum = esum + e
            vt = buf_ref[slot, h, 1, s].reshape(128, 128).astype(bf16)
            acc = acc + _dot_nt(e.astype(bf16), vt)
        inv = 1.0 / jnp.maximum(jnp.sum(esum, axis=-1, keepdims=True) + e_new, 1e-30)
        o_slc = (acc + e_new * snew[:, 128:256]) * inv
        wt = win_ref[0, 0]
        kt = wt[0].reshape(128, WINDOW).astype(bf16)
        vt = wt[1].reshape(128, WINDOW).astype(bf16)
        maskw = _lane((1, WINDOW)) >= 1
        s_w = jnp.where(maskw, _dot(qob, kt), NEG_INF)
        sw_new = jnp.sum(qo * wnew[:, 0:128], axis=-1, keepdims=True)
        mx = jnp.maximum(jnp.max(s_w, axis=-1, keepdims=True), sw_new)
        e = jnp.where(maskw, jnp.exp(s_w - mx), 0.0)
        e_new = jnp.exp(sw_new - mx)
        inv = 1.0 / jnp.maximum(jnp.sum(e, axis=-1, keepdims=True) + e_new, 1e-30)
        o_win = (_dot_nt(e.astype(bf16), vt) + e_new * wnew[:, 128:256]) * inv
        gate_rows = []
        for c in range(3):
            gate_rows.append(_rows8([jnp.broadcast_to(gs[:, (h * 4 + g) * 3 + c:(h * 4 + g) * 3 + c + 1], (1, 128))
                                     for g in range(4)]))
        o8 = gate_rows[0] * ocmp_ref[0, h] + gate_rows[1] * o_slc + gate_rows[2] * o_win
        o_all.append(jnp.where(halfmask, o8, 0.0))

    lo = lane128 < 64
    for h in range(NSA_KV_HEADS):
        o8 = o_all[h]
        o8s = pltpu.roll(o8, 64, 1)
        low_src, high_src = (o8, o8s) if h == 0 else (o8s, o8)
        for gp in range(2):
            ch = jnp.where(lo, low_src[2 * gp:2 * gp + 1], high_src[2 * gp + 1:2 * gp + 2])
            out_ref[0, :, h * 256 + gp * 128:h * 256 + (gp + 1) * 128] = ch

    last = _lane((1, WINDOW)) == WINDOW - 1
    eye64 = _row((64, 64)) == _lane((64, 64))
    for kv in range(2):
        for h in range(NSA_KV_HEADS):
            c = kv * 2 + h
            newc = jnp.broadcast_to(wnew[:, c * 64:(c + 1) * 64], (64, 64))
            colv = jnp.sum(jnp.where(eye64, newc, 0.0), axis=1, keepdims=True)
            old = win_ref[0, 0, kv, h]
            nwin_ref[0, kv, h] = jnp.where(last, colv, pltpu.roll(old, WINDOW - 1, 1))


def _nsa_sel_sample(page_table, sel_idx, qrot, gates, ocmp, snew, wnew, win_t, pool_t, layer, past_len):
    n_batch = page_table.shape[0]
    row = lambda w: pl.BlockSpec((1, 1, w), lambda b, pt, ix: (b, 0, 0))
    grid_spec = pltpu.PrefetchScalarGridSpec(
        num_scalar_prefetch=2,
        grid=(n_batch,),
        in_specs=[row(512), row(128), pl.BlockSpec((1, 2, 8, 128), lambda b, pt, ix: (b, 0, 0, 0)),
                  row(256), row(256),
                  pl.BlockSpec((1, 1, 2, 2, HEAD_DIM, WINDOW), lambda b, pt, ix: (layer, b, 0, 0, 0, 0)),
                  pl.BlockSpec(memory_space=pl.ANY)],
        out_specs=[row(512), pl.BlockSpec((1, 2, 2, HEAD_DIM, WINDOW), lambda b, pt, ix: (b, 0, 0, 0, 0))],
        scratch_shapes=[pltpu.VMEM((2, NSA_KV_HEADS, 2, SEL_TOPK, 2, HEAD_DIM, PAGE_SIZE), f32),
                        pltpu.SemaphoreType.DMA((2,))],
    )
    return pl.pallas_call(
        functools.partial(_nsa_sel_sample_kernel, layer=layer, n_batch=n_batch,
                          n_past_blocks=past_len // SEL_BLOCK),
        grid_spec=grid_spec,
        out_shape=[jax.ShapeDtypeStruct((n_batch, 1, 512), f32),
                   jax.ShapeDtypeStruct((n_batch, 2, 2, HEAD_DIM, WINDOW), f32)],
        compiler_params=_cp(("arbitrary",)),
        name="nsa_sel_sample",
    )(page_table, sel_idx, qrot, gates, ocmp, snew, wnew, win_t, pool_t)


def _diff_sample_kernel(pt_ref, q_ref, new_ref, dl_ref, pool_ref, out_ref, buf_ref, m_ref, l_ref,
                        acc_ref, sem_ref, *, layer, n_pages, n_batch, n_split, lam_init):
    b = pl.program_id(0)
    hf = pl.program_id(1)
    step = b * n_split + hf
    slot = step % 2
    pps = n_pages // n_split
    rows = pps * PAGE_SIZE

    def copy(bb, hh, sl, p):
        return pltpu.make_async_copy(pool_ref.at[layer, pt_ref[bb, hh * pps + p]],
                                     buf_ref.at[sl, pl.ds(p * 4 * PAGE_SIZE, 4 * PAGE_SIZE), :],
                                     sem_ref.at[sl])

    def fetch(bb, hh, sl):
        for p in range(pps):
            copy(bb, hh, sl, p).start()

    @pl.when(step == 0)
    def _():
        fetch(0, 0, 0)

    @pl.when(step + 1 < n_batch * n_split)
    def _():
        nxt = step + 1
        fetch(nxt // n_split, nxt % n_split, 1 - slot)

    @pl.when(hf == 0)
    def _():
        m_ref[...] = jnp.full(m_ref.shape, NEG_INF, f32)
        l_ref[...] = jnp.zeros(l_ref.shape, f32)
        acc_ref[...] = jnp.zeros(acc_ref.shape, f32)

    for p in range(pps):
        copy(b, hf, slot, p).wait()

    qrow = q_ref[0]
    lane = _lane((8, 128))
    rid = _row((8, 128))
    q8s = []
    for h in range(2):
        q8 = _rows8([qrow[:, h * 256 + (r // 2) * 128:h * 256 + (r // 2 + 1) * 128] for r in range(4)])
        q8s.append(jnp.where((lane // 64) == (rid % 2), q8, 0.0) * SCALE)
    for h in range(2):
        k = buf_ref[slot, pl.ds(h, rows, stride=4), :].astype(bf16)
        v = buf_ref[slot, pl.ds(2 + h, rows, stride=4), :].astype(bf16)
        s = _dot_nt(q8s[h].astype(bf16), k)
        m_old = m_ref[h]
        m_new = jnp.maximum(m_old, jnp.max(s, axis=-1, keepdims=True))
        alpha = jnp.exp(m_old - m_new)
        p_ = jnp.exp(s - m_new)
        l_ref[h] = alpha * l_ref[h] + jnp.sum(p_, axis=-1, keepdims=True)
        acc_ref[h] = alpha * acc_ref[h] + _dot(p_.astype(bf16), v)
        m_ref[h] = m_new

    @pl.when(hf == n_split - 1)
    def _():
        lam = _diff_lambda(dl_ref[...], lam_init)
        new = new_ref[0]
        for h in range(2):
            s_new = jnp.sum(q8s[h] * new[:, h * 128:(h + 1) * 128], axis=-1, keepdims=True)
            m_old = m_ref[h]
            m_new = jnp.maximum(m_old, s_new)
            alpha = jnp.exp(m_old - m_new)
            p_new = jnp.exp(s_new - m_new)
            l_ = alpha * l_ref[h] + p_new
            acc = alpha * acc_ref[h] + p_new * new[:, 256 + h * 128:256 + (h + 1) * 128]
            o = acc * (1.0 / jnp.maximum(l_, 1e-30))
            for g in range(2):
                og = o[2 * g:2 * g + 1] - lam * o[2 * g + 1:2 * g + 2]
                out_ref[0, :, (h * 2 + g) * 128:(h * 2 + g + 1) * 128] = _rms_unit(og) * (1.0 - lam_init)


def _diff_sample(page_table, dqrot, new_rows, dl, pool, layer, lam_init):
    n_batch, n_pages = page_table.shape
    n_split = 2
    pps = n_pages // n_split
    row = lambda w: pl.BlockSpec((1, 1, w), lambda b, s, pt: (b, 0, 0))
    grid_spec = pltpu.PrefetchScalarGridSpec(
        num_scalar_prefetch=1,
        grid=(n_batch, n_split),
        in_specs=[row(512), row(512), pl.BlockSpec((4, 64), lambda b, s, pt: (0, 0)),
                  pl.BlockSpec(memory_space=pl.ANY)],
        out_specs=row(512),
        scratch_shapes=[pltpu.VMEM((2, pps * 4 * PAGE_SIZE, 128), f32), pltpu.VMEM((2, 8, 1), f32),
                        pltpu.VMEM((2, 8, 1), f32), pltpu.VMEM((2, 8, 128), f32),
                        pltpu.SemaphoreType.DMA((2,))],
    )
    return pl.pallas_call(
        functools.partial(_diff_sample_kernel, layer=layer, n_pages=n_pages, n_batch=n_batch,
                          n_split=n_split, lam_init=lam_init),
        grid_spec=grid_spec,
        out_shape=jax.ShapeDtypeStruct((n_batch, 1, 512), f32),
        compiler_params=_cp(("arbitrary", "arbitrary")),
        name="diff_sample",
    )(page_table, dqrot, new_rows, dl, pool)


def _prep_w_in(w):
    parts = jnp.split(w, [sum(IN_SPLITS[:i + 1]) for i in range(len(IN_SPLITS) - 1)], axis=-1)
    nq, ncmp, nslc, nwin, ngate, scb, scc, sch, dq, dk, dv, s5u = parts
    gate = jnp.pad(ngate, ((0, 0), (0, HC - C_GATE - ngate.shape[1])))
    wcat = jnp.concatenate([nq, scb, scc, sch, dq, s5u, ncmp, nslc, nwin, dk, dv, gate], axis=-1).astype(bf16)
    return jnp.transpose(wcat.reshape(D_MODEL, HC // IN_TN, IN_TN), (1, 0, 2))


def _prep_phi(pe, w):
    w2 = w.reshape(2, 2, CMP_STRIDE, HEAD_DIM, HEAD_DIM)
    wc = jnp.repeat(w2, 2, axis=0)
    eye = jnp.eye(4, dtype=f32)
    ws, pes = [], []
    for half in range(2):
        ws.append(jnp.einsum('csde,cf->scdfe', wc[:, half], eye).reshape(4096, 256).astype(bf16))
        pc = jnp.repeat(pe[:, half * CMP_STRIDE:(half + 1) * CMP_STRIDE], 2, axis=0)
        pes.append(jnp.transpose(pc, (1, 0, 2)).reshape(1, 4096))
    return pes[0], pes[1], ws[0], ws[1]


def _prep_s5(bbr, bbi, c_re, c_im):
    eye = jnp.eye(8, dtype=f32)

    def wb_of(bb):
        x = bb.reshape(S5_CH, S5_SLABS, 8, S5_STATE)
        return jnp.einsum('csgp,hg->shcgp', x, eye).reshape(S5_SLABS, 128, 512)

    wb = jnp.concatenate([wb_of(bbr), wb_of(bbi)], axis=-1).astype(bf16)

    def wc_of(c):
        x = c.reshape(S5_SLABS, 8, S5_CH, S5_STATE)
        return jnp.einsum('sgcp,hg->shpgc', x, eye).reshape(S5_SLABS, 512, 128)

    wc = jnp.concatenate([wc_of(c_re), -wc_of(c_im)], axis=1).astype(bf16)
    return wb, wc


def _rope_tables(pos):
    half = HEAD_DIM // 2
    inv = ROPE_THETA ** (-jnp.arange(half, dtype=f32) / half)
    ang = pos.astype(f32)[:, None] * inv[None, :]
    c, s = jnp.cos(ang), jnp.sin(ang)
    return jnp.tile(c, (1, 4)), jnp.tile(jnp.concatenate([-s, s], axis=1), (1, 2))


def kernel(x_prompt, x_sample, cache_nsa_cmp, cache_nsa_slc, cache_diff, state_nsa_win, state_sconv, state_s5_re, state_s5_im, state_ffn_conv, page_table, w_in, nsa_phi_pe, nsa_phi_w, sc_conv_w, diff_lambda, s5_a_re, s5_a_im, s5_log_dt, s5_b_re, s5_b_im, s5_c_re, s5_c_im, s5_d, s5_glu_w, s5_glu_b, mix_gain, w_out, ln1_g, ln1_b, ffn_w_up, ffn_conv_w, ffn_w_down, ln2_g, ln2_b):
    bp, t_len, _ = x_prompt.shape
    bs = x_sample.shape[0]
    n_pool = cache_nsa_cmp.shape[1]
    n_pages = page_table.shape[1]
    past_len = n_pages * PAGE_SIZE
    n_state = S5_GROUPS * S5_STATE
    mp = bp * t_len

    cos_p, sin_p = _rope_tables(jnp.arange(t_len))
    cos_s, sin_s = _rope_tables(jnp.full((bs,), past_len))

    pool_cmp = jnp.transpose(cache_nsa_cmp, (0, 1, 3, 4, 5, 2))
    pool_slc = jnp.transpose(cache_nsa_slc, (0, 1, 3, 4, 5, 2))
    pool_diff = cache_diff.reshape(DEPTH, n_pool, PAGE_SIZE * 4, 128)
    win_t = jnp.transpose(state_nsa_win, (0, 1, 3, 4, 5, 2))

    xp = x_prompt.reshape(mp, D_MODEL)
    xs = x_sample.reshape(bs, D_MODEL)
    xp_b, xs_b = xp.astype(bf16), xs.astype(bf16)

    outs_p = {k: [] for k in ('cmp', 'slc', 'win', 'diff', 'sc', 's5r', 's5i', 'ffn')}
    outs_s = {k: [] for k in ('cmp', 'slc', 'win', 'diff', 'sc', 's5r', 's5i', 'ffn')}

    gw = s5_glu_w.astype(bf16)
    wo = w_out.astype(bf16)
    wup = jnp.transpose(ffn_w_up.astype(bf16).reshape(DEPTH, D_MODEL, 2 * D_FF // FFN_TF, FFN_TF), (0, 2, 1, 3))
    wdn = ffn_w_down.astype(bf16)

    for l in range(DEPTH):
        lam_init = 0.8 - 0.6 * math.exp(-0.3 * l)
        w_in_l = _prep_w_in(w_in[l])
        pe1, pe2, w1, w2 = _prep_phi(nsa_phi_pe[l], nsa_phi_w[l])
        tabs, bbr, bbi = _s5_prep(s5_a_re[l].reshape(1, n_state), s5_a_im[l].reshape(1, n_state),
                                  jnp.repeat(s5_log_dt[l], S5_STATE).reshape(1, n_state),
                                  jnp.transpose(s5_b_re[l], (2, 0, 1)).reshape(S5_CH, n_state),
                                  jnp.transpose(s5_b_im[l], (2, 0, 1)).reshape(S5_CH, n_state))
        wb5, wc5 = _prep_s5(bbr, bbi, s5_c_re[l], s5_c_im[l])
        d5 = s5_d[l].reshape(1, 512)
        gb = s5_glu_b[l].reshape(1, 512)
        gain = mix_gain[l].reshape(1, D_MODEL)
        g1, b1 = ln1_g[l].reshape(1, D_MODEL), ln1_b[l].reshape(1, D_MODEL)
        g2, b2 = ln2_g[l].reshape(1, D_MODEL), ln2_b[l].reshape(1, D_MODEL)
        cwf = ffn_conv_w[l]
        scw = sc_conv_w[l]
        dl = diff_lambda[l]

        hcat = _in_proj(xp_b, w_in_l, 1024)
        qrot, dqrot, kvslc, kvwin, kvdiff = _rope(hcat, cos_p, sin_p, 512)
        kvcmp = hcat[:, C_CMP:C_CMP + 256]
        kvc = _cmp_prompt(kvcmp.reshape(mp // CMP_STRIDE, CMP_STRIDE * 256), pe1, pe2, w1, w2, bp)
        nsa = _nsa_prompt(hcat, qrot, kvc, kvslc, kvwin, bp, t_len)
        dif = _diff_prompt(dqrot, kvdiff, dl, bp, t_len, lam_init)
        sc, sc_tail = _sconv_prompt(hcat, scw, bp, t_len)
        s5y, s5r, s5i = _s5_scan(hcat, wb5, wc5, d5, tabs, bp, t_len)
        x1, x1b = _mix(nsa, sc, dif, s5y, gw, gb, gain, wo, xp, g1, b1, 256, l)
        xp, xp_b, ffn_tail = _ffn_prompt(x1b, x1, wup, wdn, cwf, g2, b2, t_len, l)

        outs_p['cmp'].append(kvcmp.reshape(bp, t_len, 2, 2, HEAD_DIM))
        outs_p['slc'].append(kvslc.reshape(bp, t_len, 2, 2, HEAD_DIM))
        outs_p['win'].append(kvwin.reshape(bp, t_len, 2, 2, HEAD_DIM)[:, t_len - WINDOW:])
        outs_p['diff'].append(kvdiff.reshape(bp, t_len, 2, 2, 2 * HEAD_DIM))
        outs_p['sc'].append(sc_tail[:, 6:8])
        outs_p['s5r'].append(s5r.reshape(bp, S5_GROUPS, S5_STATE))
        outs_p['s5i'].append(s5i.reshape(bp, S5_GROUPS, S5_STATE))
        tiles_per_seq = ffn_tail.shape[0] // bp
        outs_p['ffn'].append(ffn_tail.reshape(bp, tiles_per_seq, 8, D_FF)[:, -1, 6:8])

        hs = _in_proj(xs_b, w_in_l, bs)
        qrot_s, dqrot_s, kvslc_s, kvwin_s, kvdiff_s = _rope(hs, cos_s, sin_s, bs)
        kvcmp_s = hs[:, C_CMP:C_CMP + 256]
        ocmp_s, sel_s = _cmp_sel_sample(page_table, kvcmp_s.reshape(bs, 1, 256),
                                        hs[:, C_NQ:C_NQ + 512].reshape(bs, 1, 512), pe1, pe2, w1, w2,
                                        pool_cmp, l, past_len)
        nsa_s, nwin_t = _nsa_sel_sample(page_table, sel_s[:, :, 0], qrot_s.reshape(bs, 1, 512),
                                        hs[:, C_GATE:C_GATE + 128].reshape(bs, 1, 128), ocmp_s,
                                        kvslc_s.reshape(bs, 1, 256), kvwin_s.reshape(bs, 1, 256),
                                        win_t, pool_slc, l, past_len)
        dif_s = _diff_sample(page_table, dqrot_s.reshape(bs, 1, 512), kvdiff_s.reshape(bs, 1, 512),
                             dl, pool_diff, l, lam_init)
        scp = jnp.transpose(state_sconv[l], (1, 0, 2))
        sc_s, z_s, s5y_s, s5r_s, s5i_s = _sample_small(
            hs[:, C_SCB:C_SCB + 512], hs[:, C_SCC:C_SCC + 512], hs[:, C_SCH:C_SCH + 512], scw, scp,
            hs[:, C_S5U:C_S5U + 512], wb5, wc5, d5, tabs,
            state_s5_re[l].reshape(bs, n_state), state_s5_im[l].reshape(bs, n_state))
        x1s, x1sb = _mix(nsa_s.reshape(bs, 512), sc_s, dif_s.reshape(bs, 512), s5y_s, gw, gb, gain, wo,
                         xs, g1, b1, bs, l)
        prev_ffn = state_ffn_conv[l]
        xs, xs_b, aup_s = _ffn_sample(x1sb, x1s, prev_ffn[:, 0], prev_ffn[:, 1], wup, wdn, cwf, g2, b2, l)

        outs_s['cmp'].append(kvcmp_s.reshape(bs, 1, 2, 2, HEAD_DIM))
        outs_s['slc'].append(kvslc_s.reshape(bs, 1, 2, 2, HEAD_DIM))
        outs_s['win'].append(jnp.transpose(nwin_t, (0, 4, 1, 2, 3)))
        outs_s['diff'].append(kvdiff_s.reshape(bs, 1, 2, 2, 2 * HEAD_DIM))
        outs_s['sc'].append(jnp.stack([state_sconv[l][:, 1], z_s], axis=1))
        outs_s['s5r'].append(s5r_s.reshape(bs, S5_GROUPS, S5_STATE))
        outs_s['s5i'].append(s5i_s.reshape(bs, S5_GROUPS, S5_STATE))
        outs_s['ffn'].append(jnp.stack([prev_ffn[:, 1], aup_s], axis=1))

    order = ('cmp', 'slc', 'win', 'diff', 'sc', 's5r', 's5i', 'ffn')
    res = [xp.reshape(bp, t_len, D_MODEL), xs.reshape(bs, 1, D_MODEL)]
    res += [jnp.stack(outs_p[k], axis=0) for k in order]
    res += [jnp.stack(outs_s[k], axis=0) for k in order]
    return tuple(res)
```

```python
import functools
import math

import jax
import jax.numpy as jnp
from jax import lax
from jax.experimental import pallas as pl
from jax.experimental.pallas import tpu as pltpu

f32 = jnp.float32
bf16 = jnp.bfloat16

D_MODEL = 2048
DEPTH = 2
PAGE_SIZE = 128
HEAD_DIM = 64
GROUP_WIDTH = D_MODEL // 4
NSA_KV_HEADS = 2
NSA_GROUP = 4
CMP_STRIDE = 16
CMP_LEN = 32
SEL_BLOCK = 64
SEL_TOPK = 16
WINDOW = 512
FORCE_BONUS = 1e4
CONV_W = 3
S5_CH = 16
S5_GROUPS = 32
S5_STATE = 64
D_FF = 5632
ROPE_THETA = 10000.0
QBLOCK = 128
LN_EPS = 1e-5
RMS_EPS = 1e-6
NEG_INF = -1e30
DN_ALPHA = (2 * DEPTH) ** 0.25
SCALE = HEAD_DIM ** -0.5
LOG2E = math.log2(math.e)

IN_SPLITS = (512, 256, 256, 256, 24, 512, 512, 512, 512, 256, 256, 512)
C_NQ, C_SCB, C_SCC, C_SCH, C_DQ, C_S5U = 0, 512, 1024, 1536, 2048, 2560
C_CMP, C_SLC, C_WIN, C_DK, C_DV, C_GATE = 3072, 3328, 3584, 3840, 4096, 4352
HC = 4608
IN_TN = 768
FFN_TF = 512

VMEM_CAP_V7X = 64 * 1024 * 1024
VMEM_LIMIT = 56 * 1024 * 1024
NC_PAD = 640
S5_SLABS = 4
KEY_CLASS_BLOCKS = 2
FFN_CHUNK = 256


def _cp(sem):
    return pltpu.CompilerParams(dimension_semantics=sem, vmem_limit_bytes=VMEM_LIMIT)


def _dot(a, b):
    return jnp.dot(a, b, preferred_element_type=f32)


def _dot_nt(a, b):
    return lax.dot_general(a, b, (((1,), (1,)), ((), ())), preferred_element_type=f32)


def _lane(shape):
    return lax.broadcasted_iota(jnp.int32, shape, len(shape) - 1)


def _row(shape):
    return lax.broadcasted_iota(jnp.int32, shape, len(shape) - 2)


def _msoftmax(s, mask):
    s = jnp.where(mask, s, NEG_INF)
    m = jnp.max(s, axis=-1, keepdims=True)
    e = jnp.where(mask, jnp.exp(s - m), 0.0)
    return e * (1.0 / jnp.maximum(jnp.sum(e, axis=-1, keepdims=True), 1e-30))


def _exp2_softmax(s, bias):
    s = s + bias[None]
    e = jnp.exp2(s - jnp.max(s, axis=-1, keepdims=True))
    return e, 1.0 / jnp.maximum(jnp.sum(e, axis=-1, keepdims=True), 1e-30)


def _sigmoid(x):
    return 1.0 / (1.0 + jnp.exp(-x))


def _rms_unit(x):
    return x * lax.rsqrt(jnp.mean(x * x, axis=-1, keepdims=True) + RMS_EPS)


def _layer_norm(z, g, b):
    mu = jnp.mean(z, axis=-1, keepdims=True)
    d = z - mu
    var = jnp.mean(d * d, axis=-1, keepdims=True)
    return d * lax.rsqrt(var + LN_EPS) * g + b


def _split_hi_lo(x):
    hi = x.astype(bf16)
    lo = (x - hi.astype(f32)).astype(bf16)
    return hi, lo


def _matmul_kernel(x_ref, w_ref, o_ref):
    o_ref[...] = _dot(x_ref[...], w_ref[...])


def _in_proj(xb, w, tm):
    m = xb.shape[0]
    tn = IN_TN
    return pl.pallas_call(
        _matmul_kernel,
        grid=(m // tm, HC // tn),
        in_specs=[pl.BlockSpec((tm, D_MODEL), lambda i, j: (i, 0)),
                  pl.BlockSpec((None, D_MODEL, tn), lambda i, j: (j, 0, 0))],
        out_specs=pl.BlockSpec((tm, tn), lambda i, j: (i, j)),
        out_shape=jax.ShapeDtypeStruct((m, HC), f32),
        compiler_params=_cp(("parallel", "arbitrary")),
        name="in_proj",
    )(xb, w)


def _rope_cols(x, cos, sin):
    outs = []
    first = (_lane((1, 128)) % 64) < 32
    for c in range(x.shape[1] // 128):
        xc = x[:, c * 128:(c + 1) * 128]
        sw = jnp.where(first, pltpu.roll(xc, 96, 1), pltpu.roll(xc, 32, 1))
        outs.append(xc * cos + sw * sin)
    return outs


def _rope_kernel(nq_ref, dq_ref, slc_ref, win_ref, dk_ref, dv_ref, cos_ref, sin_ref,
                 qrot_ref, dqrot_ref, kvslc_ref, kvwin_ref, kvdiff_ref):
    cos = cos_ref[...]
    sin = sin_ref[...]
    for c, v in enumerate(_rope_cols(nq_ref[...], cos, sin)):
        qrot_ref[:, c * 128:(c + 1) * 128] = v
    for c, v in enumerate(_rope_cols(dq_ref[...], cos, sin)):
        dqrot_ref[:, c * 128:(c + 1) * 128] = v
    kvslc_ref[:, 0:128] = _rope_cols(slc_ref[:, 0:128], cos, sin)[0]
    kvslc_ref[:, 128:256] = slc_ref[:, 128:256]
    kvwin_ref[:, 0:128] = _rope_cols(win_ref[:, 0:128], cos, sin)[0]
    kvwin_ref[:, 128:256] = win_ref[:, 128:256]
    for c, v in enumerate(_rope_cols(dk_ref[...], cos, sin)):
        kvdiff_ref[:, c * 128:(c + 1) * 128] = v
    kvdiff_ref[:, 256:512] = dv_ref[...]


def _rope(hcat, cos, sin, tr):
    m = hcat.shape[0]
    nt = cos.shape[0] // tr

    def col(w, off):
        return pl.BlockSpec((tr, w), lambda i: (i, off // w))

    tab = pl.BlockSpec((tr, 128), lambda i: (i % nt, 0))
    return pl.pallas_call(
        _rope_kernel,
        grid=(m // tr,),
        in_specs=[col(512, C_NQ), col(512, C_DQ), col(256, C_SLC), col(256, C_WIN),
                  col(256, C_DK), col(256, C_DV), tab, tab],
        out_specs=[pl.BlockSpec((tr, 512), lambda i: (i, 0)),
                   pl.BlockSpec((tr, 512), lambda i: (i, 0)),
                   pl.BlockSpec((tr, 256), lambda i: (i, 0)),
                   pl.BlockSpec((tr, 256), lambda i: (i, 0)),
                   pl.BlockSpec((tr, 512), lambda i: (i, 0))],
        out_shape=[jax.ShapeDtypeStruct((m, 512), f32), jax.ShapeDtypeStruct((m, 512), f32),
                   jax.ShapeDtypeStruct((m, 256), f32), jax.ShapeDtypeStruct((m, 256), f32),
                   jax.ShapeDtypeStruct((m, 512), f32)],
        compiler_params=_cp(("parallel",)),
        name="rope",
    )(hcat, hcat, hcat, hcat, hcat, hcat, cos, sin)


def _cmp_prompt_kernel(z_ref, pe1_ref, pe2_ref, w1_ref, w2_ref, o_ref, sb_ref):
    z = z_ref[...]
    a = _dot((z + pe1_ref[...]).astype(bf16), w1_ref[...])
    bm = _dot((z + pe2_ref[...]).astype(bf16), w2_ref[...])
    n = z.shape[0]
    sb_ref[0:n, :] = bm
    sb_ref[n:n + 8, :] = jnp.zeros((8, 256), f32)
    o_ref[0] = a + sb_ref[1:n + 1, :]


def _cmp_prompt(z, pe1, pe2, w1, w2, bsz):
    n = z.shape[0] // bsz
    full = lambda shp: pl.BlockSpec(shp, lambda b: (0, 0))
    return pl.pallas_call(
        _cmp_prompt_kernel,
        grid=(bsz,),
        in_specs=[pl.BlockSpec((n, 4096), lambda b: (b, 0)), full((1, 4096)), full((1, 4096)),
                  full((4096, 256)), full((4096, 256))],
        out_specs=pl.BlockSpec((1, n, 256), lambda b: (b, 0, 0)),
        out_shape=jax.ShapeDtypeStruct((bsz, n, 256), f32),
        scratch_shapes=[pltpu.VMEM((n + 8, 256), f32)],
        compiler_params=_cp(("parallel",)),
        name="cmp_prompt",
    )(z, pe1, pe2, w1, w2)


def _nsa_qstack(blk, h):
    halfmask = (_lane((1, 128)) // 64) == h
    parts = []
    for g in range(4):
        c = blk[:, (g // 2) * 128:(g // 2 + 1) * 128]
        if g % 2 != h:
            c = pltpu.roll(c, 64, 1)
        parts.append(jnp.where(halfmask, c, 0.0))
    return jnp.concatenate(parts, axis=0)


def _nsa_assemble(o_list, h):
    lo = _lane((1, 128)) < 64
    chunks = []
    for gp in range(2):
        a, b = o_list[2 * gp], o_list[2 * gp + 1]
        if h == 0:
            b = pltpu.roll(b, 64, 1)
        else:
            a = pltpu.roll(a, 64, 1)
        chunks.append(jnp.where(lo, a, b))
    return chunks


def _nsa_prompt_body(qraw_ref, qrot_ref, gate_ref, kvc_ref, slc_ref, win_ref, cov_ref, exp_ref, out_ref,
                     *, s0, kmax, t_len):
    qb = QBLOCK
    qpos = s0 + _row((qb, 1))
    gs = _sigmoid(gate_ref[...])
    n_i = _lane((1, 128))
    maskc = ((16 * n_i + 31) <= qpos) & (n_i < 127)
    cov = cov_ref[...]
    expand = exp_ref[:, 0:kmax]
    causal = _lane((1, kmax)) <= qpos
    wlen = WINDOW + qb
    start = pl.multiple_of(jnp.clip(s0 - WINDOW, 0, t_len - wlen), 128)
    wpos = start + _lane((1, wlen))
    bias_w = jnp.where((wpos <= qpos) & ((qpos - wpos) < WINDOW), 0.0, NEG_INF)
    jj = _lane((1, 128))
    cur = qpos // SEL_BLOCK
    n_sel = kmax // SEL_BLOCK
    forced = (jj == 0) | (jj == cur) | (jj == cur - 1)

    for h in range(NSA_KV_HEADS):
        kc = kvc_ref[0, :, 0:128].astype(bf16)
        vc = kvc_ref[0, :, 128:256].astype(bf16)
        qr = (_nsa_qstack(qraw_ref[:, h * 256:(h + 1) * 256], h) * SCALE).astype(bf16)
        qo = (_nsa_qstack(qrot_ref[:, h * 256:(h + 1) * 256], h) * (SCALE * LOG2E)).astype(bf16)
        s_c = _dot_nt(qr, kc).reshape(4, qb, 128)
        p_c = _msoftmax(s_c, maskc[None])
        o_cmp = _dot(p_c.reshape(4 * qb, 128).astype(bf16), vc)
        psum = p_c[0] + p_c[1] + p_c[2] + p_c[3]
        p_hi, p_lo = _split_hi_lo(psum)
        imp = _dot(p_hi, cov) + _dot(p_lo, cov)
        imp = jnp.where(forced, imp + FORCE_BONUS, imp)
        imp = jnp.where(jj <= cur, imp, -FORCE_BONUS)
        imp = jnp.where(jj < n_sel, imp, -3e38)
        rank = jnp.zeros((qb, 128), f32)
        for k in range(n_sel):
            col = imp[:, k:k + 1]
            beats = (col > imp) | ((col == imp) & (jj > k))
            rank = rank + jnp.where(beats, 1.0, 0.0)
        sel = jnp.where((rank < SEL_TOPK) & (jj < n_sel), 1.0, 0.0).astype(bf16)
        bias_s = jnp.where((_dot(sel, expand) > 0.5) & causal, 0.0, NEG_INF)
        ks = slc_ref[0:kmax, 0:128].astype(bf16)
        vs = slc_ref[0:kmax, 128:256].astype(bf16)
        e_s, inv_s = _exp2_softmax(_dot_nt(qo, ks).reshape(4, qb, kmax), bias_s)
        o_slc = _dot(e_s.reshape(4 * qb, kmax).astype(bf16), vs) * inv_s.reshape(4 * qb, 1)
        kw = win_ref[pl.ds(start, wlen), 0:128].astype(bf16)
        vw = win_ref[pl.ds(start, wlen), 128:256].astype(bf16)
        e_w, inv_w = _exp2_softmax(_dot_nt(qo, kw).reshape(4, qb, wlen), bias_w)
        o_win = _dot(e_w.reshape(4 * qb, wlen).astype(bf16), vw) * inv_w.reshape(4 * qb, 1)
        o_list = []
        for g in range(NSA_GROUP):
            gi = (h * NSA_GROUP + g) * 3
            r = slice(g * qb, (g + 1) * qb)
            o_list.append(gs[:, gi:gi + 1] * o_cmp[r] + gs[:, gi + 1:gi + 2] * o_slc[r]
                          + gs[:, gi + 2:gi + 3] * o_win[r])
        for gp, ch in enumerate(_nsa_assemble(o_list, h)):
            out_ref[:, h * 256 + gp * 128:h * 256 + (gp + 1) * 128] = ch


def _by_key_class(body, t_len):
    i = pl.program_id(1)
    span = KEY_CLASS_BLOCKS * QBLOCK
    for c in range(t_len // span):
        @pl.when(i // KEY_CLASS_BLOCKS == c)
        def _(c=c):
            body(s0=i * QBLOCK, kmax=(c + 1) * span)


def _nsa_prompt_kernel(*refs, t_len):
    _by_key_class(functools.partial(_nsa_prompt_body, *refs, t_len=t_len), t_len)


def _cover_matrix(n_rows, n_cols, n_cmp, n_sel):
    n = jnp.arange(n_rows)[:, None]
    j = jnp.arange(n_cols)[None, :]
    cov = jnp.clip(jnp.minimum(CMP_STRIDE * n + CMP_LEN, SEL_BLOCK * (j + 1)) - jnp.maximum(CMP_STRIDE * n, SEL_BLOCK * j),
                   0, CMP_LEN)
    cov = jnp.where((n < n_cmp) & (j < n_sel), cov, 0).astype(f32) / CMP_LEN
    return cov.astype(bf16)


def _expand_matrix(n_rows, n_keys):
    return (jnp.arange(n_keys)[None, :] // SEL_BLOCK == jnp.arange(n_rows)[:, None]).astype(bf16)


def _nsa_prompt(hcat, qrot, kvc, kvslc, kvwin, bsz, t_len):
    m = hcat.shape[0]
    nqb = t_len // QBLOCK
    cov = _cover_matrix(128, 128, t_len // CMP_STRIDE - 1, t_len // SEL_BLOCK)
    expand = _expand_matrix(128, t_len)
    return pl.pallas_call(
        functools.partial(_nsa_prompt_kernel, t_len=t_len),
        grid=(bsz, nqb),
        in_specs=[pl.BlockSpec((QBLOCK, 512), lambda b, i: (b * nqb + i, 0)),
                  pl.BlockSpec((QBLOCK, 512), lambda b, i: (b * nqb + i, 0)),
                  pl.BlockSpec((QBLOCK, 128), lambda b, i: (b * nqb + i, C_GATE // 128)),
                  pl.BlockSpec((1, 128, 256), lambda b, i: (b, 0, 0)),
                  pl.BlockSpec((t_len, 256), lambda b, i: (b, 0)),
                  pl.BlockSpec((t_len, 256), lambda b, i: (b, 0)),
                  pl.BlockSpec((128, 128), lambda b, i: (0, 0)),
                  pl.BlockSpec((128, t_len), lambda b, i: (0, 0))],
        out_specs=pl.BlockSpec((QBLOCK, 512), lambda b, i: (b * nqb + i, 0)),
        out_shape=jax.ShapeDtypeStruct((m, 512), f32),
        compiler_params=_cp(("parallel", "arbitrary")),
        name="nsa_prompt",
    )(hcat, qrot, hcat, kvc, kvslc, kvwin, cov, expand)


def _diff_lambda(dl, lam_init):
    a = jnp.sum(dl[0:1, :] * dl[1:2, :], axis=-1, keepdims=True)
    b = jnp.sum(dl[2:3, :] * dl[3:4, :], axis=-1, keepdims=True)
    return jnp.exp(a) - jnp.exp(b) + lam_init


def _diff_prompt_body(dq_ref, kv_ref, dl_ref, out_ref, *, s0, kmax, lam_init):
    qb = QBLOCK
    qpos = s0 + _row((qb, 1))
    bias = jnp.where(_lane((1, kmax)) <= qpos, 0.0, NEG_INF)
    lam = _diff_lambda(dl_ref[...], lam_init)
    lane = _lane((1, 128))
    for h in range(2):
        k = kv_ref[0:kmax, h * 128:(h + 1) * 128].astype(bf16)
        v = kv_ref[0:kmax, 256 + h * 128:256 + (h + 1) * 128].astype(bf16)
        parts = []
        for g in range(2):
            c = dq_ref[:, h * 256 + g * 128:h * 256 + (g + 1) * 128] * (SCALE * LOG2E)
            for i in range(2):
                parts.append(jnp.where((lane // 64) == i, c, 0.0))
        q = jnp.concatenate(parts, axis=0).astype(bf16)
        e, inv = _exp2_softmax(_dot_nt(q, k).reshape(4, qb, kmax), bias)
        o = _dot(e.reshape(4 * qb, kmax).astype(bf16), v) * inv.reshape(4 * qb, 1)
        for g in range(2):
            og = o[2 * g * qb:(2 * g + 1) * qb] - lam * o[(2 * g + 1) * qb:(2 * g + 2) * qb]
            out_ref[:, (h * 2 + g) * 128:(h * 2 + g + 1) * 128] = _rms_unit(og) * (1.0 - lam_init)


def _diff_prompt_kernel(*refs, t_len, lam_init):
    _by_key_class(functools.partial(_diff_prompt_body, *refs, lam_init=lam_init), t_len)


def _diff_prompt(dqrot, kvdiff, dl, bsz, t_len, lam_init):
    m = dqrot.shape[0]
    nqb = t_len // QBLOCK
    return pl.pallas_call(
        functools.partial(_diff_prompt_kernel, t_len=t_len, lam_init=lam_init),
        grid=(bsz, nqb),
        in_specs=[pl.BlockSpec((QBLOCK, 512), lambda b, i: (b * nqb + i, 0)),
                  pl.BlockSpec((t_len, 512), lambda b, i: (b, 0)),
                  pl.BlockSpec((4, 64), lambda b, i: (0, 0))],
        out_specs=pl.BlockSpec((QBLOCK, 512), lambda b, i: (b * nqb + i, 0)),
        out_shape=jax.ShapeDtypeStruct((m, 512), f32),
        compiler_params=_cp(("parallel", "arbitrary")),
        name="diff_prompt",
    )(dqrot, kvdiff, dl)


def _sconv_prompt_kernel(b_ref, c_ref, h_ref, w_ref, out_ref, tail_ref, buf_ref, *, tr):
    t = pl.program_id(1)

    @pl.when(t == 0)
    def _():
        buf_ref[0:8, :] = jnp.zeros((8, 512), f32)

    z = c_ref[...] * h_ref[...]
    buf_ref[8:8 + tr, :] = z
    w = w_ref[...]
    y = w[0:1] * buf_ref[6:6 + tr, :] + w[1:2] * buf_ref[7:7 + tr, :] + w[2:3] * z
    out_ref[...] = _rms_unit(b_ref[...] * y)
    tail = z[tr - 8:tr]
    tail_ref[0] = tail
    buf_ref[0:8, :] = tail


def _sconv_prompt(hcat, w, bsz, t_len):
    tr = 512
    nt = t_len // tr
    m = hcat.shape[0]

    def col(off):
        return pl.BlockSpec((tr, 512), lambda b, t: (b * nt + t, off // 512))

    return pl.pallas_call(
        functools.partial(_sconv_prompt_kernel, tr=tr),
        grid=(bsz, nt),
        in_specs=[col(C_SCB), col(C_SCC), col(C_SCH), pl.BlockSpec((3, 512), lambda b, t: (0, 0))],
        out_specs=[pl.BlockSpec((tr, 512), lambda b, t: (b * nt + t, 0)),
                   pl.BlockSpec((1, 8, 512), lambda b, t: (b, 0, 0))],
        out_shape=[jax.ShapeDtypeStruct((m, 512), f32), jax.ShapeDtypeStruct((bsz, 8, 512), f32)],
        scratch_shapes=[pltpu.VMEM((8 + tr, 512), f32)],
        compiler_params=_cp(("parallel", "arbitrary")),
        name="sconv_prompt",
    )(hcat, hcat, hcat, w)


def _cmul(ar, ai, br, bi):
    return ar * br - ai * bi, ar * bi + ai * br


def _s5_prep_kernel(ar_ref, ai_ref, ldt_ref, br_ref, bi_ref, tab_ref, bbr_ref, bbi_ref):
    ar, ai = ar_ref[...], ai_ref[...]
    dt = jnp.exp(ldt_ref[...])
    mag = jnp.exp(ar * dt)
    abr, abi = mag * jnp.cos(ai * dt), mag * jnp.sin(ai * dt)
    den = ar * ar + ai * ai
    nr, ni = abr - 1.0, abi
    cre = (nr * ar + ni * ai) / den
    cim = (ni * ar - nr * ai) / den
    br, bi = br_ref[...], bi_ref[...]
    bbr_ref[...] = cre * br - cim * bi
    bbi_ref[...] = cre * bi + cim * br
    pw = [(abr, abi)]
    for _ in range(7):
        pw.append(_cmul(pw[-1][0], pw[-1][1], abr, abi))
    n = ar.shape[1]
    row = _row((8, n))
    zero = jnp.zeros((8, n), f32)
    for idx, (sh, p) in enumerate(((1, pw[0]), (2, pw[1]), (4, pw[3]))):
        tab_ref[2 * idx] = jnp.where(row >= sh, jnp.broadcast_to(p[0], (8, n)), zero)
        tab_ref[2 * idx + 1] = jnp.where(row >= sh, jnp.broadcast_to(p[1], (8, n)), zero)
    pr, pi = zero, zero
    for i in range(8):
        pr = jnp.where(row == i, jnp.broadcast_to(pw[i][0], (8, n)), pr)
        pi = jnp.where(row == i, jnp.broadcast_to(pw[i][1], (8, n)), pi)
    tab_ref[6] = pr
    tab_ref[7] = pi


def _s5_prep(ar, ai, ldt, br, bi):
    n = S5_GROUPS * S5_STATE
    return pl.pallas_call(
        _s5_prep_kernel,
        out_shape=[jax.ShapeDtypeStruct((8, 8, n), f32), jax.ShapeDtypeStruct((S5_CH, n), f32),
                   jax.ShapeDtypeStruct((S5_CH, n), f32)],
        name="s5_prep",
    )(ar, ai, ldt, br, bi)


def _s5_scan_kernel(u_ref, wb_ref, wc_ref, d_ref, tab_ref, y_ref, hr_ref, hi_ref,
                    xbuf_ref, cr_ref, ci_ref, *, tt):
    t = pl.program_id(2)

    @pl.when(t == 0)
    def _():
        cr_ref[...] = jnp.zeros((8, 512), f32)
        ci_ref[...] = jnp.zeros((8, 512), f32)

    u = u_ref[...]
    xbuf_ref[...] = _dot(u.astype(bf16), wb_ref[0])

    def body(r, carry):
        cr, ci = carry
        rows = pl.ds(pl.multiple_of(r * 8, 8), 8)
        xr = xbuf_ref[rows, 0:512]
        xi = xbuf_ref[rows, 512:1024]
        for idx, sh in enumerate((1, 2, 4)):
            a_r, a_i = tab_ref[2 * idx], tab_ref[2 * idx + 1]
            sr, si = pltpu.roll(xr, sh, 0), pltpu.roll(xi, sh, 0)
            xr, xi = xr + a_r * sr - a_i * si, xi + a_r * si + a_i * sr
        p_r, p_i = tab_ref[6], tab_ref[7]
        hr = xr + p_r * cr - p_i * ci
        hi = xi + p_r * ci + p_i * cr
        xbuf_ref[rows, 0:512] = hr
        xbuf_ref[rows, 512:1024] = hi
        return (jnp.broadcast_to(hr[7:8, :], (8, 512)), jnp.broadcast_to(hi[7:8, :], (8, 512)))

    cr, ci = lax.fori_loop(0, tt // 8, body, (cr_ref[...], ci_ref[...]), unroll=4)
    cr_ref[...] = cr
    ci_ref[...] = ci
    hr_ref[0] = cr[0:1, :]
    hi_ref[0] = ci[0:1, :]
    y_ref[...] = _dot(xbuf_ref[...].astype(bf16), wc_ref[0]) + d_ref[...] * u


def _s5_scan(hcat, wb, wc, d, tabs, bsz, t_len):
    tt = 512
    nt = t_len // tt
    m = hcat.shape[0]
    n = S5_GROUPS * S5_STATE
    return pl.pallas_call(
        functools.partial(_s5_scan_kernel, tt=tt),
        grid=(bsz, S5_SLABS, nt),
        in_specs=[pl.BlockSpec((tt, 128), lambda b, s, t: (b * nt + t, C_S5U // 128 + s)),
                  pl.BlockSpec((1, 128, 1024), lambda b, s, t: (s, 0, 0)),
                  pl.BlockSpec((1, 1024, 128), lambda b, s, t: (s, 0, 0)),
                  pl.BlockSpec((1, 128), lambda b, s, t: (0, s)),
                  pl.BlockSpec((8, 8, 512), lambda b, s, t: (0, 0, s))],
        out_specs=[pl.BlockSpec((tt, 128), lambda b, s, t: (b * nt + t, s)),
                   pl.BlockSpec((1, 1, 512), lambda b, s, t: (b, 0, s)),
                   pl.BlockSpec((1, 1, 512), lambda b, s, t: (b, 0, s))],
        out_shape=[jax.ShapeDtypeStruct((m, 512), f32), jax.ShapeDtypeStruct((bsz, 1, n), f32),
                   jax.ShapeDtypeStruct((bsz, 1, n), f32)],
        scratch_shapes=[pltpu.VMEM((tt, 1024), f32), pltpu.VMEM((8, 512), f32), pltpu.VMEM((8, 512), f32)],
        compiler_params=_cp(("parallel", "parallel", "arbitrary")),
        name="s5_scan",
    )(hcat, wb, wc, d, tabs)


def _sample_small_kernel(scb_ref, scc_ref, sch_ref, scw_ref, scp_ref, u_ref, wb_ref, wc_ref, d_ref,
                         tab_ref, h0r_ref, h0i_ref, sc_ref, z_ref, y_ref, hr_ref, hi_ref):
    z = scc_ref[...] * sch_ref[...]
    w = scw_ref[...]
    y = w[0:1] * scp_ref[0] + w[1:2] * scp_ref[1] + w[2:3] * z
    sc_ref[...] = _rms_unit(scb_ref[...] * y)
    z_ref[...] = z
    u = u_ref[...]
    for s in range(S5_SLABS):
        x = _dot(u[:, s * 128:(s + 1) * 128].astype(bf16), wb_ref[s])
        lanes = slice(s * 512, (s + 1) * 512)
        a_r, a_i = tab_ref[6, 0:1, lanes], tab_ref[7, 0:1, lanes]
        h0r, h0i = h0r_ref[:, lanes], h0i_ref[:, lanes]
        hr = a_r * h0r - a_i * h0i + x[:, 0:512]
        hi = a_r * h0i + a_i * h0r + x[:, 512:1024]
        hr_ref[:, lanes] = hr
        hi_ref[:, lanes] = hi
        hcat = jnp.concatenate([hr, hi], axis=1).astype(bf16)
        cols = slice(s * 128, (s + 1) * 128)
        y_ref[:, cols] = _dot(hcat, wc_ref[s]) + d_ref[:, cols] * u[:, cols]


def _sample_small(scb, scc, sch, scw, scp, u, wb, wc, d, tabs, h0r, h0i):
    bsz = u.shape[0]
    n = S5_GROUPS * S5_STATE
    return pl.pallas_call(
        _sample_small_kernel,
        out_shape=[jax.ShapeDtypeStruct((bsz, 512), f32), jax.ShapeDtypeStruct((bsz, 512), f32),
                   jax.ShapeDtypeStruct((bsz, 512), f32), jax.ShapeDtypeStruct((bsz, n), f32),
                   jax.ShapeDtypeStruct((bsz, n), f32)],
        compiler_params=pltpu.CompilerParams(vmem_limit_bytes=VMEM_LIMIT),
        name="sample_small",
    )(scb, scc, sch, scw, scp, u, wb, wc, d, tabs, h0r, h0i)


def _gelu(x):
    return 0.5 * x * (1.0 + jnp.tanh(math.sqrt(2.0 / math.pi) * (x + 0.044715 * (x * x * x))))


def _mix_kernel(nsa_ref, sc_ref, diff_ref, s5_ref, gw_ref, gb_ref, gain_ref, wo_ref, x_ref,
                g_ref, b_ref, out_ref, outb_ref):
    y = _gelu(s5_ref[...])
    s5o = y * _sigmoid(_dot(y.astype(bf16), gw_ref[...]) + gb_ref[...])
    parts = (_rms_unit(nsa_ref[...]), sc_ref[...], diff_ref[...], _rms_unit(s5o))
    acc = None
    for k, p in enumerate(parts):
        pk = (p * gain_ref[:, k * 512:(k + 1) * 512]).astype(bf16)
        d = _dot(pk, wo_ref[k * 512:(k + 1) * 512, :])
        acc = d if acc is None else acc + d
    o = _layer_norm(DN_ALPHA * x_ref[...] + acc, g_ref[...], b_ref[...])
    out_ref[...] = o
    outb_ref[...] = o.astype(bf16)


def _mix(nsa, sc, diff, s5y, gw, gb, gain, wo, x, g, b, tm, layer):
    m = x.shape[0]
    row = lambda w: pl.BlockSpec((tm, w), lambda i: (i, 0))
    full = lambda shp: pl.BlockSpec(shp, lambda i: (0, 0))
    stacked = lambda shp: pl.BlockSpec((None,) + shp, lambda i: (layer, 0, 0))
    return pl.pallas_call(
        _mix_kernel,
        grid=(m // tm,),
        in_specs=[row(512), row(512), row(512), row(512), stacked((512, 512)), full((1, 512)),
                  full((1, D_MODEL)), stacked((D_MODEL, D_MODEL)), row(D_MODEL), full((1, D_MODEL)),
                  full((1, D_MODEL))],
        out_specs=[row(D_MODEL), row(D_MODEL)],
        out_shape=[jax.ShapeDtypeStruct((m, D_MODEL), f32), jax.ShapeDtypeStruct((m, D_MODEL), bf16)],
        compiler_params=_cp(("parallel",)),
        name="mix_outproj_ln",
    )(nsa, sc, diff, s5y, gw, gb, gain, wo, x, g, b)


def _ffn_tail(acc_ref, xres_ref, g_ref, b_ref, out_ref, outb_ref):
    o = _layer_norm(DN_ALPHA * xres_ref[...] + acc_ref[...], g_ref[...], b_ref[...])
    out_ref[...] = o
    outb_ref[...] = o.astype(bf16)


def _ffn_prompt_kernel(x_ref, halo_ref, wa_ref, wb_ref, wd_ref, cw_ref, xres_ref, g_ref, b_ref,
                       out_ref, outb_ref, tail_ref, acc_ref, abuf_ref, *, tm, nf, tiles_per_seq):
    i = pl.program_id(0)
    f = pl.program_id(1)

    @pl.when(f == 0)
    def _():
        acc_ref[...] = jnp.zeros_like(acc_ref)

    x = x_ref[...]
    halo = halo_ref[...]
    keep = jnp.where(i % tiles_per_seq != 0, 1.0, 0.0)
    cw = cw_ref[...]
    down = None
    for c in range(abuf_ref.shape[0]):
        cols = slice(c * FFN_CHUNK, (c + 1) * FFN_CHUNK)
        a = _dot(x, wa_ref[:, cols])
        bb = _dot(x, wb_ref[:, cols])
        abuf_ref[c, 0:16, :] = _dot(halo, wa_ref[:, cols]) * keep
        abuf_ref[c, 16:16 + tm, :] = a
        ac = (cw[0:1, cols] * abuf_ref[c, 14:14 + tm, :] + cw[1:2, cols] * abuf_ref[c, 15:15 + tm, :]
              + cw[2:3, cols] * a)
        gate = (ac * _sigmoid(ac) * bb).astype(bf16)
        d = _dot(gate, wd_ref[cols, :])
        down = d if down is None else down + d
        tail_ref[0, :, cols] = a[tm - 8:tm]
    acc_ref[...] += down

    @pl.when(f == nf - 1)
    def _():
        _ffn_tail(acc_ref, xres_ref, g_ref, b_ref, out_ref, outb_ref)


def _ffn_prompt(xb, x, wup, wd, cw, g, b, t_len, layer):
    m = x.shape[0]
    tm, tf = 512, FFN_TF
    nf = D_FF // tf
    full = lambda shp: pl.BlockSpec(shp, lambda i, f: (0, 0))
    return pl.pallas_call(
        functools.partial(_ffn_prompt_kernel, tm=tm, nf=nf, tiles_per_seq=t_len // tm),
        grid=(m // tm, nf),
        in_specs=[pl.BlockSpec((tm, D_MODEL), lambda i, f: (i, 0)),
                  pl.BlockSpec((16, D_MODEL), lambda i, f: (jnp.maximum(i * (tm // 16) - 1, 0), 0)),
                  pl.BlockSpec((None, D_MODEL, tf), lambda i, f: (layer, 0, f)),
                  pl.BlockSpec((None, D_MODEL, tf), lambda i, f: (layer, 0, f + nf)),
                  pl.BlockSpec((None, tf, D_MODEL), lambda i, f: (layer, f, 0)),
                  pl.BlockSpec((3, tf), lambda i, f: (0, f)),
                  pl.BlockSpec((tm, D_MODEL), lambda i, f: (i, 0)),
                  full((1, D_MODEL)), full((1, D_MODEL))],
        out_specs=[pl.BlockSpec((tm, D_MODEL), lambda i, f: (i, 0)),
                   pl.BlockSpec((tm, D_MODEL), lambda i, f: (i, 0)),
                   pl.BlockSpec((1, 8, tf), lambda i, f: (i, 0, f))],
        out_shape=[jax.ShapeDtypeStruct((m, D_MODEL), f32), jax.ShapeDtypeStruct((m, D_MODEL), bf16),
                   jax.ShapeDtypeStruct((m // tm, 8, D_FF), f32)],
        scratch_shapes=[pltpu.VMEM((tm, D_MODEL), f32), pltpu.VMEM((tf // FFN_CHUNK, 16 + tm, FFN_CHUNK), f32)],
        compiler_params=_cp(("parallel", "arbitrary")),
        name="ffn_prompt",
    )(xb, xb, wup, wup, wd, cw, x, g, b)


def _ffn_sample_kernel(x_ref, p0_ref, p1_ref, wa_ref, wb_ref, wd_ref, cw_ref, xres_ref, g_ref, b_ref,
                       out_ref, outb_ref, aup_ref, acc_ref, *, nf):
    f = pl.program_id(0)

    @pl.when(f == 0)
    def _():
        acc_ref[...] = jnp.zeros_like(acc_ref)

    x = x_ref[...]
    a = _dot(x, wa_ref[...])
    bb = _dot(x, wb_ref[...])
    cw = cw_ref[...]
    ac = cw[0:1] * p0_ref[...] + cw[1:2] * p1_ref[...] + cw[2:3] * a
    gate = (ac * _sigmoid(ac) * bb).astype(bf16)
    acc_ref[...] += _dot(gate, wd_ref[...])
    aup_ref[...] = a

    @pl.when(f == nf - 1)
    def _():
        _ffn_tail(acc_ref, xres_ref, g_ref, b_ref, out_ref, outb_ref)


def _ffn_sample(xb, x, p0, p1, wup, wd, cw, g, b, layer):
    m = x.shape[0]
    tf = FFN_TF
    nf = D_FF // tf
    full = lambda shp: pl.BlockSpec(shp, lambda f: (0, 0))
    return pl.pallas_call(
        functools.partial(_ffn_sample_kernel, nf=nf),
        grid=(nf,),
        in_specs=[full((m, D_MODEL)), pl.BlockSpec((m, tf), lambda f: (0, f)),
                  pl.BlockSpec((m, tf), lambda f: (0, f)),
                  pl.BlockSpec((None, D_MODEL, tf), lambda f: (layer, 0, f)),
                  pl.BlockSpec((None, D_MODEL, tf), lambda f: (layer, 0, f + nf)),
                  pl.BlockSpec((None, tf, D_MODEL), lambda f: (layer, f, 0)),
                  pl.BlockSpec((3, tf), lambda f: (0, f)),
                  full((m, D_MODEL)), full((1, D_MODEL)), full((1, D_MODEL))],
        out_specs=[full((m, D_MODEL)), full((m, D_MODEL)), pl.BlockSpec((m, tf), lambda f: (0, f))],
        out_shape=[jax.ShapeDtypeStruct((m, D_MODEL), f32), jax.ShapeDtypeStruct((m, D_MODEL), bf16),
                   jax.ShapeDtypeStruct((m, D_FF), f32)],
        scratch_shapes=[pltpu.VMEM((m, D_MODEL), f32)],
        compiler_params=_cp(("arbitrary",)),
        name="ffn_sample",
    )(xb, p0, p1, wup, wup, wd, cw, x, g, b)


def _cmp_sel_sample_kernel(pt_ref, new_ref, qraw_ref, pe1_ref, pe2_ref, w1_ref, w2_ref, cov_ref, pool_ref,
                           ocmp_ref, idx_ref, buf_ref, x_ref, sb_ref, kvc_ref, cst_ref, sem_ref,
                           *, layer, n_pages, n_batch, past_len):
    b = pl.program_id(0)
    slot = b % 2
    n = n_pages * (PAGE_SIZE // CMP_STRIDE)

    def copy(bb, sl, p):
        return pltpu.make_async_copy(pool_ref.at[layer, pt_ref[bb, p]], buf_ref.at[sl, p], sem_ref.at[sl])

    def fetch(bb, sl):
        for p in range(n_pages):
            copy(bb, sl, p).start()

    @pl.when(b == 0)
    def _():
        fetch(0, 0)
        for idx, (pe_ref, w_ref) in enumerate(((pe1_ref, w1_ref), (pe2_ref, w2_ref))):
            tot = None
            for s in range(CMP_STRIDE):
                pes = jnp.broadcast_to(pe_ref[:, s * 256:(s + 1) * 256], (8, 256)).astype(bf16)
                d = _dot(pes, w_ref[s * 256:(s + 1) * 256, :])
                tot = d if tot is None else tot + d
            cst_ref[idx] = tot

    @pl.when(b + 1 < n_batch)
    def _():
        fetch(b + 1, 1 - slot)

    for p in range(n_pages):
        copy(b, slot, p).wait()

    r_i = _row((PAGE_SIZE, PAGE_SIZE))
    pick = jnp.where(_lane((PAGE_SIZE, PAGE_SIZE)) == CMP_STRIDE * (r_i % 8) + r_i // 8, 1.0, 0.0).astype(bf16)

    def regroup(q, carry):
        rows = pl.ds(pl.multiple_of(q * 16, 16), 16)
        r0 = _dot_nt(pick, buf_ref[slot, 2 * q].reshape(256, PAGE_SIZE).astype(bf16))
        r1 = _dot_nt(pick, buf_ref[slot, 2 * q + 1].reshape(256, PAGE_SIZE).astype(bf16))
        for s in range(CMP_STRIDE):
            pair = jnp.concatenate([r0[s * 8:(s + 1) * 8], r1[s * 8:(s + 1) * 8]], axis=0)
            x_ref[rows, s * 256:(s + 1) * 256] = pair.astype(bf16)
        return carry

    lax.fori_loop(0, n_pages // 2, regroup, 0, unroll=4)

    xb = x_ref[...]
    a = _dot(xb, w1_ref[...]) + cst_ref[0, 0:1, :]
    bm = _dot(xb, w2_ref[...]) + cst_ref[1, 0:1, :]
    new8 = jnp.broadcast_to(new_ref[0], (8, 256)).astype(bf16)
    row0 = _row((8, 256)) == 0
    a_new = cst_ref[0] + jnp.where(row0, _dot(new8, w1_ref[0:256, :]), 0.0)
    b_new = cst_ref[1] + jnp.where(row0, _dot(new8, w2_ref[0:256, :]), 0.0)
    sb_ref[0:n, :] = bm
    sb_ref[n:n + 8, :] = b_new
    sb_ref[n + 8:n + 16, :] = jnp.zeros((8, 256), f32)
    kvc_ref[0:n, :] = a + sb_ref[1:n + 1, :]
    kvc_ref[n:n + 8, :] = a_new + sb_ref[n + 1:n + 9, :]
    kvc_ref[n + 8:NC_PAD, :] = jnp.zeros((NC_PAD - n - 8, 256), f32)

    qpos = past_len
    n_sel = past_len // SEL_BLOCK + 1
    n_cmp = n_sel * SEL_BLOCK // CMP_STRIDE - 1
    cur = qpos // SEL_BLOCK
    n_i = _lane((1, NC_PAD))
    maskc = ((16 * n_i + 31) <= qpos) & (n_i < n_cmp)
    cov = cov_ref[...]
    jj = _lane((1, 256))
    forced = (jj == 0) | (jj == cur) | (jj == cur - 1)
    kk = _row((256, 256))
    jjm = _lane((256, 256))
    eye = kk == jjm
    before = jnp.where(kk < jjm, 1.0, 0.0).astype(bf16)
    slot_id = _row((SEL_TOPK, 256))
    jj16 = _lane((SEL_TOPK, 256)).astype(f32)
    qraw = qraw_ref[0]
    for h in range(NSA_KV_HEADS):
        kc = kvc_ref[:, 0:128].astype(bf16)
        vc = kvc_ref[:, 128:256].astype(bf16)
        qr = (_nsa_q8(qraw, h, h) * SCALE).astype(bf16)
        p_c = _msoftmax(_dot_nt(qr, kc), maskc)
        ocmp_ref[0, h] = _dot(p_c.astype(bf16), vc)
        psum = jnp.broadcast_to(p_c[0:1] + p_c[1:2] + p_c[2:3] + p_c[3:4], (8, NC_PAD))
        p_hi, p_lo = _split_hi_lo(psum)
        imp = (_dot(p_hi, cov) + _dot(p_lo, cov))[0:1]
        imp = jnp.where(forced, imp + FORCE_BONUS, imp)
        imp = jnp.where(jj <= cur, imp, -FORCE_BONUS)
        imp = jnp.where(jj < n_sel, imp, -3e38)
        imp_j = jnp.broadcast_to(imp, (256, 256))
        imp_k = jnp.broadcast_to(jnp.sum(jnp.where(eye, imp_j, 0.0), axis=1, keepdims=True), (256, 256))
        beats = (imp_k > imp_j) | ((imp_k == imp_j) & (kk < jjm))
        rank = jnp.sum(jnp.where(beats, 1.0, 0.0), axis=0, keepdims=True)
        sel = jnp.where((rank < SEL_TOPK) & (jj < n_sel), 1.0, 0.0)
        pos = _dot(jnp.broadcast_to(sel, (8, 256)).astype(bf16), before)[0:1]
        hit = (jnp.broadcast_to(pos, (SEL_TOPK, 256)) == slot_id.astype(f32)) & (jnp.broadcast_to(sel, (SEL_TOPK, 256)) > 0.5)
        ids = jnp.sum(jnp.where(hit, jj16, 0.0), axis=1, keepdims=True)
        idx_ref[0, h * SEL_TOPK:(h + 1) * SEL_TOPK, :] = jnp.broadcast_to(ids, (SEL_TOPK, 128)).astype(jnp.int32)


def _cmp_sel_sample(page_table, new_rows, qraw, pe1, pe2, w1, w2, pool_t, layer, past_len):
    n_batch, n_pages = page_table.shape
    n = n_pages * (PAGE_SIZE // CMP_STRIDE)
    n_sel = past_len // SEL_BLOCK + 1
    cov = _cover_matrix(NC_PAD, 256, n_sel * SEL_BLOCK // CMP_STRIDE - 1, n_sel)
    full = lambda shp: pl.BlockSpec(shp, lambda b, pt: (0,) * len(shp))
    grid_spec = pltpu.PrefetchScalarGridSpec(
        num_scalar_prefetch=1,
        grid=(n_batch,),
        in_specs=[pl.BlockSpec((1, 1, 256), lambda b, pt: (b, 0, 0)),
                  pl.BlockSpec((1, 1, 512), lambda b, pt: (b, 0, 0)), full((1, 4096)), full((1, 4096)),
                  full((4096, 256)), full((4096, 256)), full((NC_PAD, 256)), pl.BlockSpec(memory_space=pl.ANY)],
        out_specs=[pl.BlockSpec((1, 2, 8, 128), lambda b, pt: (b, 0, 0, 0)),
                   pl.BlockSpec((1, 2 * SEL_TOPK, 128), lambda b, pt: (b, 0, 0))],
        scratch_shapes=[pltpu.VMEM((2, n_pages, 2, 2, HEAD_DIM, PAGE_SIZE), f32),
                        pltpu.VMEM((n, CMP_STRIDE * 256), bf16), pltpu.VMEM((n + 16, 256), f32),
                        pltpu.VMEM((NC_PAD, 256), f32), pltpu.VMEM((2, 8, 256), f32),
                        pltpu.SemaphoreType.DMA((2,))],
    )
    return pl.pallas_call(
        functools.partial(_cmp_sel_sample_kernel, layer=layer, n_pages=n_pages, n_batch=n_batch,
                          past_len=past_len),
        grid_spec=grid_spec,
        out_shape=[jax.ShapeDtypeStruct((n_batch, 2, 8, 128), f32),
                   jax.ShapeDtypeStruct((n_batch, 2 * SEL_TOPK, 128), jnp.int32)],
        compiler_params=_cp(("arbitrary",)),
        name="cmp_sel_sample",
    )(page_table, new_rows, qraw, pe1, pe2, w1, w2, cov, pool_t)


def _rows8(row_chunks):
    rid = _row((8, 128))
    out = jnp.zeros((8, 128), f32)
    for r, c in enumerate(row_chunks):
        out = jnp.where(rid == r, jnp.broadcast_to(c, (8, 128)), out)
    return out


def _nsa_q8(qrow, h, half):
    chunks = []
    for g in range(4):
        hd = h * 4 + g
        chunks.append(qrow[:, (hd // 2) * 128:(hd // 2 + 1) * 128])
    q8 = _rows8(chunks)
    sw = pltpu.roll(q8, 64, 1)
    in_place = (_row((8, 128)) % 2) == half
    q8 = jnp.where(in_place, q8, sw)
    return jnp.where((_lane((8, 128)) // 64) == half, q8, 0.0)


def _nsa_sel_sample_kernel(pt_ref, idx_ref, qrot_ref, gate_ref, ocmp_ref, snew_ref, wnew_ref, win_ref,
                           pool_ref, out_ref, nwin_ref, buf_ref, sem_ref, *, layer, n_batch, n_past_blocks):
    b = pl.program_id(0)
    slot = b % 2

    def copy(bb, sl, h, s, kv):
        j = jnp.minimum(idx_ref[bb, h * SEL_TOPK + s], n_past_blocks - 1)
        page = pt_ref[bb, j // (PAGE_SIZE // SEL_BLOCK)]
        return pltpu.make_async_copy(pool_ref.at[layer, page, kv], buf_ref.at[sl, h, kv, s], sem_ref.at[sl])

    def for_all(bb, sl, fn):
        for h in range(NSA_KV_HEADS):
            for s in range(SEL_TOPK):
                for kv in range(2):
                    fn(copy(bb, sl, h, s, kv))

    @pl.when(b == 0)
    def _():
        for_all(0, 0, lambda c: c.start())

    @pl.when(b + 1 < n_batch)
    def _():
        for_all(b + 1, 1 - slot, lambda c: c.start())

    gs = _sigmoid(gate_ref[0])
    lane128 = _lane((1, 128))
    qrot = qrot_ref[0]

    for_all(b, slot, lambda c: c.wait())

    snew = snew_ref[0]
    wnew = wnew_ref[0]
    o_all = []
    for h in range(NSA_KV_HEADS):
        qo = _nsa_q8(qrot, h, h) * SCALE
        qob = qo.astype(bf16)
        halfmask = (lane128 // 64) == h
        scores, valids = [], []
        has_new = False
        for s in range(SEL_TOPK):
            j = idx_ref[b, h * SEL_TOPK + s]
            kt = buf_ref[slot, h, 0, s].reshape(128, 128).astype(bf16)
            valid = ((lane128 // SEL_BLOCK) == (j % (PAGE_SIZE // SEL_BLOCK))) & (j < n_past_blocks)
            scores.append(jnp.where(valid, _dot(qob, kt), NEG_INF))
            valids.append(valid)
            has_new = jnp.logical_or(has_new, j == n_past_blocks)
        s_new = jnp.sum(qo * snew[:, 0:128], axis=-1, keepdims=True)
        s_new = jnp.where(has_new, s_new, NEG_INF)
        smax = scores[0]
        for sc in scores[1:]:
            smax = jnp.maximum(smax, sc)
        mx = jnp.maximum(jnp.max(smax, axis=-1, keepdims=True), s_new)
        e_new = jnp.where(has_new, jnp.exp(s_new - mx), 0.0)
        esum = jnp.zeros((8, 128), f32)
        acc = jnp.zeros((8, 128), f32)
        for s in range(SEL_TOPK):
            e = jnp.where(valids[s], jnp.exp(scores[s] - mx), 0.0)
            esum = esum + e
            vt = buf_ref[slot, h, 1, s].reshape(128, 128).astype(bf16)
            acc = acc + _dot_nt(e.astype(bf16), vt)
        inv = 1.0 / jnp.maximum(jnp.sum(esum, axis=-1, keepdims=True) + e_new, 1e-30)
        o_slc = (acc + e_new * snew[:, 128:256]) * inv
        wt = win_ref[0, 0]
        kt = wt[0].reshape(128, WINDOW).astype(bf16)
        vt = wt[1].reshape(128, WINDOW).astype(bf16)
        maskw = _lane((1, WINDOW)) >= 1
        s_w = jnp.where(maskw, _dot(qob, kt), NEG_INF)
        sw_new = jnp.sum(qo * wnew[:, 0:128], axis=-1, keepdims=True)
        mx = jnp.maximum(jnp.max(s_w, axis=-1, keepdims=True), sw_new)
        e = jnp.where(maskw, jnp.exp(s_w - mx), 0.0)
        e_new = jnp.exp(sw_new - mx)
        inv = 1.0 / jnp.maximum(jnp.sum(e, axis=-1, keepdims=True) + e_new, 1e-30)
        o_win = (_dot_nt(e.astype(bf16), vt) + e_new * wnew[:, 128:256]) * inv
        gate_rows = []
        for c in range(3):
            gate_rows.append(_rows8([jnp.broadcast_to(gs[:, (h * 4 + g) * 3 + c:(h * 4 + g) * 3 + c + 1], (1, 128))
                                     for g in range(4)]))
        o8 = gate_rows[0] * ocmp_ref[0, h] + gate_rows[1] * o_slc + gate_rows[2] * o_win
        o_all.append(jnp.where(halfmask, o8, 0.0))

    lo = lane128 < 64
    for h in range(NSA_KV_HEADS):
        o8 = o_all[h]
        o8s = pltpu.roll(o8, 64, 1)
        low_src, high_src = (o8, o8s) if h == 0 else (o8s, o8)
        for gp in range(2):
            ch = jnp.where(lo, low_src[2 * gp:2 * gp + 1], high_src[2 * gp + 1:2 * gp + 2])
            out_ref[0, :, h * 256 + gp * 128:h * 256 + (gp + 1) * 128] = ch

    last = _lane((1, WINDOW)) == WINDOW - 1
    eye64 = _row((64, 64)) == _lane((64, 64))
    for kv in range(2):
        for h in range(NSA_KV_HEADS):
            c = kv * 2 + h
            newc = jnp.broadcast_to(wnew[:, c * 64:(c + 1) * 64], (64, 64))
            colv = jnp.sum(jnp.where(eye64, newc, 0.0), axis=1, keepdims=True)
            old = win_ref[0, 0, kv, h]
            nwin_ref[0, kv, h] = jnp.where(last, colv, pltpu.roll(old, WINDOW - 1, 1))


def _nsa_sel_sample(page_table, sel_idx, qrot, gates, ocmp, snew, wnew, win_t, pool_t, layer, past_len):
    n_batch = page_table.shape[0]
    row = lambda w: pl.BlockSpec((1, 1, w), lambda b, pt, ix: (b, 0, 0))
    grid_spec = pltpu.PrefetchScalarGridSpec(
        num_scalar_prefetch=2,
        grid=(n_batch,),
        in_specs=[row(512), row(128), pl.BlockSpec((1, 2, 8, 128), lambda b, pt, ix: (b, 0, 0, 0)),
                  row(256), row(256),
                  pl.BlockSpec((1, 1, 2, 2, HEAD_DIM, WINDOW), lambda b, pt, ix: (layer, b, 0, 0, 0, 0)),
                  pl.BlockSpec(memory_space=pl.ANY)],
        out_specs=[row(512), pl.BlockSpec((1, 2, 2, HEAD_DIM, WINDOW), lambda b, pt, ix: (b, 0, 0, 0, 0))],
        scratch_shapes=[pltpu.VMEM((2, NSA_KV_HEADS, 2, SEL_TOPK, 2, HEAD_DIM, PAGE_SIZE), f32),
                        pltpu.SemaphoreType.DMA((2,))],
    )
    return pl.pallas_call(
        functools.partial(_nsa_sel_sample_kernel, layer=layer, n_batch=n_batch,
                          n_past_blocks=past_len // SEL_BLOCK),
        grid_spec=grid_spec,
        out_shape=[jax.ShapeDtypeStruct((n_batch, 1, 512), f32),
                   jax.ShapeDtypeStruct((n_batch, 2, 2, HEAD_DIM, WINDOW), f32)],
        compiler_params=_cp(("arbitrary",)),
        name="nsa_sel_sample",
    )(page_table, sel_idx, qrot, gates, ocmp, snew, wnew, win_t, pool_t)


def _diff_sample_kernel(pt_ref, q_ref, new_ref, dl_ref, pool_ref, out_ref, buf_ref, m_ref, l_ref,
                        acc_ref, sem_ref, *, layer, n_pages, n_batch, n_split, lam_init):
    b = pl.program_id(0)
    hf = pl.program_id(1)
    step = b * n_split + hf
    slot = step % 2
    pps = n_pages // n_split
    rows = pps * PAGE_SIZE

    def copy(bb, hh, sl, p):
        return pltpu.make_async_copy(pool_ref.at[layer, pt_ref[bb, hh * pps + p]],
                                     buf_ref.at[sl, pl.ds(p * 4 * PAGE_SIZE, 4 * PAGE_SIZE), :],
                                     sem_ref.at[sl])

    def fetch(bb, hh, sl):
        for p in range(pps):
            copy(bb, hh, sl, p).start()

    @pl.when(step == 0)
    def _():
        fetch(0, 0, 0)

    @pl.when(step + 1 < n_batch * n_split)
    def _():
        nxt = step + 1
        fetch(nxt // n_split, nxt % n_split, 1 - slot)

    @pl.when(hf == 0)
    def _():
        m_ref[...] = jnp.full(m_ref.shape, NEG_INF, f32)
        l_ref[...] = jnp.zeros(l_ref.shape, f32)
        acc_ref[...] = jnp.zeros(acc_ref.shape, f32)

    for p in range(pps):
        copy(b, hf, slot, p).wait()

    qrow = q_ref[0]
    lane = _lane((8, 128))
    rid = _row((8, 128))
    q8s = []
    for h in range(2):
        q8 = _rows8([qrow[:, h * 256 + (r // 2) * 128:h * 256 + (r // 2 + 1) * 128] for r in range(4)])
        q8s.append(jnp.where((lane // 64) == (rid % 2), q8, 0.0) * SCALE)
    for h in range(2):
        k = buf_ref[slot, pl.ds(h, rows, stride=4), :].astype(bf16)
        v = buf_ref[slot, pl.ds(2 + h, rows, stride=4), :].astype(bf16)
        s = _dot_nt(q8s[h].astype(bf16), k)
        m_old = m_ref[h]
        m_new = jnp.maximum(m_old, jnp.max(s, axis=-1, keepdims=True))
        alpha = jnp.exp(m_old - m_new)
        p_ = jnp.exp(s - m_new)
        l_ref[h] = alpha * l_ref[h] + jnp.sum(p_, axis=-1, keepdims=True)
        acc_ref[h] = alpha * acc_ref[h] + _dot(p_.astype(bf16), v)
        m_ref[h] = m_new

    @pl.when(hf == n_split - 1)
    def _():
        lam = _diff_lambda(dl_ref[...], lam_init)
        new = new_ref[0]
        for h in range(2):
            s_new = jnp.sum(q8s[h] * new[:, h * 128:(h + 1) * 128], axis=-1, keepdims=True)
            m_old = m_ref[h]
            m_new = jnp.maximum(m_old, s_new)
            alpha = jnp.exp(m_old - m_new)
            p_new = jnp.exp(s_new - m_new)
            l_ = alpha * l_ref[h] + p_new
            acc = alpha * acc_ref[h] + p_new * new[:, 256 + h * 128:256 + (h + 1) * 128]
            o = acc * (1.0 / jnp.maximum(l_, 1e-30))
            for g in range(2):
                og = o[2 * g:2 * g + 1] - lam * o[2 * g + 1:2 * g + 2]
                out_ref[0, :, (h * 2 + g) * 128:(h * 2 + g + 1) * 128] = _rms_unit(og) * (1.0 - lam_init)


def _diff_sample(page_table, dqrot, new_rows, dl, pool, layer, lam_init):
    n_batch, n_pages = page_table.shape
    n_split = 2
    pps = n_pages // n_split
    row = lambda w: pl.BlockSpec((1, 1, w), lambda b, s, pt: (b, 0, 0))
    grid_spec = pltpu.PrefetchScalarGridSpec(
        num_scalar_prefetch=1,
        grid=(n_batch, n_split),
        in_specs=[row(512), row(512), pl.BlockSpec((4, 64), lambda b, s, pt: (0, 0)),
                  pl.BlockSpec(memory_space=pl.ANY)],
        out_specs=row(512),
        scratch_shapes=[pltpu.VMEM((2, pps * 4 * PAGE_SIZE, 128), f32), pltpu.VMEM((2, 8, 1), f32),
                        pltpu.VMEM((2, 8, 1), f32), pltpu.VMEM((2, 8, 128), f32),
                        pltpu.SemaphoreType.DMA((2,))],
    )
    return pl.pallas_call(
        functools.partial(_diff_sample_kernel, layer=layer, n_pages=n_pages, n_batch=n_batch,
                          n_split=n_split, lam_init=lam_init),
        grid_spec=grid_spec,
        out_shape=jax.ShapeDtypeStruct((n_batch, 1, 512), f32),
        compiler_params=_cp(("arbitrary", "arbitrary")),
        name="diff_sample",
    )(page_table, dqrot, new_rows, dl, pool)


def _prep_w_in(w):
    parts = jnp.split(w, [sum(IN_SPLITS[:i + 1]) for i in range(len(IN_SPLITS) - 1)], axis=-1)
    nq, ncmp, nslc, nwin, ngate, scb, scc, sch, dq, dk, dv, s5u = parts
    gate = jnp.pad(ngate, ((0, 0), (0, HC - C_GATE - ngate.shape[1])))
    wcat = jnp.concatenate([nq, scb, scc, sch, dq, s5u, ncmp, nslc, nwin, dk, dv, gate], axis=-1).astype(bf16)
    return jnp.transpose(wcat.reshape(D_MODEL, HC // IN_TN, IN_TN), (1, 0, 2))


def _prep_phi(pe, w):
    w2 = w.reshape(2, 2, CMP_STRIDE, HEAD_DIM, HEAD_DIM)
    wc = jnp.repeat(w2, 2, axis=0)
    eye = jnp.eye(4, dtype=f32)
    ws, pes = [], []
    for half in range(2):
        ws.append(jnp.einsum('csde,cf->scdfe', wc[:, half], eye).reshape(4096, 256).astype(bf16))
        pc = jnp.repeat(pe[:, half * CMP_STRIDE:(half + 1) * CMP_STRIDE], 2, axis=0)
        pes.append(jnp.transpose(pc, (1, 0, 2)).reshape(1, 4096))
    return pes[0], pes[1], ws[0], ws[1]


def _prep_s5(bbr, bbi, c_re, c_im):
    eye = jnp.eye(8, dtype=f32)

    def wb_of(bb):
        x = bb.reshape(S5_CH, S5_SLABS, 8, S5_STATE)
        return jnp.einsum('csgp,hg->shcgp', x, eye).reshape(S5_SLABS, 128, 512)

    wb = jnp.concatenate([wb_of(bbr), wb_of(bbi)], axis=-1).astype(bf16)

    def wc_of(c):
        x = c.reshape(S5_SLABS, 8, S5_CH, S5_STATE)
        return jnp.einsum('sgcp,hg->shpgc', x, eye).reshape(S5_SLABS, 512, 128)

    wc = jnp.concatenate([wc_of(c_re), -wc_of(c_im)], axis=1).astype(bf16)
    return wb, wc


def _rope_tables(pos):
    half = HEAD_DIM // 2
    inv = ROPE_THETA ** (-jnp.arange(half, dtype=f32) / half)
    ang = pos.astype(f32)[:, None] * inv[None, :]
    c, s = jnp.cos(ang), jnp.sin(ang)
    return jnp.tile(c, (1, 4)), jnp.tile(jnp.concatenate([-s, s], axis=1), (1, 2))


def kernel(x_prompt, x_sample, cache_nsa_cmp, cache_nsa_slc, cache_diff, state_nsa_win, state_sconv, state_s5_re, state_s5_im, state_ffn_conv, page_table, w_in, nsa_phi_pe, nsa_phi_w, sc_conv_w, diff_lambda, s5_a_re, s5_a_im, s5_log_dt, s5_b_re, s5_b_im, s5_c_re, s5_c_im, s5_d, s5_glu_w, s5_glu_b, mix_gain, w_out, ln1_g, ln1_b, ffn_w_up, ffn_conv_w, ffn_w_down, ln2_g, ln2_b):
    bp, t_len, _ = x_prompt.shape
    bs = x_sample.shape[0]
    n_pool = cache_nsa_cmp.shape[1]
    n_pages = page_table.shape[1]
    past_len = n_pages * PAGE_SIZE
    n_state = S5_GROUPS * S5_STATE
    mp = bp * t_len

    cos_p, sin_p = _rope_tables(jnp.arange(t_len))
    cos_s, sin_s = _rope_tables(jnp.full((bs,), past_len))

    pool_cmp = jnp.transpose(cache_nsa_cmp, (0, 1, 3, 4, 5, 2))
    pool_slc = jnp.transpose(cache_nsa_slc, (0, 1, 3, 4, 5, 2))
    pool_diff = cache_diff.reshape(DEPTH, n_pool, PAGE_SIZE * 4, 128)
    win_t = jnp.transpose(state_nsa_win, (0, 1, 3, 4, 5, 2))

    xp = x_prompt.reshape(mp, D_MODEL)
    xs = x_sample.reshape(bs, D_MODEL)
    xp_b, xs_b = xp.astype(bf16), xs.astype(bf16)

    outs_p = {k: [] for k in ('cmp', 'slc', 'win', 'diff', 'sc', 's5r', 's5i', 'ffn')}
    outs_s = {k: [] for k in ('cmp', 'slc', 'win', 'diff', 'sc', 's5r', 's5i', 'ffn')}

    gw = s5_glu_w.astype(bf16)
    wo = w_out.astype(bf16)
    wup = ffn_w_up.astype(bf16)
    wdn = ffn_w_down.astype(bf16)

    for l in range(DEPTH):
        lam_init = 0.8 - 0.6 * math.exp(-0.3 * l)
        w_in_l = _prep_w_in(w_in[l])
        pe1, pe2, w1, w2 = _prep_phi(nsa_phi_pe[l], nsa_phi_w[l])
        tabs, bbr, bbi = _s5_prep(s5_a_re[l].reshape(1, n_state), s5_a_im[l].reshape(1, n_state),
                                  jnp.repeat(s5_log_dt[l], S5_STATE).reshape(1, n_state),
                                  jnp.transpose(s5_b_re[l], (2, 0, 1)).reshape(S5_CH, n_state),
                                  jnp.transpose(s5_b_im[l], (2, 0, 1)).reshape(S5_CH, n_state))
        wb5, wc5 = _prep_s5(bbr, bbi, s5_c_re[l], s5_c_im[l])
        d5 = s5_d[l].reshape(1, 512)
        gb = s5_glu_b[l].reshape(1, 512)
        gain = mix_gain[l].reshape(1, D_MODEL)
        g1, b1 = ln1_g[l].reshape(1, D_MODEL), ln1_b[l].reshape(1, D_MODEL)
        g2, b2 = ln2_g[l].reshape(1, D_MODEL), ln2_b[l].reshape(1, D_MODEL)
        cwf = ffn_conv_w[l]
        scw = sc_conv_w[l]
        dl = diff_lambda[l]

        hcat = _in_proj(xp_b, w_in_l, 1024)
        qrot, dqrot, kvslc, kvwin, kvdiff = _rope(hcat, cos_p, sin_p, 512)
        kvcmp = hcat[:, C_CMP:C_CMP + 256]
        kvc = _cmp_prompt(kvcmp.reshape(mp // CMP_STRIDE, CMP_STRIDE * 256), pe1, pe2, w1, w2, bp)
        nsa = _nsa_prompt(hcat, qrot, kvc, kvslc, kvwin, bp, t_len)
        dif = _diff_prompt(dqrot, kvdiff, dl, bp, t_len, lam_init)
        sc, sc_tail = _sconv_prompt(hcat, scw, bp, t_len)
        s5y, s5r, s5i = _s5_scan(hcat, wb5, wc5, d5, tabs, bp, t_len)
        x1, x1b = _mix(nsa, sc, dif, s5y, gw, gb, gain, wo, xp, g1, b1, 256, l)
        xp, xp_b, ffn_tail = _ffn_prompt(x1b, x1, wup, wdn, cwf, g2, b2, t_len, l)

        outs_p['cmp'].append(kvcmp.reshape(bp, t_len, 2, 2, HEAD_DIM))
        outs_p['slc'].append(kvslc.reshape(bp, t_len, 2, 2, HEAD_DIM))
        outs_p['win'].append(kvwin.reshape(bp, t_len, 2, 2, HEAD_DIM)[:, t_len - WINDOW:])
        outs_p['diff'].append(kvdiff.reshape(bp, t_len, 2, 2, 2 * HEAD_DIM))
        outs_p['sc'].append(sc_tail[:, 6:8])
        outs_p['s5r'].append(s5r.reshape(bp, S5_GROUPS, S5_STATE))
        outs_p['s5i'].append(s5i.reshape(bp, S5_GROUPS, S5_STATE))
        tiles_per_seq = ffn_tail.shape[0] // bp
        outs_p['ffn'].append(ffn_tail.reshape(bp, tiles_per_seq, 8, D_FF)[:, -1, 6:8])

        hs = _in_proj(xs_b, w_in_l, bs)
        qrot_s, dqrot_s, kvslc_s, kvwin_s, kvdiff_s = _rope(hs, cos_s, sin_s, bs)
        kvcmp_s = hs[:, C_CMP:C_CMP + 256]
        ocmp_s, sel_s = _cmp_sel_sample(page_table, kvcmp_s.reshape(bs, 1, 256),
                                        hs[:, C_NQ:C_NQ + 512].reshape(bs, 1, 512), pe1, pe2, w1, w2,
                                        pool_cmp, l, past_len)
        nsa_s, nwin_t = _nsa_sel_sample(page_table, sel_s[:, :, 0], qrot_s.reshape(bs, 1, 512),
                                        hs[:, C_GATE:C_GATE + 128].reshape(bs, 1, 128), ocmp_s,
                                        kvslc_s.reshape(bs, 1, 256), kvwin_s.reshape(bs, 1, 256),
                                        win_t, pool_slc, l, past_len)
        dif_s = _diff_sample(page_table, dqrot_s.reshape(bs, 1, 512), kvdiff_s.reshape(bs, 1, 512),
                             dl, pool_diff, l, lam_init)
        scp = jnp.transpose(state_sconv[l], (1, 0, 2))
        sc_s, z_s, s5y_s, s5r_s, s5i_s = _sample_small(
            hs[:, C_SCB:C_SCB + 512], hs[:, C_SCC:C_SCC + 512], hs[:, C_SCH:C_SCH + 512], scw, scp,
            hs[:, C_S5U:C_S5U + 512], wb5, wc5, d5, tabs,
            state_s5_re[l].reshape(bs, n_state), state_s5_im[l].reshape(bs, n_state))
        x1s, x1sb = _mix(nsa_s.reshape(bs, 512), sc_s, dif_s.reshape(bs, 512), s5y_s, gw, gb, gain, wo,
                         xs, g1, b1, bs, l)
        prev_ffn = state_ffn_conv[l]
        xs, xs_b, aup_s = _ffn_sample(x1sb, x1s, prev_ffn[:, 0], prev_ffn[:, 1], wup, wdn, cwf, g2, b2, l)

        outs_s['cmp'].append(kvcmp_s.reshape(bs, 1, 2, 2, HEAD_DIM))
        outs_s['slc'].append(kvslc_s.reshape(bs, 1, 2, 2, HEAD_DIM))
        outs_s['win'].append(jnp.transpose(nwin_t, (0, 4, 1, 2, 3)))
        outs_s['diff'].append(kvdiff_s.reshape(bs, 1, 2, 2, 2 * HEAD_DIM))
        outs_s['sc'].append(jnp.stack([state_sconv[l][:, 1], z_s], axis=1))
        outs_s['s5r'].append(s5r_s.reshape(bs, S5_GROUPS, S5_STATE))
        outs_s['s5i'].append(s5i_s.reshape(bs, S5_GROUPS, S5_STATE))
        outs_s['ffn'].append(jnp.stack([prev_ffn[:, 1], aup_s], axis=1))

    order = ('cmp', 'slc', 'win', 'diff', 'sc', 's5r', 's5i', 'ffn')
    res = [xp.reshape(bp, t_len, D_MODEL), xs.reshape(bs, 1, D_MODEL)]
    res += [jnp.stack(outs_p[k], axis=0) for k in order]
    res += [jnp.stack(outs_s[k], axis=0) for k in order]
    return tuple(res)
```

```python
import functools
import math

import jax
import jax.numpy as jnp
from jax import lax
from jax.experimental import pallas as pl
from jax.experimental.pallas import tpu as pltpu

f32 = jnp.float32
bf16 = jnp.bfloat16

D_MODEL = 2048
DEPTH = 2
PAGE_SIZE = 128
HEAD_DIM = 64
GROUP_WIDTH = D_MODEL // 4
NSA_KV_HEADS = 2
NSA_GROUP = 4
CMP_STRIDE = 16
CMP_LEN = 32
SEL_BLOCK = 64
SEL_TOPK = 16
WINDOW = 512
FORCE_BONUS = 1e4
CONV_W = 3
S5_CH = 16
S5_GROUPS = 32
S5_STATE = 64
D_FF = 5632
ROPE_THETA = 10000.0
QBLOCK = 128
LN_EPS = 1e-5
RMS_EPS = 1e-6
NEG_INF = -1e30
DN_ALPHA = (2 * DEPTH) ** 0.25
SCALE = HEAD_DIM ** -0.5
LOG2E = math.log2(math.e)

IN_SPLITS = (512, 256, 256, 256, 24, 512, 512, 512, 512, 256, 256, 512)
C_NQ, C_SCB, C_SCC, C_SCH, C_DQ, C_S5U = 0, 512, 1024, 1536, 2048, 2560
C_CMP, C_SLC, C_WIN, C_DK, C_DV, C_GATE = 3072, 3328, 3584, 3840, 4096, 4352
HC = 4608
IN_TN = 768
FFN_TF = 512

VMEM_CAP_V7X = 64 * 1024 * 1024
VMEM_LIMIT = 56 * 1024 * 1024
NC_PAD = 640
S5_SLABS = 4
NSA_CLASS_BLOCKS = 4
DIFF_CLASS_BLOCKS = 2
FFN_CHUNK = 256


def _cp(sem):
    return pltpu.CompilerParams(dimension_semantics=sem, vmem_limit_bytes=VMEM_LIMIT)


def _dot(a, b):
    return jnp.dot(a, b, preferred_element_type=f32)


def _dot_nt(a, b):
    return lax.dot_general(a, b, (((1,), (1,)), ((), ())), preferred_element_type=f32)


def _lane(shape):
    return lax.broadcasted_iota(jnp.int32, shape, len(shape) - 1)


def _row(shape):
    return lax.broadcasted_iota(jnp.int32, shape, len(shape) - 2)


def _msoftmax(s, mask):
    s = jnp.where(mask, s, NEG_INF)
    m = jnp.max(s, axis=-1, keepdims=True)
    e = jnp.where(mask, jnp.exp(s - m), 0.0)
    return e * (1.0 / jnp.maximum(jnp.sum(e, axis=-1, keepdims=True), 1e-30))


def _exp2_softmax(s, bias):
    s = s + bias[None]
    e = jnp.exp2(s - jnp.max(s, axis=-1, keepdims=True))
    return e, 1.0 / jnp.maximum(jnp.sum(e, axis=-1, keepdims=True), 1e-30)


def _sigmoid(x):
    return 1.0 / (1.0 + jnp.exp(-x))


def _rms_unit(x):
    return x * lax.rsqrt(jnp.mean(x * x, axis=-1, keepdims=True) + RMS_EPS)


def _layer_norm(z, g, b):
    mu = jnp.mean(z, axis=-1, keepdims=True)
    d = z - mu
    var = jnp.mean(d * d, axis=-1, keepdims=True)
    return d * lax.rsqrt(var + LN_EPS) * g + b


def _split_hi_lo(x):
    hi = x.astype(bf16)
    lo = (x - hi.astype(f32)).astype(bf16)
    return hi, lo


def _matmul_kernel(x_ref, w_ref, o_ref):
    o_ref[...] = _dot(x_ref[...], w_ref[...])


def _in_proj(xb, w, tm):
    m = xb.shape[0]
    tn = IN_TN
    return pl.pallas_call(
        _matmul_kernel,
        grid=(m // tm, HC // tn),
        in_specs=[pl.BlockSpec((tm, D_MODEL), lambda i, j: (i, 0)),
                  pl.BlockSpec((None, D_MODEL, tn), lambda i, j: (j, 0, 0))],
        out_specs=pl.BlockSpec((tm, tn), lambda i, j: (i, j)),
        out_shape=jax.ShapeDtypeStruct((m, HC), f32),
        compiler_params=_cp(("parallel", "arbitrary")),
        name="in_proj",
    )(xb, w)


def _rope_cols(x, cos, sin):
    outs = []
    first = (_lane((1, 128)) % 64) < 32
    for c in range(x.shape[1] // 128):
        xc = x[:, c * 128:(c + 1) * 128]
        sw = jnp.where(first, pltpu.roll(xc, 96, 1), pltpu.roll(xc, 32, 1))
        outs.append(xc * cos + sw * sin)
    return outs


def _rope_kernel(nq_ref, dq_ref, slc_ref, win_ref, dk_ref, dv_ref, cos_ref, sin_ref,
                 qrot_ref, dqrot_ref, kvslc_ref, kvwin_ref, kvdiff_ref):
    cos = cos_ref[...]
    sin = sin_ref[...]
    for c, v in enumerate(_rope_cols(nq_ref[...], cos, sin)):
        qrot_ref[:, c * 128:(c + 1) * 128] = v
    for c, v in enumerate(_rope_cols(dq_ref[...], cos, sin)):
        dqrot_ref[:, c * 128:(c + 1) * 128] = v
    kvslc_ref[:, 0:128] = _rope_cols(slc_ref[:, 0:128], cos, sin)[0]
    kvslc_ref[:, 128:256] = slc_ref[:, 128:256]
    kvwin_ref[:, 0:128] = _rope_cols(win_ref[:, 0:128], cos, sin)[0]
    kvwin_ref[:, 128:256] = win_ref[:, 128:256]
    for c, v in enumerate(_rope_cols(dk_ref[...], cos, sin)):
        kvdiff_ref[:, c * 128:(c + 1) * 128] = v
    kvdiff_ref[:, 256:512] = dv_ref[...]


def _rope(hcat, cos, sin, tr):
    m = hcat.shape[0]
    nt = cos.shape[0] // tr

    def col(w, off):
        return pl.BlockSpec((tr, w), lambda i: (i, off // w))

    tab = pl.BlockSpec((tr, 128), lambda i: (i % nt, 0))
    return pl.pallas_call(
        _rope_kernel,
        grid=(m // tr,),
        in_specs=[col(512, C_NQ), col(512, C_DQ), col(256, C_SLC), col(256, C_WIN),
                  col(256, C_DK), col(256, C_DV), tab, tab],
        out_specs=[pl.BlockSpec((tr, 512), lambda i: (i, 0)),
                   pl.BlockSpec((tr, 512), lambda i: (i, 0)),
                   pl.BlockSpec((tr, 256), lambda i: (i, 0)),
                   pl.BlockSpec((tr, 256), lambda i: (i, 0)),
                   pl.BlockSpec((tr, 512), lambda i: (i, 0))],
        out_shape=[jax.ShapeDtypeStruct((m, 512), f32), jax.ShapeDtypeStruct((m, 512), f32),
                   jax.ShapeDtypeStruct((m, 256), f32), jax.ShapeDtypeStruct((m, 256), f32),
                   jax.ShapeDtypeStruct((m, 512), f32)],
        compiler_params=_cp(("parallel",)),
        name="rope",
    )(hcat, hcat, hcat, hcat, hcat, hcat, cos, sin)


def _cmp_prompt_kernel(z_ref, pe1_ref, pe2_ref, w1_ref, w2_ref, o_ref, sb_ref):
    z = z_ref[...]
    a = _dot((z + pe1_ref[...]).astype(bf16), w1_ref[...])
    bm = _dot((z + pe2_ref[...]).astype(bf16), w2_ref[...])
    n = z.shape[0]
    sb_ref[0:n, :] = bm
    sb_ref[n:n + 8, :] = jnp.zeros((8, 256), f32)
    o_ref[0] = a + sb_ref[1:n + 1, :]


def _cmp_prompt(z, pe1, pe2, w1, w2, bsz):
    n = z.shape[0] // bsz
    full = lambda shp: pl.BlockSpec(shp, lambda b: (0, 0))
    return pl.pallas_call(
        _cmp_prompt_kernel,
        grid=(bsz,),
        in_specs=[pl.BlockSpec((n, 4096), lambda b: (b, 0)), full((1, 4096)), full((1, 4096)),
                  full((4096, 256)), full((4096, 256))],
        out_specs=pl.BlockSpec((1, n, 256), lambda b: (b, 0, 0)),
        out_shape=jax.ShapeDtypeStruct((bsz, n, 256), f32),
        scratch_shapes=[pltpu.VMEM((n + 8, 256), f32)],
        compiler_params=_cp(("parallel",)),
        name="cmp_prompt",
    )(z, pe1, pe2, w1, w2)


def _nsa_qstack(blk, h):
    halfmask = (_lane((1, 128)) // 64) == h
    parts = []
    for g in range(4):
        c = blk[:, (g // 2) * 128:(g // 2 + 1) * 128]
        if g % 2 != h:
            c = pltpu.roll(c, 64, 1)
        parts.append(jnp.where(halfmask, c, 0.0))
    return jnp.concatenate(parts, axis=0)


def _nsa_assemble(o_list, h):
    lo = _lane((1, 128)) < 64
    chunks = []
    for gp in range(2):
        a, b = o_list[2 * gp], o_list[2 * gp + 1]
        if h == 0:
            b = pltpu.roll(b, 64, 1)
        else:
            a = pltpu.roll(a, 64, 1)
        chunks.append(jnp.where(lo, a, b))
    return chunks


def _nsa_prompt_body(qraw_ref, qrot_ref, gate_ref, kvc_ref, slc_ref, win_ref, cov_ref, exp_ref, out_ref,
                     *, s0, kmax, t_len):
    qb = QBLOCK
    qpos = s0 + _row((qb, 1))
    gs = _sigmoid(gate_ref[...])
    n_i = _lane((1, 128))
    maskc = ((16 * n_i + 31) <= qpos) & (n_i < 127)
    cov = cov_ref[...]
    expand = exp_ref[:, 0:kmax]
    causal = _lane((1, kmax)) <= qpos
    wlen = WINDOW + qb
    start = pl.multiple_of(jnp.clip(s0 - WINDOW, 0, t_len - wlen), 128)
    wpos = start + _lane((1, wlen))
    bias_w = jnp.where((wpos <= qpos) & ((qpos - wpos) < WINDOW), 0.0, NEG_INF)
    jj = _lane((1, 128))
    cur = qpos // SEL_BLOCK
    n_sel = kmax // SEL_BLOCK
    n_blk = t_len // SEL_BLOCK
    j_t = _row((n_blk, qb))
    forced = (jj == 0) | (jj == cur) | (jj == cur - 1)

    for h in range(NSA_KV_HEADS):
        kc = kvc_ref[0, :, 0:128].astype(bf16)
        vc = kvc_ref[0, :, 128:256].astype(bf16)
        qr = (_nsa_qstack(qraw_ref[:, h * 256:(h + 1) * 256], h) * SCALE).astype(bf16)
        qo = (_nsa_qstack(qrot_ref[:, h * 256:(h + 1) * 256], h) * (SCALE * LOG2E)).astype(bf16)
        s_c = _dot_nt(qr, kc).reshape(4, qb, 128)
        p_c = _msoftmax(s_c, maskc[None])
        o_cmp = _dot(p_c.reshape(4 * qb, 128).astype(bf16), vc)
        psum = p_c[0] + p_c[1] + p_c[2] + p_c[3]
        p_hi, p_lo = _split_hi_lo(psum)
        imp = _dot(p_hi, cov) + _dot(p_lo, cov)
        imp = jnp.where(forced, imp + FORCE_BONUS, imp)
        imp = jnp.where(jj <= cur, imp, -FORCE_BONUS)
        imp = jnp.where(jj < n_sel, imp, -3e38)
        imp_t = imp.T[0:n_blk]
        rank = jnp.zeros((n_blk, qb), f32)
        for k in range(n_sel):
            rk = imp_t[k:k + 1, :]
            beats = (rk > imp_t) | ((rk == imp_t) & (j_t > k))
            rank = rank + jnp.where(beats, 1.0, 0.0)
        sel_t = jnp.where((rank < SEL_TOPK) & (j_t < n_sel), 1.0, 0.0)
        sel = jnp.concatenate([sel_t, jnp.zeros((128 - n_blk, qb), f32)], axis=0).T.astype(bf16)
        bias_s = jnp.where((_dot(sel, expand) > 0.5) & causal, 0.0, NEG_INF)
        ks = slc_ref[0:kmax, 0:128].astype(bf16)
        vs = slc_ref[0:kmax, 128:256].astype(bf16)
        e_s, inv_s = _exp2_softmax(_dot_nt(qo, ks).reshape(4, qb, kmax), bias_s)
        o_slc = _dot(e_s.reshape(4 * qb, kmax).astype(bf16), vs) * inv_s.reshape(4 * qb, 1)
        kw = win_ref[pl.ds(start, wlen), 0:128].astype(bf16)
        vw = win_ref[pl.ds(start, wlen), 128:256].astype(bf16)
        e_w, inv_w = _exp2_softmax(_dot_nt(qo, kw).reshape(4, qb, wlen), bias_w)
        o_win = _dot(e_w.reshape(4 * qb, wlen).astype(bf16), vw) * inv_w.reshape(4 * qb, 1)
        o_list = []
        for g in range(NSA_GROUP):
            gi = (h * NSA_GROUP + g) * 3
            r = slice(g * qb, (g + 1) * qb)
            o_list.append(gs[:, gi:gi + 1] * o_cmp[r] + gs[:, gi + 1:gi + 2] * o_slc[r]
                          + gs[:, gi + 2:gi + 3] * o_win[r])
        for gp, ch in enumerate(_nsa_assemble(o_list, h)):
            out_ref[:, h * 256 + gp * 128:h * 256 + (gp + 1) * 128] = ch


def _by_key_class(body, t_len, blocks):
    i = pl.program_id(1)
    span = blocks * QBLOCK
    for c in range(t_len // span):
        @pl.when(i // blocks == c)
        def _(c=c):
            body(s0=i * QBLOCK, kmax=(c + 1) * span)


def _nsa_prompt_kernel(*refs, t_len):
    _by_key_class(functools.partial(_nsa_prompt_body, *refs, t_len=t_len), t_len, NSA_CLASS_BLOCKS)


def _cover_matrix(n_rows, n_cols, n_cmp, n_sel):
    n = jnp.arange(n_rows)[:, None]
    j = jnp.arange(n_cols)[None, :]
    cov = jnp.clip(jnp.minimum(CMP_STRIDE * n + CMP_LEN, SEL_BLOCK * (j + 1)) - jnp.maximum(CMP_STRIDE * n, SEL_BLOCK * j),
                   0, CMP_LEN)
    cov = jnp.where((n < n_cmp) & (j < n_sel), cov, 0).astype(f32) / CMP_LEN
    return cov.astype(bf16)


def _expand_matrix(n_rows, n_keys):
    return (jnp.arange(n_keys)[None, :] // SEL_BLOCK == jnp.arange(n_rows)[:, None]).astype(bf16)


def _nsa_prompt(hcat, qrot, kvc, kvslc, kvwin, bsz, t_len):
    m = hcat.shape[0]
    nqb = t_len // QBLOCK
    cov = _cover_matrix(128, 128, t_len // CMP_STRIDE - 1, t_len // SEL_BLOCK)
    expand = _expand_matrix(128, t_len)
    return pl.pallas_call(
        functools.partial(_nsa_prompt_kernel, t_len=t_len),
        grid=(bsz, nqb),
        in_specs=[pl.BlockSpec((QBLOCK, 512), lambda b, i: (b * nqb + i, 0)),
                  pl.BlockSpec((QBLOCK, 512), lambda b, i: (b * nqb + i, 0)),
                  pl.BlockSpec((QBLOCK, 128), lambda b, i: (b * nqb + i, C_GATE // 128)),
                  pl.BlockSpec((1, 128, 256), lambda b, i: (b, 0, 0)),
                  pl.BlockSpec((t_len, 256), lambda b, i: (b, 0)),
                  pl.BlockSpec((t_len, 256), lambda b, i: (b, 0)),
                  pl.BlockSpec((128, 128), lambda b, i: (0, 0)),
                  pl.BlockSpec((128, t_len), lambda b, i: (0, 0))],
        out_specs=pl.BlockSpec((QBLOCK, 512), lambda b, i: (b * nqb + i, 0)),
        out_shape=jax.ShapeDtypeStruct((m, 512), f32),
        compiler_params=_cp(("parallel", "arbitrary")),
        name="nsa_prompt",
    )(hcat, qrot, hcat, kvc, kvslc, kvwin, cov, expand)


def _diff_lambda(dl, lam_init):
    a = jnp.sum(dl[0:1, :] * dl[1:2, :], axis=-1, keepdims=True)
    b = jnp.sum(dl[2:3, :] * dl[3:4, :], axis=-1, keepdims=True)
    return jnp.exp(a) - jnp.exp(b) + lam_init


def _diff_prompt_body(dq_ref, kv_ref, dl_ref, out_ref, *, s0, kmax, lam_init):
    qb = QBLOCK
    qpos = s0 + _row((qb, 1))
    bias = jnp.where(_lane((1, kmax)) <= qpos, 0.0, NEG_INF)
    lam = _diff_lambda(dl_ref[...], lam_init)
    lane = _lane((1, 128))
    for h in range(2):
        k = kv_ref[0:kmax, h * 128:(h + 1) * 128].astype(bf16)
        v = kv_ref[0:kmax, 256 + h * 128:256 + (h + 1) * 128].astype(bf16)
        parts = []
        for g in range(2):
            c = dq_ref[:, h * 256 + g * 128:h * 256 + (g + 1) * 128] * (SCALE * LOG2E)
            for i in range(2):
                parts.append(jnp.where((lane // 64) == i, c, 0.0))
        q = jnp.concatenate(parts, axis=0).astype(bf16)
        e, inv = _exp2_softmax(_dot_nt(q, k).reshape(4, qb, kmax), bias)
        o = _dot(e.reshape(4 * qb, kmax).astype(bf16), v) * inv.reshape(4 * qb, 1)
        for g in range(2):
            og = o[2 * g * qb:(2 * g + 1) * qb] - lam * o[(2 * g + 1) * qb:(2 * g + 2) * qb]
            out_ref[:, (h * 2 + g) * 128:(h * 2 + g + 1) * 128] = _rms_unit(og) * (1.0 - lam_init)


def _diff_prompt_kernel(*refs, t_len, lam_init):
    _by_key_class(functools.partial(_diff_prompt_body, *refs, lam_init=lam_init), t_len, DIFF_CLASS_BLOCKS)


def _diff_prompt(dqrot, kvdiff, dl, bsz, t_len, lam_init):
    m = dqrot.shape[0]
    nqb = t_len // QBLOCK
    return pl.pallas_call(
        functools.partial(_diff_prompt_kernel, t_len=t_len, lam_init=lam_init),
        grid=(bsz, nqb),
        in_specs=[pl.BlockSpec((QBLOCK, 512), lambda b, i: (b * nqb + i, 0)),
                  pl.BlockSpec((t_len, 512), lambda b, i: (b, 0)),
                  pl.BlockSpec((4, 64), lambda b, i: (0, 0))],
        out_specs=pl.BlockSpec((QBLOCK, 512), lambda b, i: (b * nqb + i, 0)),
        out_shape=jax.ShapeDtypeStruct((m, 512), f32),
        compiler_params=_cp(("parallel", "arbitrary")),
        name="diff_prompt",
    )(dqrot, kvdiff, dl)


def _sconv_prompt_kernel(b_ref, c_ref, h_ref, w_ref, out_ref, tail_ref, buf_ref, *, tr):
    t = pl.program_id(1)

    @pl.when(t == 0)
    def _():
        buf_ref[0:8, :] = jnp.zeros((8, 512), f32)

    z = c_ref[...] * h_ref[...]
    buf_ref[8:8 + tr, :] = z
    w = w_ref[...]
    y = w[0:1] * buf_ref[6:6 + tr, :] + w[1:2] * buf_ref[7:7 + tr, :] + w[2:3] * z
    out_ref[...] = _rms_unit(b_ref[...] * y)
    tail = z[tr - 8:tr]
    tail_ref[0] = tail
    buf_ref[0:8, :] = tail


def _sconv_prompt(hcat, w, bsz, t_len):
    tr = 512
    nt = t_len // tr
    m = hcat.shape[0]

    def col(off):
        return pl.BlockSpec((tr, 512), lambda b, t: (b * nt + t, off // 512))

    return pl.pallas_call(
        functools.partial(_sconv_prompt_kernel, tr=tr),
        grid=(bsz, nt),
        in_specs=[col(C_SCB), col(C_SCC), col(C_SCH), pl.BlockSpec((3, 512), lambda b, t: (0, 0))],
        out_specs=[pl.BlockSpec((tr, 512), lambda b, t: (b * nt + t, 0)),
                   pl.BlockSpec((1, 8, 512), lambda b, t: (b, 0, 0))],
        out_shape=[jax.ShapeDtypeStruct((m, 512), f32), jax.ShapeDtypeStruct((bsz, 8, 512), f32)],
        scratch_shapes=[pltpu.VMEM((8 + tr, 512), f32)],
        compiler_params=_cp(("parallel", "arbitrary")),
        name="sconv_prompt",
    )(hcat, hcat, hcat, w)


def _cmul(ar, ai, br, bi):
    return ar * br - ai * bi, ar * bi + ai * br


def _s5_prep_kernel(ar_ref, ai_ref, ldt_ref, br_ref, bi_ref, tab_ref, bbr_ref, bbi_ref):
    ar, ai = ar_ref[...], ai_ref[...]
    dt = jnp.exp(ldt_ref[...])
    mag = jnp.exp(ar * dt)
    abr, abi = mag * jnp.cos(ai * dt), mag * jnp.sin(ai * dt)
    den = ar * ar + ai * ai
    nr, ni = abr - 1.0, abi
    cre = (nr * ar + ni * ai) / den
    cim = (ni * ar - nr * ai) / den
    br, bi = br_ref[...], bi_ref[...]
    bbr_ref[...] = cre * br - cim * bi
    bbi_ref[...] = cre * bi + cim * br
    pw = [(abr, abi)]
    for _ in range(7):
        pw.append(_cmul(pw[-1][0], pw[-1][1], abr, abi))
    n = ar.shape[1]
    row = _row((8, n))
    zero = jnp.zeros((8, n), f32)
    for idx, (sh, p) in enumerate(((1, pw[0]), (2, pw[1]), (4, pw[3]))):
        tab_ref[2 * idx] = jnp.where(row >= sh, jnp.broadcast_to(p[0], (8, n)), zero)
        tab_ref[2 * idx + 1] = jnp.where(row >= sh, jnp.broadcast_to(p[1], (8, n)), zero)
    pr, pi = zero, zero
    for i in range(8):
        pr = jnp.where(row == i, jnp.broadcast_to(pw[i][0], (8, n)), pr)
        pi = jnp.where(row == i, jnp.broadcast_to(pw[i][1], (8, n)), pi)
    tab_ref[6] = pr
    tab_ref[7] = pi


def _s5_prep(ar, ai, ldt, br, bi):
    n = S5_GROUPS * S5_STATE
    return pl.pallas_call(
        _s5_prep_kernel,
        out_shape=[jax.ShapeDtypeStruct((8, 8, n), f32), jax.ShapeDtypeStruct((S5_CH, n), f32),
                   jax.ShapeDtypeStruct((S5_CH, n), f32)],
        name="s5_prep",
    )(ar, ai, ldt, br, bi)


def _s5_scan_kernel(u_ref, wb_ref, wc_ref, d_ref, tab_ref, y_ref, hr_ref, hi_ref,
                    xbuf_ref, cr_ref, ci_ref, *, tt):
    t = pl.program_id(2)

    @pl.when(t == 0)
    def _():
        cr_ref[...] = jnp.zeros((8, 512), f32)
        ci_ref[...] = jnp.zeros((8, 512), f32)

    u = u_ref[...]
    xbuf_ref[...] = _dot(u.astype(bf16), wb_ref[0])

    def body(r, carry):
        cr, ci = carry
        rows = pl.ds(pl.multiple_of(r * 8, 8), 8)
        xr = xbuf_ref[rows, 0:512]
        xi = xbuf_ref[rows, 512:1024]
        for idx, sh in enumerate((1, 2, 4)):
            a_r, a_i = tab_ref[2 * idx], tab_ref[2 * idx + 1]
            sr, si = pltpu.roll(xr, sh, 0), pltpu.roll(xi, sh, 0)
            xr, xi = xr + a_r * sr - a_i * si, xi + a_r * si + a_i * sr
        p_r, p_i = tab_ref[6], tab_ref[7]
        hr = xr + p_r * cr - p_i * ci
        hi = xi + p_r * ci + p_i * cr
        xbuf_ref[rows, 0:512] = hr
        xbuf_ref[rows, 512:1024] = hi
        return (jnp.broadcast_to(hr[7:8, :], (8, 512)), jnp.broadcast_to(hi[7:8, :], (8, 512)))

    cr, ci = lax.fori_loop(0, tt // 8, body, (cr_ref[...], ci_ref[...]), unroll=4)
    cr_ref[...] = cr
    ci_ref[...] = ci
    hr_ref[0] = cr[0:1, :]
    hi_ref[0] = ci[0:1, :]
    y_ref[...] = _dot(xbuf_ref[...].astype(bf16), wc_ref[0]) + d_ref[...] * u


def _s5_scan(hcat, wb, wc, d, tabs, bsz, t_len):
    tt = 512
    nt = t_len // tt
    m = hcat.shape[0]
    n = S5_GROUPS * S5_STATE
    return pl.pallas_call(
        functools.partial(_s5_scan_kernel, tt=tt),
        grid=(bsz, S5_SLABS, nt),
        in_specs=[pl.BlockSpec((tt, 128), lambda b, s, t: (b * nt + t, C_S5U // 128 + s)),
                  pl.BlockSpec((1, 128, 1024), lambda b, s, t: (s, 0, 0)),
                  pl.BlockSpec((1, 1024, 128), lambda b, s, t: (s, 0, 0)),
                  pl.BlockSpec((1, 128), lambda b, s, t: (0, s)),
                  pl.BlockSpec((8, 8, 512), lambda b, s, t: (0, 0, s))],
        out_specs=[pl.BlockSpec((tt, 128), lambda b, s, t: (b * nt + t, s)),
                   pl.BlockSpec((1, 1, 512), lambda b, s, t: (b, 0, s)),
                   pl.BlockSpec((1, 1, 512), lambda b, s, t: (b, 0, s))],
        out_shape=[jax.ShapeDtypeStruct((m, 512), f32), jax.ShapeDtypeStruct((bsz, 1, n), f32),
                   jax.ShapeDtypeStruct((bsz, 1, n), f32)],
        scratch_shapes=[pltpu.VMEM((tt, 1024), f32), pltpu.VMEM((8, 512), f32), pltpu.VMEM((8, 512), f32)],
        compiler_params=_cp(("parallel", "parallel", "arbitrary")),
        name="s5_scan",
    )(hcat, wb, wc, d, tabs)


def _sample_small_kernel(scb_ref, scc_ref, sch_ref, scw_ref, scp_ref, u_ref, wb_ref, wc_ref, d_ref,
                         tab_ref, h0r_ref, h0i_ref, sc_ref, z_ref, y_ref, hr_ref, hi_ref):
    z = scc_ref[...] * sch_ref[...]
    w = scw_ref[...]
    y = w[0:1] * scp_ref[0] + w[1:2] * scp_ref[1] + w[2:3] * z
    sc_ref[...] = _rms_unit(scb_ref[...] * y)
    z_ref[...] = z
    u = u_ref[...]
    for s in range(S5_SLABS):
        x = _dot(u[:, s * 128:(s + 1) * 128].astype(bf16), wb_ref[s])
        lanes = slice(s * 512, (s + 1) * 512)
        a_r, a_i = tab_ref[6, 0:1, lanes], tab_ref[7, 0:1, lanes]
        h0r, h0i = h0r_ref[:, lanes], h0i_ref[:, lanes]
        hr = a_r * h0r - a_i * h0i + x[:, 0:512]
        hi = a_r * h0i + a_i * h0r + x[:, 512:1024]
        hr_ref[:, lanes] = hr
        hi_ref[:, lanes] = hi
        hcat = jnp.concatenate([hr, hi], axis=1).astype(bf16)
        cols = slice(s * 128, (s + 1) * 128)
        y_ref[:, cols] = _dot(hcat, wc_ref[s]) + d_ref[:, cols] * u[:, cols]


def _sample_small(scb, scc, sch, scw, scp, u, wb, wc, d, tabs, h0r, h0i):
    bsz = u.shape[0]
    n = S5_GROUPS * S5_STATE
    return pl.pallas_call(
        _sample_small_kernel,
        out_shape=[jax.ShapeDtypeStruct((bsz, 512), f32), jax.ShapeDtypeStruct((bsz, 512), f32),
                   jax.ShapeDtypeStruct((bsz, 512), f32), jax.ShapeDtypeStruct((bsz, n), f32),
                   jax.ShapeDtypeStruct((bsz, n), f32)],
        compiler_params=pltpu.CompilerParams(vmem_limit_bytes=VMEM_LIMIT),
        name="sample_small",
    )(scb, scc, sch, scw, scp, u, wb, wc, d, tabs, h0r, h0i)


def _gelu(x):
    return 0.5 * x * (1.0 + jnp.tanh(math.sqrt(2.0 / math.pi) * (x + 0.044715 * (x * x * x))))


def _mix_kernel(nsa_ref, sc_ref, diff_ref, s5_ref, gw_ref, gb_ref, gain_ref, wo_ref, x_ref,
                g_ref, b_ref, out_ref, outb_ref):
    y = _gelu(s5_ref[...])
    s5o = y * _sigmoid(_dot(y.astype(bf16), gw_ref[...]) + gb_ref[...])
    parts = (_rms_unit(nsa_ref[...]), sc_ref[...], diff_ref[...], _rms_unit(s5o))
    acc = None
    for k, p in enumerate(parts):
        pk = (p * gain_ref[:, k * 512:(k + 1) * 512]).astype(bf16)
        d = _dot(pk, wo_ref[k * 512:(k + 1) * 512, :])
        acc = d if acc is None else acc + d
    o = _layer_norm(DN_ALPHA * x_ref[...] + acc, g_ref[...], b_ref[...])
    out_ref[...] = o
    outb_ref[...] = o.astype(bf16)


def _mix(nsa, sc, diff, s5y, gw, gb, gain, wo, x, g, b, tm, layer):
    m = x.shape[0]
    row = lambda w: pl.BlockSpec((tm, w), lambda i: (i, 0))
    full = lambda shp: pl.BlockSpec(shp, lambda i: (0, 0))
    stacked = lambda shp: pl.BlockSpec((None,) + shp, lambda i: (layer, 0, 0))
    return pl.pallas_call(
        _mix_kernel,
        grid=(m // tm,),
        in_specs=[row(512), row(512), row(512), row(512), stacked((512, 512)), full((1, 512)),
                  full((1, D_MODEL)), stacked((D_MODEL, D_MODEL)), row(D_MODEL), full((1, D_MODEL)),
                  full((1, D_MODEL))],
        out_specs=[row(D_MODEL), row(D_MODEL)],
        out_shape=[jax.ShapeDtypeStruct((m, D_MODEL), f32), jax.ShapeDtypeStruct((m, D_MODEL), bf16)],
        compiler_params=_cp(("parallel",)),
        name="mix_outproj_ln",
    )(nsa, sc, diff, s5y, gw, gb, gain, wo, x, g, b)


def _ffn_tail(acc_ref, xres_ref, g_ref, b_ref, out_ref, outb_ref):
    o = _layer_norm(DN_ALPHA * xres_ref[...] + acc_ref[...], g_ref[...], b_ref[...])
    out_ref[...] = o
    outb_ref[...] = o.astype(bf16)


def _ffn_prompt_kernel(x_ref, halo_ref, wa_ref, wb_ref, wd_ref, cw_ref, xres_ref, g_ref, b_ref,
                       out_ref, outb_ref, tail_ref, acc_ref, abuf_ref, *, tm, nf, tiles_per_seq):
    i = pl.program_id(0)
    f = pl.program_id(1)

    @pl.when(f == 0)
    def _():
        acc_ref[...] = jnp.zeros_like(acc_ref)

    x = x_ref[...]
    halo = halo_ref[...]
    keep = jnp.where(i % tiles_per_seq != 0, 1.0, 0.0)
    cw = cw_ref[...]
    down = None
    for c in range(abuf_ref.shape[0]):
        cols = slice(c * FFN_CHUNK, (c + 1) * FFN_CHUNK)
        a = _dot(x, wa_ref[:, cols])
        bb = _dot(x, wb_ref[:, cols])
        abuf_ref[c, 0:16, :] = _dot(halo, wa_ref[:, cols]) * keep
        abuf_ref[c, 16:16 + tm, :] = a
        ac = (cw[0:1, cols] * abuf_ref[c, 14:14 + tm, :] + cw[1:2, cols] * abuf_ref[c, 15:15 + tm, :]
              + cw[2:3, cols] * a)
        gate = (ac * _sigmoid(ac) * bb).astype(bf16)
        d = _dot(gate, wd_ref[cols, :])
        down = d if down is None else down + d
        tail_ref[0, :, cols] = a[tm - 8:tm]
    acc_ref[...] += down

    @pl.when(f == nf - 1)
    def _():
        _ffn_tail(acc_ref, xres_ref, g_ref, b_ref, out_ref, outb_ref)


def _ffn_prompt(xb, x, wup, wd, cw, g, b, t_len, layer):
    m = x.shape[0]
    tm, tf = 512, FFN_TF
    nf = D_FF // tf
    full = lambda shp: pl.BlockSpec(shp, lambda i, f: (0, 0))
    return pl.pallas_call(
        functools.partial(_ffn_prompt_kernel, tm=tm, nf=nf, tiles_per_seq=t_len // tm),
        grid=(m // tm, nf),
        in_specs=[pl.BlockSpec((tm, D_MODEL), lambda i, f: (i, 0)),
                  pl.BlockSpec((16, D_MODEL), lambda i, f: (jnp.maximum(i * (tm // 16) - 1, 0), 0)),
                  pl.BlockSpec((None, D_MODEL, tf), lambda i, f: (layer, 0, f)),
                  pl.BlockSpec((None, D_MODEL, tf), lambda i, f: (layer, 0, f + nf)),
                  pl.BlockSpec((None, tf, D_MODEL), lambda i, f: (layer, f, 0)),
                  pl.BlockSpec((3, tf), lambda i, f: (0, f)),
                  pl.BlockSpec((tm, D_MODEL), lambda i, f: (i, 0)),
                  full((1, D_MODEL)), full((1, D_MODEL))],
        out_specs=[pl.BlockSpec((tm, D_MODEL), lambda i, f: (i, 0)),
                   pl.BlockSpec((tm, D_MODEL), lambda i, f: (i, 0)),
                   pl.BlockSpec((1, 8, tf), lambda i, f: (i, 0, f))],
        out_shape=[jax.ShapeDtypeStruct((m, D_MODEL), f32), jax.ShapeDtypeStruct((m, D_MODEL), bf16),
                   jax.ShapeDtypeStruct((m // tm, 8, D_FF), f32)],
        scratch_shapes=[pltpu.VMEM((tm, D_MODEL), f32), pltpu.VMEM((tf // FFN_CHUNK, 16 + tm, FFN_CHUNK), f32)],
        compiler_params=_cp(("parallel", "arbitrary")),
        name="ffn_prompt",
    )(xb, xb, wup, wup, wd, cw, x, g, b)


def _ffn_sample_kernel(x_ref, p0_ref, p1_ref, wa_ref, wb_ref, wd_ref, cw_ref, xres_ref, g_ref, b_ref,
                       out_ref, outb_ref, aup_ref, acc_ref, *, nf):
    f = pl.program_id(0)

    @pl.when(f == 0)
    def _():
        acc_ref[...] = jnp.zeros_like(acc_ref)

    x = x_ref[...]
    a = _dot(x, wa_ref[...])
    bb = _dot(x, wb_ref[...])
    cw = cw_ref[...]
    ac = cw[0:1] * p0_ref[...] + cw[1:2] * p1_ref[...] + cw[2:3] * a
    gate = (ac * _sigmoid(ac) * bb).astype(bf16)
    acc_ref[...] += _dot(gate, wd_ref[...])
    aup_ref[...] = a

    @pl.when(f == nf - 1)
    def _():
        _ffn_tail(acc_ref, xres_ref, g_ref, b_ref, out_ref, outb_ref)


def _ffn_sample(xb, x, p0, p1, wup, wd, cw, g, b, layer):
    m = x.shape[0]
    tf = FFN_TF
    nf = D_FF // tf
    full = lambda shp: pl.BlockSpec(shp, lambda f: (0, 0))
    return pl.pallas_call(
        functools.partial(_ffn_sample_kernel, nf=nf),
        grid=(nf,),
        in_specs=[full((m, D_MODEL)), pl.BlockSpec((m, tf), lambda f: (0, f)),
                  pl.BlockSpec((m, tf), lambda f: (0, f)),
                  pl.BlockSpec((None, D_MODEL, tf), lambda f: (layer, 0, f)),
                  pl.BlockSpec((None, D_MODEL, tf), lambda f: (layer, 0, f + nf)),
                  pl.BlockSpec((None, tf, D_MODEL), lambda f: (layer, f, 0)),
                  pl.BlockSpec((3, tf), lambda f: (0, f)),
                  full((m, D_MODEL)), full((1, D_MODEL)), full((1, D_MODEL))],
        out_specs=[full((m, D_MODEL)), full((m, D_MODEL)), pl.BlockSpec((m, tf), lambda f: (0, f))],
        out_shape=[jax.ShapeDtypeStruct((m, D_MODEL), f32), jax.ShapeDtypeStruct((m, D_MODEL), bf16),
                   jax.ShapeDtypeStruct((m, D_FF), f32)],
        scratch_shapes=[pltpu.VMEM((m, D_MODEL), f32)],
        compiler_params=_cp(("arbitrary",)),
        name="ffn_sample",
    )(xb, p0, p1, wup, wup, wd, cw, x, g, b)


def _cmp_sel_sample_kernel(pt_ref, new_ref, qraw_ref, pe1_ref, pe2_ref, w1_ref, w2_ref, cov_ref, pool_ref,
                           ocmp_ref, idx_ref, buf_ref, x_ref, sb_ref, kvc_ref, cst_ref, sem_ref,
                           *, layer, n_pages, n_batch, past_len):
    b = pl.program_id(0)
    slot = b % 2
    n = n_pages * (PAGE_SIZE // CMP_STRIDE)

    def copy(bb, sl, p):
        return pltpu.make_async_copy(pool_ref.at[layer, pt_ref[bb, p]], buf_ref.at[sl, p], sem_ref.at[sl])

    def fetch(bb, sl):
        for p in range(n_pages):
            copy(bb, sl, p).start()

    @pl.when(b == 0)
    def _():
        fetch(0, 0)
        for idx, (pe_ref, w_ref) in enumerate(((pe1_ref, w1_ref), (pe2_ref, w2_ref))):
            tot = None
            for s in range(CMP_STRIDE):
                pes = jnp.broadcast_to(pe_ref[:, s * 256:(s + 1) * 256], (8, 256)).astype(bf16)
                d = _dot(pes, w_ref[s * 256:(s + 1) * 256, :])
                tot = d if tot is None else tot + d
            cst_ref[idx] = tot

    @pl.when(b + 1 < n_batch)
    def _():
        fetch(b + 1, 1 - slot)

    for p in range(n_pages):
        copy(b, slot, p).wait()

    r_i = _row((PAGE_SIZE, PAGE_SIZE))
    pick = jnp.where(_lane((PAGE_SIZE, PAGE_SIZE)) == CMP_STRIDE * (r_i % 8) + r_i // 8, 1.0, 0.0).astype(bf16)

    def regroup(q, carry):
        rows = pl.ds(pl.multiple_of(q * 16, 16), 16)
        r0 = _dot_nt(pick, buf_ref[slot, 2 * q].reshape(256, PAGE_SIZE).astype(bf16))
        r1 = _dot_nt(pick, buf_ref[slot, 2 * q + 1].reshape(256, PAGE_SIZE).astype(bf16))
        for s in range(CMP_STRIDE):
            pair = jnp.concatenate([r0[s * 8:(s + 1) * 8], r1[s * 8:(s + 1) * 8]], axis=0)
            x_ref[rows, s * 256:(s + 1) * 256] = pair.astype(bf16)
        return carry

    lax.fori_loop(0, n_pages // 2, regroup, 0, unroll=4)

    xb = x_ref[...]
    a = _dot(xb, w1_ref[...]) + cst_ref[0, 0:1, :]
    bm = _dot(xb, w2_ref[...]) + cst_ref[1, 0:1, :]
    new8 = jnp.broadcast_to(new_ref[0], (8, 256)).astype(bf16)
    row0 = _row((8, 256)) == 0
    a_new = cst_ref[0] + jnp.where(row0, _dot(new8, w1_ref[0:256, :]), 0.0)
    b_new = cst_ref[1] + jnp.where(row0, _dot(new8, w2_ref[0:256, :]), 0.0)
    sb_ref[0:n, :] = bm
    sb_ref[n:n + 8, :] = b_new
    sb_ref[n + 8:n + 16, :] = jnp.zeros((8, 256), f32)
    kvc_ref[0:n, :] = a + sb_ref[1:n + 1, :]
    kvc_ref[n:n + 8, :] = a_new + sb_ref[n + 1:n + 9, :]
    kvc_ref[n + 8:NC_PAD, :] = jnp.zeros((NC_PAD - n - 8, 256), f32)

    qpos = past_len
    n_sel = past_len // SEL_BLOCK + 1
    n_cmp = n_sel * SEL_BLOCK // CMP_STRIDE - 1
    cur = qpos // SEL_BLOCK
    n_i = _lane((1, NC_PAD))
    maskc = ((16 * n_i + 31) <= qpos) & (n_i < n_cmp)
    cov = cov_ref[...]
    jj = _lane((1, 256))
    forced = (jj == 0) | (jj == cur) | (jj == cur - 1)
    kk = _row((256, 256))
    jjm = _lane((256, 256))
    eye = kk == jjm
    before = jnp.where(kk < jjm, 1.0, 0.0).astype(bf16)
    slot_id = _row((SEL_TOPK, 256))
    jj16 = _lane((SEL_TOPK, 256)).astype(f32)
    qraw = qraw_ref[0]
    for h in range(NSA_KV_HEADS):
        kc = kvc_ref[:, 0:128].astype(bf16)
        vc = kvc_ref[:, 128:256].astype(bf16)
        qr = (_nsa_q8(qraw, h, h) * SCALE).astype(bf16)
        p_c = _msoftmax(_dot_nt(qr, kc), maskc)
        ocmp_ref[0, h] = _dot(p_c.astype(bf16), vc)
        psum = jnp.broadcast_to(p_c[0:1] + p_c[1:2] + p_c[2:3] + p_c[3:4], (8, NC_PAD))
        p_hi, p_lo = _split_hi_lo(psum)
        imp = (_dot(p_hi, cov) + _dot(p_lo, cov))[0:1]
        imp = jnp.where(forced, imp + FORCE_BONUS, imp)
        imp = jnp.where(jj <= cur, imp, -FORCE_BONUS)
        imp = jnp.where(jj < n_sel, imp, -3e38)
        imp_j = jnp.broadcast_to(imp, (256, 256))
        imp_k = jnp.broadcast_to(jnp.sum(jnp.where(eye, imp_j, 0.0), axis=1, keepdims=True), (256, 256))
        beats = (imp_k > imp_j) | ((imp_k == imp_j) & (kk < jjm))
        rank = jnp.sum(jnp.where(beats, 1.0, 0.0), axis=0, keepdims=True)
        sel = jnp.where((rank < SEL_TOPK) & (jj < n_sel), 1.0, 0.0)
        pos = _dot(jnp.broadcast_to(sel, (8, 256)).astype(bf16), before)[0:1]
        hit = (jnp.broadcast_to(pos, (SEL_TOPK, 256)) == slot_id.astype(f32)) & (jnp.broadcast_to(sel, (SEL_TOPK, 256)) > 0.5)
        ids = jnp.sum(jnp.where(hit, jj16, 0.0), axis=1, keepdims=True)
        idx_ref[0, h * SEL_TOPK:(h + 1) * SEL_TOPK, :] = jnp.broadcast_to(ids, (SEL_TOPK, 128)).astype(jnp.int32)


def _cmp_sel_sample(page_table, new_rows, qraw, pe1, pe2, w1, w2, pool_t, layer, past_len):
    n_batch, n_pages = page_table.shape
    n = n_pages * (PAGE_SIZE // CMP_STRIDE)
    n_sel = past_len // SEL_BLOCK + 1
    cov = _cover_matrix(NC_PAD, 256, n_sel * SEL_BLOCK // CMP_STRIDE - 1, n_sel)
    full = lambda shp: pl.BlockSpec(shp, lambda b, pt: (0,) * len(shp))
    grid_spec = pltpu.PrefetchScalarGridSpec(
        num_scalar_prefetch=1,
        grid=(n_batch,),
        in_specs=[pl.BlockSpec((1, 1, 256), lambda b, pt: (b, 0, 0)),
                  pl.BlockSpec((1, 1, 512), lambda b, pt: (b, 0, 0)), full((1, 4096)), full((1, 4096)),
                  full((4096, 256)), full((4096, 256)), full((NC_PAD, 256)), pl.BlockSpec(memory_space=pl.ANY)],
        out_specs=[pl.BlockSpec((1, 2, 8, 128), lambda b, pt: (b, 0, 0, 0)),
                   pl.BlockSpec((1, 2 * SEL_TOPK, 128), lambda b, pt: (b, 0, 0))],
        scratch_shapes=[pltpu.VMEM((2, n_pages, 2, 2, HEAD_DIM, PAGE_SIZE), f32),
                        pltpu.VMEM((n, CMP_STRIDE * 256), bf16), pltpu.VMEM((n + 16, 256), f32),
                        pltpu.VMEM((NC_PAD, 256), f32), pltpu.VMEM((2, 8, 256), f32),
                        pltpu.SemaphoreType.DMA((2,))],
    )
    return pl.pallas_call(
        functools.partial(_cmp_sel_sample_kernel, layer=layer, n_pages=n_pages, n_batch=n_batch,
                          past_len=past_len),
        grid_spec=grid_spec,
        out_shape=[jax.ShapeDtypeStruct((n_batch, 2, 8, 128), f32),
                   jax.ShapeDtypeStruct((n_batch, 2 * SEL_TOPK, 128), jnp.int32)],
        compiler_params=_cp(("arbitrary",)),
        name="cmp_sel_sample",
    )(page_table, new_rows, qraw, pe1, pe2, w1, w2, cov, pool_t)


def _rows8(row_chunks):
    rid = _row((8, 128))
    out = jnp.zeros((8, 128), f32)
    for r, c in enumerate(row_chunks):
        out = jnp.where(rid == r, jnp.broadcast_to(c, (8, 128)), out)
    return out


def _nsa_q8(qrow, h, half):
    chunks = []
    for g in range(4):
        hd = h * 4 + g
        chunks.append(qrow[:, (hd // 2) * 128:(hd // 2 + 1) * 128])
    q8 = _rows8(chunks)
    sw = pltpu.roll(q8, 64, 1)
    in_place = (_row((8, 128)) % 2) == half
    q8 = jnp.where(in_place, q8, sw)
    return jnp.where((_lane((8, 128)) // 64) == half, q8, 0.0)


def _nsa_sel_sample_kernel(pt_ref, idx_ref, qrot_ref, gate_ref, ocmp_ref, snew_ref, wnew_ref, win_ref,
                           pool_ref, out_ref, nwin_ref, buf_ref, sem_ref, *, layer, n_batch, n_past_blocks):
    b = pl.program_id(0)
    slot = b % 2

    def copy(bb, sl, h, s, kv):
        j = jnp.minimum(idx_ref[bb, h * SEL_TOPK + s], n_past_blocks - 1)
        page = pt_ref[bb, j // (PAGE_SIZE // SEL_BLOCK)]
        return pltpu.make_async_copy(pool_ref.at[layer, page, kv], buf_ref.at[sl, h, kv, s], sem_ref.at[sl])

    def for_all(bb, sl, fn):
        for h in range(NSA_KV_HEADS):
            for s in range(SEL_TOPK):
                for kv in range(2):
                    fn(copy(bb, sl, h, s, kv))

    @pl.when(b == 0)
    def _():
        for_all(0, 0, lambda c: c.start())

    @pl.when(b + 1 < n_batch)
    def _():
        for_all(b + 1, 1 - slot, lambda c: c.start())

    gs = _sigmoid(gate_ref[0])
    lane128 = _lane((1, 128))
    qrot = qrot_ref[0]

    for_all(b, slot, lambda c: c.wait())

    snew = snew_ref[0]
    wnew = wnew_ref[0]
    o_all = []
    for h in range(NSA_KV_HEADS):
        qo = _nsa_q8(qrot, h, h) * SCALE
        qob = qo.astype(bf16)
        halfmask = (lane128 // 64) == h
        scores, valids = [], []
        has_new = False
        for s in range(SEL_TOPK):
            j = idx_ref[b, h * SEL_TOPK + s]
            kt = buf_ref[slot, h, 0, s].reshape(128, 128).astype(bf16)
            valid = ((lane128 // SEL_BLOCK) == (j % (PAGE_SIZE // SEL_BLOCK))) & (j < n_past_blocks)
            scores.append(jnp.where(valid, _dot(qob, kt), NEG_INF))
            valids.append(valid)
            has_new = jnp.logical_or(has_new, j == n_past_blocks)
        s_new = jnp.sum(qo * snew[:, 0:128], axis=-1, keepdims=True)
        s_new = jnp.where(has_new, s_new, NEG_INF)
        smax = scores[0]
        for sc in scores[1:]:
            smax = jnp.maximum(smax, sc)
        mx = jnp.maximum(jnp.max(smax, axis=-1, keepdims=True), s_new)
        e_new = jnp.where(has_new, jnp.exp(s_new - mx), 0.0)
        esum = jnp.zeros((8, 128), f32)
        acc = jnp.zeros((8, 128), f32)
        for s in range(SEL_TOPK):
            e = jnp.where(valids[s], jnp.exp(scores[s] - mx), 0.0)
            esum = esum + e
            vt = buf_ref[slot, h, 1, s].reshape(128, 128).astype(bf16)
            acc = acc + _dot_nt(e.astype(bf16), vt)
        inv = 1.0 / jnp.maximum(jnp.sum(esum, axis=-1, keepdims=True) + e_new, 1e-30)
        o_slc = (acc + e_new * snew[:, 128:256]) * inv
        wt = win_ref[0, 0]
        kt = wt[0].reshape(128, WINDOW).astype(bf16)
        vt = wt[1].reshape(128, WINDOW).astype(bf16)
        maskw = _lane((1, WINDOW)) >= 1
        s_w = jnp.where(maskw, _dot(qob, kt), NEG_INF)
        sw_new = jnp.sum(qo * wnew[:, 0:128], axis=-1, keepdims=True)
        mx = jnp.maximum(jnp.max(s_w, axis=-1, keepdims=True), sw_new)
        e = jnp.where(maskw, jnp.exp(s_w - mx), 0.0)
        e_new = jnp.exp(sw_new - mx)
        inv = 1.0 / jnp.maximum(jnp.sum(e, axis=-1, keepdims=True) + e_new, 1e-30)
        o_win = (_dot_nt(e.astype(bf16), vt) + e_new * wnew[:, 128:256]) * inv
        gate_rows = []
        for c in range(3):
            gate_rows.append(_rows8([jnp.broadcast_to(gs[:, (h * 4 + g) * 3 + c:(h * 4 + g) * 3 + c + 1], (1, 128))
                                     for g in range(4)]))
        o8 = gate_rows[0] * ocmp_ref[0, h] + gate_rows[1] * o_slc + gate_rows[2] * o_win
        o_all.append(jnp.where(halfmask, o8, 0.0))

    lo = lane128 < 64
    for h in range(NSA_KV_HEADS):
        o8 = o_all[h]
        o8s = pltpu.roll(o8, 64, 1)
        low_src, high_src = (o8, o8s) if h == 0 else (o8s, o8)
        for gp in range(2):
            ch = jnp.where(lo, low_src[2 * gp:2 * gp + 1], high_src[2 * gp + 1:2 * gp + 2])
            out_ref[0, :, h * 256 + gp * 128:h * 256 + (gp + 1) * 128] = ch

    last = _lane((1, WINDOW)) == WINDOW - 1
    eye64 = _row((64, 64)) == _lane((64, 64))
    for kv in range(2):
        for h in range(NSA_KV_HEADS):
            c = kv * 2 + h
            newc = jnp.broadcast_to(wnew[:, c * 64:(c + 1) * 64], (64, 64))
            colv = jnp.sum(jnp.where(eye64, newc, 0.0), axis=1, keepdims=True)
            old = win_ref[0, 0, kv, h]
            nwin_ref[0, kv, h] = jnp.where(last, colv, pltpu.roll(old, WINDOW - 1, 1))


def _nsa_sel_sample(page_table, sel_idx, qrot, gates, ocmp, snew, wnew, win_t, pool_t, layer, past_len):
    n_batch = page_table.shape[0]
    row = lambda w: pl.BlockSpec((1, 1, w), lambda b, pt, ix: (b, 0, 0))
    grid_spec = pltpu.PrefetchScalarGridSpec(
        num_scalar_prefetch=2,
        grid=(n_batch,),
        in_specs=[row(512), row(128), pl.BlockSpec((1, 2, 8, 128), lambda b, pt, ix: (b, 0, 0, 0)),
                  row(256), row(256),
                  pl.BlockSpec((1, 1, 2, 2, HEAD_DIM, WINDOW), lambda b, pt, ix: (layer, b, 0, 0, 0, 0)),
                  pl.BlockSpec(memory_space=pl.ANY)],
        out_specs=[row(512), pl.BlockSpec((1, 2, 2, HEAD_DIM, WINDOW), lambda b, pt, ix: (b, 0, 0, 0, 0))],
        scratch_shapes=[pltpu.VMEM((2, NSA_KV_HEADS, 2, SEL_TOPK, 2, HEAD_DIM, PAGE_SIZE), f32),
                        pltpu.SemaphoreType.DMA((2,))],
    )
    return pl.pallas_call(
        functools.partial(_nsa_sel_sample_kernel, layer=layer, n_batch=n_batch,
                          n_past_blocks=past_len // SEL_BLOCK),
        grid_spec=grid_spec,
        out_shape=[jax.ShapeDtypeStruct((n_batch, 1, 512), f32),
                   jax.ShapeDtypeStruct((n_batch, 2, 2, HEAD_DIM, WINDOW), f32)],
        compiler_params=_cp(("arbitrary",)),
        name="nsa_sel_sample",
    )(page_table, sel_idx, qrot, gates, ocmp, snew, wnew, win_t, pool_t)


def _diff_sample_kernel(pt_ref, q_ref, new_ref, dl_ref, pool_ref, out_ref, buf_ref, m_ref, l_ref,
                        acc_ref, sem_ref, *, layer, n_pages, n_batch, n_split, lam_init):
    b = pl.program_id(0)
    hf = pl.program_id(1)
    step = b * n_split + hf
    slot = step % 2
    pps = n_pages // n_split
    rows = pps * PAGE_SIZE

    def copy(bb, hh, sl, p):
        return pltpu.make_async_copy(pool_ref.at[layer, pt_ref[bb, hh * pps + p]],
                                     buf_ref.at[sl, pl.ds(p * 4 * PAGE_SIZE, 4 * PAGE_SIZE), :],
                                     sem_ref.at[sl])

    def fetch(bb, hh, sl):
        for p in range(pps):
            copy(bb, hh, sl, p).start()

    @pl.when(step == 0)
    def _():
        fetch(0, 0, 0)

    @pl.when(step + 1 < n_batch * n_split)
    def _():
        nxt = step + 1
        fetch(nxt // n_split, nxt % n_split, 1 - slot)

    @pl.when(hf == 0)
    def _():
        m_ref[...] = jnp.full(m_ref.shape, NEG_INF, f32)
        l_ref[...] = jnp.zeros(l_ref.shape, f32)
        acc_ref[...] = jnp.zeros(acc_ref.shape, f32)

    for p in range(pps):
        copy(b, hf, slot, p).wait()

    qrow = q_ref[0]
    lane = _lane((8, 128))
    rid = _row((8, 128))
    q8s = []
    for h in range(2):
        q8 = _rows8([qrow[:, h * 256 + (r // 2) * 128:h * 256 + (r // 2 + 1) * 128] for r in range(4)])
        q8s.append(jnp.where((lane // 64) == (rid % 2), q8, 0.0) * SCALE)
    for h in range(2):
        k = buf_ref[slot, pl.ds(h, rows, stride=4), :].astype(bf16)
        v = buf_ref[slot, pl.ds(2 + h, rows, stride=4), :].astype(bf16)
        s = _dot_nt(q8s[h].astype(bf16), k)
        m_old = m_ref[h]
        m_new = jnp.maximum(m_old, jnp.max(s, axis=-1, keepdims=True))
        alpha = jnp.exp(m_old - m_new)
        p_ = jnp.exp(s - m_new)
        l_ref[h] = alpha * l_ref[h] + jnp.sum(p_, axis=-1, keepdims=True)
        acc_ref[h] = alpha * acc_ref[h] + _dot(p_.astype(bf16), v)
        m_ref[h] = m_new

    @pl.when(hf == n_split - 1)
    def _():
        lam = _diff_lambda(dl_ref[...], lam_init)
        new = new_ref[0]
        for h in range(2):
            s_new = jnp.sum(q8s[h] * new[:, h * 128:(h + 1) * 128], axis=-1, keepdims=True)
            m_old = m_ref[h]
            m_new = jnp.maximum(m_old, s_new)
            alpha = jnp.exp(m_old - m_new)
            p_new = jnp.exp(s_new - m_new)
            l_ = alpha * l_ref[h] + p_new
            acc = alpha * acc_ref[h] + p_new * new[:, 256 + h * 128:256 + (h + 1) * 128]
            o = acc * (1.0 / jnp.maximum(l_, 1e-30))
            for g in range(2):
                og = o[2 * g:2 * g + 1] - lam * o[2 * g + 1:2 * g + 2]
                out_ref[0, :, (h * 2 + g) * 128:(h * 2 + g + 1) * 128] = _rms_unit(og) * (1.0 - lam_init)


def _diff_sample(page_table, dqrot, new_rows, dl, pool, layer, lam_init):
    n_batch, n_pages = page_table.shape
    n_split = 2
    pps = n_pages // n_split
    row = lambda w: pl.BlockSpec((1, 1, w), lambda b, s, pt: (b, 0, 0))
    grid_spec = pltpu.PrefetchScalarGridSpec(
        num_scalar_prefetch=1,
        grid=(n_batch, n_split),
        in_specs=[row(512), row(512), pl.BlockSpec((4, 64), lambda b, s, pt: (0, 0)),
                  pl.BlockSpec(memory_space=pl.ANY)],
        out_specs=row(512),
        scratch_shapes=[pltpu.VMEM((2, pps * 4 * PAGE_SIZE, 128), f32), pltpu.VMEM((2, 8, 1), f32),
                        pltpu.VMEM((2, 8, 1), f32), pltpu.VMEM((2, 8, 128), f32),
                        pltpu.SemaphoreType.DMA((2,))],
    )
    return pl.pallas_call(
        functools.partial(_diff_sample_kernel, layer=layer, n_pages=n_pages, n_batch=n_batch,
                          n_split=n_split, lam_init=lam_init),
        grid_spec=grid_spec,
        out_shape=jax.ShapeDtypeStruct((n_batch, 1, 512), f32),
        compiler_params=_cp(("arbitrary", "arbitrary")),
        name="diff_sample",
    )(page_table, dqrot, new_rows, dl, pool)


def _prep_w_in(w):
    parts = jnp.split(w, [sum(IN_SPLITS[:i + 1]) for i in range(len(IN_SPLITS) - 1)], axis=-1)
    nq, ncmp, nslc, nwin, ngate, scb, scc, sch, dq, dk, dv, s5u = parts
    gate = jnp.pad(ngate, ((0, 0), (0, HC - C_GATE - ngate.shape[1])))
    wcat = jnp.concatenate([nq, scb, scc, sch, dq, s5u, ncmp, nslc, nwin, dk, dv, gate], axis=-1).astype(bf16)
    return jnp.transpose(wcat.reshape(D_MODEL, HC // IN_TN, IN_TN), (1, 0, 2))


def _prep_phi(pe, w):
    w2 = w.reshape(2, 2, CMP_STRIDE, HEAD_DIM, HEAD_DIM)
    wc = jnp.repeat(w2, 2, axis=0)
    eye = jnp.eye(4, dtype=f32)
    ws, pes = [], []
    for half in range(2):
        ws.append(jnp.einsum('csde,cf->scdfe', wc[:, half], eye).reshape(4096, 256).astype(bf16))
        pc = jnp.repeat(pe[:, half * CMP_STRIDE:(half + 1) * CMP_STRIDE], 2, axis=0)
        pes.append(jnp.transpose(pc, (1, 0, 2)).reshape(1, 4096))
    return pes[0], pes[1], ws[0], ws[1]


def _prep_s5(bbr, bbi, c_re, c_im):
    eye = jnp.eye(8, dtype=f32)

    def wb_of(bb):
        x = bb.reshape(S5_CH, S5_SLABS, 8, S5_STATE)
        return jnp.einsum('csgp,hg->shcgp', x, eye).reshape(S5_SLABS, 128, 512)

    wb = jnp.concatenate([wb_of(bbr), wb_of(bbi)], axis=-1).astype(bf16)

    def wc_of(c):
        x = c.reshape(S5_SLABS, 8, S5_CH, S5_STATE)
        return jnp.einsum('sgcp,hg->shpgc', x, eye).reshape(S5_SLABS, 512, 128)

    wc = jnp.concatenate([wc_of(c_re), -wc_of(c_im)], axis=1).astype(bf16)
    return wb, wc


def _rope_tables(pos):
    half = HEAD_DIM // 2
    inv = ROPE_THETA ** (-jnp.arange(half, dtype=f32) / half)
    ang = pos.astype(f32)[:, None] * inv[None, :]
    c, s = jnp.cos(ang), jnp.sin(ang)
    return jnp.tile(c, (1, 4)), jnp.tile(jnp.concatenate([-s, s], axis=1), (1, 2))


def kernel(x_prompt, x_sample, cache_nsa_cmp, cache_nsa_slc, cache_diff, state_nsa_win, state_sconv, state_s5_re, state_s5_im, state_ffn_conv, page_table, w_in, nsa_phi_pe, nsa_phi_w, sc_conv_w, diff_lambda, s5_a_re, s5_a_im, s5_log_dt, s5_b_re, s5_b_im, s5_c_re, s5_c_im, s5_d, s5_glu_w, s5_glu_b, mix_gain, w_out, ln1_g, ln1_b, ffn_w_up, ffn_conv_w, ffn_w_down, ln2_g, ln2_b):
    bp, t_len, _ = x_prompt.shape
    bs = x_sample.shape[0]
    n_pool = cache_nsa_cmp.shape[1]
    n_pages = page_table.shape[1]
    past_len = n_pages * PAGE_SIZE
    n_state = S5_GROUPS * S5_STATE
    mp = bp * t_len

    cos_p, sin_p = _rope_tables(jnp.arange(t_len))
    cos_s, sin_s = _rope_tables(jnp.full((bs,), past_len))

    pool_cmp = jnp.transpose(cache_nsa_cmp, (0, 1, 3, 4, 5, 2))
    pool_slc = jnp.transpose(cache_nsa_slc, (0, 1, 3, 4, 5, 2))
    pool_diff = cache_diff.reshape(DEPTH, n_pool, PAGE_SIZE * 4, 128)
    win_t = jnp.transpose(state_nsa_win, (0, 1, 3, 4, 5, 2))

    xp = x_prompt.reshape(mp, D_MODEL)
    xs = x_sample.reshape(bs, D_MODEL)
    xp_b, xs_b = xp.astype(bf16), xs.astype(bf16)

    outs_p = {k: [] for k in ('cmp', 'slc', 'win', 'diff', 'sc', 's5r', 's5i', 'ffn')}
    outs_s = {k: [] for k in ('cmp', 'slc', 'win', 'diff', 'sc', 's5r', 's5i', 'ffn')}

    gw = s5_glu_w.astype(bf16)
    wo = w_out.astype(bf16)
    wup = ffn_w_up.astype(bf16)
    wdn = ffn_w_down.astype(bf16)

    for l in range(DEPTH):
        lam_init = 0.8 - 0.6 * math.exp(-0.3 * l)
        w_in_l = _prep_w_in(w_in[l])
        pe1, pe2, w1, w2 = _prep_phi(nsa_phi_pe[l], nsa_phi_w[l])
        tabs, bbr, bbi = _s5_prep(s5_a_re[l].reshape(1, n_state), s5_a_im[l].reshape(1, n_state),
                                  jnp.repeat(s5_log_dt[l], S5_STATE).reshape(1, n_state),
                                  jnp.transpose(s5_b_re[l], (2, 0, 1)).reshape(S5_CH, n_state),
                                  jnp.transpose(s5_b_im[l], (2, 0, 1)).reshape(S5_CH, n_state))
        wb5, wc5 = _prep_s5(bbr, bbi, s5_c_re[l], s5_c_im[l])
        d5 = s5_d[l].reshape(1, 512)
        gb = s5_glu_b[l].reshape(1, 512)
        gain = mix_gain[l].reshape(1, D_MODEL)
        g1, b1 = ln1_g[l].reshape(1, D_MODEL), ln1_b[l].reshape(1, D_MODEL)
        g2, b2 = ln2_g[l].reshape(1, D_MODEL), ln2_b[l].reshape(1, D_MODEL)
        cwf = ffn_conv_w[l]
        scw = sc_conv_w[l]
        dl = diff_lambda[l]

        hcat = _in_proj(xp_b, w_in_l, 1024)
        qrot, dqrot, kvslc, kvwin, kvdiff = _rope(hcat, cos_p, sin_p, 512)
        kvcmp = hcat[:, C_CMP:C_CMP + 256]
        kvc = _cmp_prompt(kvcmp.reshape(mp // CMP_STRIDE, CMP_STRIDE * 256), pe1, pe2, w1, w2, bp)
        nsa = _nsa_prompt(hcat, qrot, kvc, kvslc, kvwin, bp, t_len)
        dif = _diff_prompt(dqrot, kvdiff, dl, bp, t_len, lam_init)
        sc, sc_tail = _sconv_prompt(hcat, scw, bp, t_len)
        s5y, s5r, s5i = _s5_scan(hcat, wb5, wc5, d5, tabs, bp, t_len)
        x1, x1b = _mix(nsa, sc, dif, s5y, gw, gb, gain, wo, xp, g1, b1, 256, l)
        xp, xp_b, ffn_tail = _ffn_prompt(x1b, x1, wup, wdn, cwf, g2, b2, t_len, l)

        outs_p['cmp'].append(kvcmp.reshape(bp, t_len, 2, 2, HEAD_DIM))
        outs_p['slc'].append(kvslc.reshape(bp, t_len, 2, 2, HEAD_DIM))
        outs_p['win'].append(kvwin.reshape(bp, t_len, 2, 2, HEAD_DIM)[:, t_len - WINDOW:])
        outs_p['diff'].append(kvdiff.reshape(bp, t_len, 2, 2, 2 * HEAD_DIM))
        outs_p['sc'].append(sc_tail[:, 6:8])
        outs_p['s5r'].append(s5r.reshape(bp, S5_GROUPS, S5_STATE))
        outs_p['s5i'].append(s5i.reshape(bp, S5_GROUPS, S5_STATE))
        tiles_per_seq = ffn_tail.shape[0] // bp
        outs_p['ffn'].append(ffn_tail.reshape(bp, tiles_per_seq, 8, D_FF)[:, -1, 6:8])

        hs = _in_proj(xs_b, w_in_l, bs)
        qrot_s, dqrot_s, kvslc_s, kvwin_s, kvdiff_s = _rope(hs, cos_s, sin_s, bs)
        kvcmp_s = hs[:, C_CMP:C_CMP + 256]
        ocmp_s, sel_s = _cmp_sel_sample(page_table, kvcmp_s.reshape(bs, 1, 256),
                                        hs[:, C_NQ:C_NQ + 512].reshape(bs, 1, 512), pe1, pe2, w1, w2,
                                        pool_cmp, l, past_len)
        nsa_s, nwin_t = _nsa_sel_sample(page_table, sel_s[:, :, 0], qrot_s.reshape(bs, 1, 512),
                                        hs[:, C_GATE:C_GATE + 128].reshape(bs, 1, 128), ocmp_s,
                                        kvslc_s.reshape(bs, 1, 256), kvwin_s.reshape(bs, 1, 256),
                                        win_t, pool_slc, l, past_len)
        dif_s = _diff_sample(page_table, dqrot_s.reshape(bs, 1, 512), kvdiff_s.reshape(bs, 1, 512),
                             dl, pool_diff, l, lam_init)
        scp = jnp.transpose(state_sconv[l], (1, 0, 2))
        sc_s, z_s, s5y_s, s5r_s, s5i_s = _sample_small(
            hs[:, C_SCB:C_SCB + 512], hs[:, C_SCC:C_SCC + 512], hs[:, C_SCH:C_SCH + 512], scw, scp,
            hs[:, C_S5U:C_S5U + 512], wb5, wc5, d5, tabs,
            state_s5_re[l].reshape(bs, n_state), state_s5_im[l].reshape(bs, n_state))
        x1s, x1sb = _mix(nsa_s.reshape(bs, 512), sc_s, dif_s.reshape(bs, 512), s5y_s, gw, gb, gain, wo,
                         xs, g1, b1, bs, l)
        prev_ffn = state_ffn_conv[l]
        xs, xs_b, aup_s = _ffn_sample(x1sb, x1s, prev_ffn[:, 0], prev_ffn[:, 1], wup, wdn, cwf, g2, b2, l)

        outs_s['cmp'].append(kvcmp_s.reshape(bs, 1, 2, 2, HEAD_DIM))
        outs_s['slc'].append(kvslc_s.reshape(bs, 1, 2, 2, HEAD_DIM))
        outs_s['win'].append(jnp.transpose(nwin_t, (0, 4, 1, 2, 3)))
        outs_s['diff'].append(kvdiff_s.reshape(bs, 1, 2, 2, 2 * HEAD_DIM))
        outs_s['sc'].append(jnp.stack([state_sconv[l][:, 1], z_s], axis=1))
        outs_s['s5r'].append(s5r_s.reshape(bs, S5_GROUPS, S5_STATE))
        outs_s['s5i'].append(s5i_s.reshape(bs, S5_GROUPS, S5_STATE))
        outs_s['ffn'].append(jnp.stack([prev_ffn[:, 1], aup_s], axis=1))

    order = ('cmp', 'slc', 'win', 'diff', 'sc', 's5r', 's5i', 'ffn')
    res = [xp.reshape(bp, t_len, D_MODEL), xs.reshape(bs, 1, D_MODEL)]
    res += [jnp.stack(outs_p[k], axis=0) for k in order]
    res += [jnp.stack(outs_s[k], axis=0) for k in order]
    return tuple(res)
```

```python
import functools
import math

import jax
import jax.numpy as jnp
from jax import lax
from jax.experimental import pallas as pl
from jax.experimental.pallas import tpu as pltpu

f32 = jnp.float32
bf16 = jnp.bfloat16

D_MODEL = 2048
DEPTH = 2
PAGE_SIZE = 128
HEAD_DIM = 64
GROUP_WIDTH = D_MODEL // 4
NSA_KV_HEADS = 2
NSA_GROUP = 4
CMP_STRIDE = 16
CMP_LEN = 32
SEL_BLOCK = 64
SEL_TOPK = 16
WINDOW = 512
FORCE_BONUS = 1e4
CONV_W = 3
S5_CH = 16
S5_GROUPS = 32
S5_STATE = 64
D_FF = 5632
ROPE_THETA = 10000.0
QBLOCK = 128
LN_EPS = 1e-5
RMS_EPS = 1e-6
NEG_INF = -1e30
DN_ALPHA = (2 * DEPTH) ** 0.25
SCALE = HEAD_DIM ** -0.5
LOG2E = math.log2(math.e)

IN_SPLITS = (512, 256, 256, 256, 24, 512, 512, 512, 512, 256, 256, 512)
C_NQ, C_SCB, C_SCC, C_SCH, C_DQ, C_S5U = 0, 512, 1024, 1536, 2048, 2560
C_CMP, C_SLC, C_WIN, C_DK, C_DV, C_GATE = 3072, 3328, 3584, 3840, 4096, 4352
HC = 4608
IN_TN = 768
FFN_TF = 512

VMEM_CAP_V7X = 64 * 1024 * 1024
VMEM_LIMIT = 56 * 1024 * 1024
NC_PAD = 640
S5_SLABS = 4
NSA_CLASS_BLOCKS = 4
DIFF_CLASS_BLOCKS = 2
FFN_CHUNK = 256


def _cp(sem):
    return pltpu.CompilerParams(dimension_semantics=sem, vmem_limit_bytes=VMEM_LIMIT)


def _dot(a, b):
    return jnp.dot(a, b, preferred_element_type=f32)


def _dot_nt(a, b):
    return lax.dot_general(a, b, (((1,), (1,)), ((), ())), preferred_element_type=f32)


def _lane(shape):
    return lax.broadcasted_iota(jnp.int32, shape, len(shape) - 1)


def _row(shape):
    return lax.broadcasted_iota(jnp.int32, shape, len(shape) - 2)


def _msoftmax(s, mask):
    s = jnp.where(mask, s, NEG_INF)
    m = jnp.max(s, axis=-1, keepdims=True)
    e = jnp.where(mask, jnp.exp(s - m), 0.0)
    return e * (1.0 / jnp.maximum(jnp.sum(e, axis=-1, keepdims=True), 1e-30))


def _exp2_softmax(s, bias):
    s = s + bias[None]
    e = jnp.exp2(s - jnp.max(s, axis=-1, keepdims=True))
    return e, 1.0 / jnp.maximum(jnp.sum(e, axis=-1, keepdims=True), 1e-30)


def _sigmoid(x):
    return 1.0 / (1.0 + jnp.exp(-x))


def _rms_unit(x):
    return x * lax.rsqrt(jnp.mean(x * x, axis=-1, keepdims=True) + RMS_EPS)


def _layer_norm(z, g, b):
    mu = jnp.mean(z, axis=-1, keepdims=True)
    d = z - mu
    var = jnp.mean(d * d, axis=-1, keepdims=True)
    return d * lax.rsqrt(var + LN_EPS) * g + b


def _split_hi_lo(x):
    hi = x.astype(bf16)
    lo = (x - hi.astype(f32)).astype(bf16)
    return hi, lo


def _matmul_kernel(x_ref, w_ref, o_ref):
    o_ref[...] = _dot(x_ref[...], w_ref[...])


def _in_proj(xb, w, tm):
    m = xb.shape[0]
    tn = IN_TN
    return pl.pallas_call(
        _matmul_kernel,
        grid=(m // tm, HC // tn),
        in_specs=[pl.BlockSpec((tm, D_MODEL), lambda i, j: (i, 0)),
                  pl.BlockSpec((None, D_MODEL, tn), lambda i, j: (j, 0, 0))],
        out_specs=pl.BlockSpec((tm, tn), lambda i, j: (i, j)),
        out_shape=jax.ShapeDtypeStruct((m, HC), f32),
        compiler_params=_cp(("parallel", "arbitrary")),
        name="in_proj",
    )(xb, w)


def _rope_cols(x, cos, sin):
    outs = []
    first = (_lane((1, 128)) % 64) < 32
    for c in range(x.shape[1] // 128):
        xc = x[:, c * 128:(c + 1) * 128]
        sw = jnp.where(first, pltpu.roll(xc, 96, 1), pltpu.roll(xc, 32, 1))
        outs.append(xc * cos + sw * sin)
    return outs


def _rope_kernel(nq_ref, dq_ref, slc_ref, win_ref, dk_ref, dv_ref, cos_ref, sin_ref,
                 qrot_ref, dqrot_ref, kvslc_ref, kvwin_ref, kvdiff_ref):
    cos = cos_ref[...]
    sin = sin_ref[...]
    for c, v in enumerate(_rope_cols(nq_ref[...], cos, sin)):
        qrot_ref[:, c * 128:(c + 1) * 128] = v
    for c, v in enumerate(_rope_cols(dq_ref[...], cos, sin)):
        dqrot_ref[:, c * 128:(c + 1) * 128] = v
    kvslc_ref[:, 0:128] = _rope_cols(slc_ref[:, 0:128], cos, sin)[0]
    kvslc_ref[:, 128:256] = slc_ref[:, 128:256]
    kvwin_ref[:, 0:128] = _rope_cols(win_ref[:, 0:128], cos, sin)[0]
    kvwin_ref[:, 128:256] = win_ref[:, 128:256]
    for c, v in enumerate(_rope_cols(dk_ref[...], cos, sin)):
        kvdiff_ref[:, c * 128:(c + 1) * 128] = v
    kvdiff_ref[:, 256:512] = dv_ref[...]


def _rope(hcat, cos, sin, tr):
    m = hcat.shape[0]
    nt = cos.shape[0] // tr

    def col(w, off):
        return pl.BlockSpec((tr, w), lambda i: (i, off // w))

    tab = pl.BlockSpec((tr, 128), lambda i: (i % nt, 0))
    return pl.pallas_call(
        _rope_kernel,
        grid=(m // tr,),
        in_specs=[col(512, C_NQ), col(512, C_DQ), col(256, C_SLC), col(256, C_WIN),
                  col(256, C_DK), col(256, C_DV), tab, tab],
        out_specs=[pl.BlockSpec((tr, 512), lambda i: (i, 0)),
                   pl.BlockSpec((tr, 512), lambda i: (i, 0)),
                   pl.BlockSpec((tr, 256), lambda i: (i, 0)),
                   pl.BlockSpec((tr, 256), lambda i: (i, 0)),
                   pl.BlockSpec((tr, 512), lambda i: (i, 0))],
        out_shape=[jax.ShapeDtypeStruct((m, 512), f32), jax.ShapeDtypeStruct((m, 512), f32),
                   jax.ShapeDtypeStruct((m, 256), f32), jax.ShapeDtypeStruct((m, 256), f32),
                   jax.ShapeDtypeStruct((m, 512), f32)],
        compiler_params=_cp(("parallel",)),
        name="rope",
    )(hcat, hcat, hcat, hcat, hcat, hcat, cos, sin)


def _cmp_prompt_kernel(z_ref, pe1_ref, pe2_ref, w1_ref, w2_ref, o_ref, sb_ref):
    z = z_ref[...]
    a = _dot((z + pe1_ref[...]).astype(bf16), w1_ref[...])
    bm = _dot((z + pe2_ref[...]).astype(bf16), w2_ref[...])
    n = z.shape[0]
    sb_ref[0:n, :] = bm
    sb_ref[n:n + 8, :] = jnp.zeros((8, 256), f32)
    o_ref[0] = a + sb_ref[1:n + 1, :]


def _cmp_prompt(z, pe1, pe2, w1, w2, bsz):
    n = z.shape[0] // bsz
    full = lambda shp: pl.BlockSpec(shp, lambda b: (0, 0))
    return pl.pallas_call(
        _cmp_prompt_kernel,
        grid=(bsz,),
        in_specs=[pl.BlockSpec((n, 4096), lambda b: (b, 0)), full((1, 4096)), full((1, 4096)),
                  full((4096, 256)), full((4096, 256))],
        out_specs=pl.BlockSpec((1, n, 256), lambda b: (b, 0, 0)),
        out_shape=jax.ShapeDtypeStruct((bsz, n, 256), f32),
        scratch_shapes=[pltpu.VMEM((n + 8, 256), f32)],
        compiler_params=_cp(("parallel",)),
        name="cmp_prompt",
    )(z, pe1, pe2, w1, w2)


def _nsa_qstack(blk, h):
    halfmask = (_lane((1, 128)) // 64) == h
    parts = []
    for g in range(4):
        c = blk[:, (g // 2) * 128:(g // 2 + 1) * 128]
        if g % 2 != h:
            c = pltpu.roll(c, 64, 1)
        parts.append(jnp.where(halfmask, c, 0.0))
    return jnp.concatenate(parts, axis=0)


def _nsa_assemble(o_list, h):
    lo = _lane((1, 128)) < 64
    chunks = []
    for gp in range(2):
        a, b = o_list[2 * gp], o_list[2 * gp + 1]
        if h == 0:
            b = pltpu.roll(b, 64, 1)
        else:
            a = pltpu.roll(a, 64, 1)
        chunks.append(jnp.where(lo, a, b))
    return chunks


def _nsa_prompt_body(qraw_ref, qrot_ref, gate_ref, kvc_ref, slc_ref, win_ref, cov_ref, exp_ref, out_ref,
                     *, s0, kmax, t_len):
    qb = QBLOCK
    qpos = s0 + _row((qb, 1))
    gs = _sigmoid(gate_ref[...])
    n_i = _lane((1, 128))
    maskc = ((16 * n_i + 31) <= qpos) & (n_i < 127)
    cov = cov_ref[...]
    expand = exp_ref[:, 0:kmax]
    causal = _lane((1, kmax)) <= qpos
    wlen = WINDOW + qb
    start = pl.multiple_of(jnp.clip(s0 - WINDOW, 0, t_len - wlen), 128)
    wpos = start + _lane((1, wlen))
    bias_w = jnp.where((wpos <= qpos) & ((qpos - wpos) < WINDOW), 0.0, NEG_INF)
    jj = _lane((1, 128))
    cur = qpos // SEL_BLOCK
    n_sel = kmax // SEL_BLOCK
    n_blk = t_len // SEL_BLOCK
    j_t = _row((n_blk, qb))
    forced = (jj == 0) | (jj == cur) | (jj == cur - 1)

    for h in range(NSA_KV_HEADS):
        kc = kvc_ref[0, :, 0:128].astype(bf16)
        vc = kvc_ref[0, :, 128:256].astype(bf16)
        qr = (_nsa_qstack(qraw_ref[:, h * 256:(h + 1) * 256], h) * SCALE).astype(bf16)
        qo = (_nsa_qstack(qrot_ref[:, h * 256:(h + 1) * 256], h) * (SCALE * LOG2E)).astype(bf16)
        s_c = _dot_nt(qr, kc).reshape(4, qb, 128)
        p_c = _msoftmax(s_c, maskc[None])
        o_cmp = _dot(p_c.reshape(4 * qb, 128).astype(bf16), vc)
        psum = p_c[0] + p_c[1] + p_c[2] + p_c[3]
        p_hi, p_lo = _split_hi_lo(psum)
        imp = _dot(p_hi, cov) + _dot(p_lo, cov)
        imp = jnp.where(forced, imp + FORCE_BONUS, imp)
        imp = jnp.where(jj <= cur, imp, -FORCE_BONUS)
        imp = jnp.where(jj < n_sel, imp, -3e38)
        imp_t = imp.T[0:n_blk]
        rank = jnp.zeros((n_blk, qb), f32)
        for k in range(n_sel):
            rk = imp_t[k:k + 1, :]
            beats = (rk > imp_t) | ((rk == imp_t) & (j_t > k))
            rank = rank + jnp.where(beats, 1.0, 0.0)
        sel_t = jnp.where((rank < SEL_TOPK) & (j_t < n_sel), 1.0, 0.0)
        sel = jnp.concatenate([sel_t, jnp.zeros((128 - n_blk, qb), f32)], axis=0).T.astype(bf16)
        bias_s = jnp.where((_dot(sel, expand) > 0.5) & causal, 0.0, NEG_INF)
        ks = slc_ref[0:kmax, 0:128].astype(bf16)
        vs = slc_ref[0:kmax, 128:256].astype(bf16)
        e_s, inv_s = _exp2_softmax(_dot_nt(qo, ks).reshape(4, qb, kmax), bias_s)
        o_slc = _dot(e_s.reshape(4 * qb, kmax).astype(bf16), vs) * inv_s.reshape(4 * qb, 1)
        kw = win_ref[pl.ds(start, wlen), 0:128].astype(bf16)
        vw = win_ref[pl.ds(start, wlen), 128:256].astype(bf16)
        e_w, inv_w = _exp2_softmax(_dot_nt(qo, kw).reshape(4, qb, wlen), bias_w)
        o_win = _dot(e_w.reshape(4 * qb, wlen).astype(bf16), vw) * inv_w.reshape(4 * qb, 1)
        o_list = []
        for g in range(NSA_GROUP):
            gi = (h * NSA_GROUP + g) * 3
            r = slice(g * qb, (g + 1) * qb)
            o_list.append(gs[:, gi:gi + 1] * o_cmp[r] + gs[:, gi + 1:gi + 2] * o_slc[r]
                          + gs[:, gi + 2:gi + 3] * o_win[r])
        for gp, ch in enumerate(_nsa_assemble(o_list, h)):
            out_ref[:, h * 256 + gp * 128:h * 256 + (gp + 1) * 128] = ch


def _by_key_class(body, t_len, blocks):
    i = pl.program_id(1)
    span = blocks * QBLOCK
    for c in range(t_len // span):
        @pl.when(i // blocks == c)
        def _(c=c):
            body(s0=i * QBLOCK, kmax=(c + 1) * span)


def _nsa_prompt_kernel(*refs, t_len):
    _by_key_class(functools.partial(_nsa_prompt_body, *refs, t_len=t_len), t_len, NSA_CLASS_BLOCKS)


def _cover_matrix(n_rows, n_cols, n_cmp, n_sel):
    n = jnp.arange(n_rows)[:, None]
    j = jnp.arange(n_cols)[None, :]
    cov = jnp.clip(jnp.minimum(CMP_STRIDE * n + CMP_LEN, SEL_BLOCK * (j + 1)) - jnp.maximum(CMP_STRIDE * n, SEL_BLOCK * j),
                   0, CMP_LEN)
    cov = jnp.where((n < n_cmp) & (j < n_sel), cov, 0).astype(f32) / CMP_LEN
    return cov.astype(bf16)


def _expand_matrix(n_rows, n_keys):
    return (jnp.arange(n_keys)[None, :] // SEL_BLOCK == jnp.arange(n_rows)[:, None]).astype(bf16)


def _nsa_prompt(hcat, qrot, kvc, kvslc, kvwin, bsz, t_len):
    m = hcat.shape[0]
    nqb = t_len // QBLOCK
    cov = _cover_matrix(128, 128, t_len // CMP_STRIDE - 1, t_len // SEL_BLOCK)
    expand = _expand_matrix(128, t_len)
    return pl.pallas_call(
        functools.partial(_nsa_prompt_kernel, t_len=t_len),
        grid=(bsz, nqb),
        in_specs=[pl.BlockSpec((QBLOCK, 512), lambda b, i: (b * nqb + i, 0)),
                  pl.BlockSpec((QBLOCK, 512), lambda b, i: (b * nqb + i, 0)),
                  pl.BlockSpec((QBLOCK, 128), lambda b, i: (b * nqb + i, C_GATE // 128)),
                  pl.BlockSpec((1, 128, 256), lambda b, i: (b, 0, 0)),
                  pl.BlockSpec((t_len, 256), lambda b, i: (b, 0)),
                  pl.BlockSpec((t_len, 256), lambda b, i: (b, 0)),
                  pl.BlockSpec((128, 128), lambda b, i: (0, 0)),
                  pl.BlockSpec((128, t_len), lambda b, i: (0, 0))],
        out_specs=pl.BlockSpec((QBLOCK, 512), lambda b, i: (b * nqb + i, 0)),
        out_shape=jax.ShapeDtypeStruct((m, 512), f32),
        compiler_params=_cp(("parallel", "arbitrary")),
        name="nsa_prompt",
    )(hcat, qrot, hcat, kvc, kvslc, kvwin, cov, expand)


def _diff_lambda(dl, lam_init):
    a = jnp.sum(dl[0:1, :] * dl[1:2, :], axis=-1, keepdims=True)
    b = jnp.sum(dl[2:3, :] * dl[3:4, :], axis=-1, keepdims=True)
    return jnp.exp(a) - jnp.exp(b) + lam_init


def _diff_prompt_body(dq_ref, kv_ref, dl_ref, out_ref, *, s0, kmax, lam_init):
    qb = QBLOCK
    qpos = s0 + _row((qb, 1))
    bias = jnp.where(_lane((1, kmax)) <= qpos, 0.0, NEG_INF)
    lam = _diff_lambda(dl_ref[...], lam_init)
    lane = _lane((1, 128))
    for h in range(2):
        k = kv_ref[0:kmax, h * 128:(h + 1) * 128].astype(bf16)
        v = kv_ref[0:kmax, 256 + h * 128:256 + (h + 1) * 128].astype(bf16)
        parts = []
        for g in range(2):
            c = dq_ref[:, h * 256 + g * 128:h * 256 + (g + 1) * 128] * (SCALE * LOG2E)
            for i in range(2):
                parts.append(jnp.where((lane // 64) == i, c, 0.0))
        q = jnp.concatenate(parts, axis=0).astype(bf16)
        e, inv = _exp2_softmax(_dot_nt(q, k).reshape(4, qb, kmax), bias)
        o = _dot(e.reshape(4 * qb, kmax).astype(bf16), v) * inv.reshape(4 * qb, 1)
        for g in range(2):
            og = o[2 * g * qb:(2 * g + 1) * qb] - lam * o[(2 * g + 1) * qb:(2 * g + 2) * qb]
            out_ref[:, (h * 2 + g) * 128:(h * 2 + g + 1) * 128] = _rms_unit(og) * (1.0 - lam_init)


def _diff_prompt_kernel(*refs, t_len, lam_init):
    _by_key_class(functools.partial(_diff_prompt_body, *refs, lam_init=lam_init), t_len, DIFF_CLASS_BLOCKS)


def _diff_prompt(dqrot, kvdiff, dl, bsz, t_len, lam_init):
    m = dqrot.shape[0]
    nqb = t_len // QBLOCK
    return pl.pallas_call(
        functools.partial(_diff_prompt_kernel, t_len=t_len, lam_init=lam_init),
        grid=(bsz, nqb),
        in_specs=[pl.BlockSpec((QBLOCK, 512), lambda b, i: (b * nqb + i, 0)),
                  pl.BlockSpec((t_len, 512), lambda b, i: (b, 0)),
                  pl.BlockSpec((4, 64), lambda b, i: (0, 0))],
        out_specs=pl.BlockSpec((QBLOCK, 512), lambda b, i: (b * nqb + i, 0)),
        out_shape=jax.ShapeDtypeStruct((m, 512), f32),
        compiler_params=_cp(("parallel", "arbitrary")),
        name="diff_prompt",
    )(dqrot, kvdiff, dl)


def _sconv_prompt_kernel(b_ref, c_ref, h_ref, w_ref, out_ref, tail_ref, buf_ref, *, tr):
    t = pl.program_id(1)

    @pl.when(t == 0)
    def _():
        buf_ref[0:8, :] = jnp.zeros((8, 512), f32)

    z = c_ref[...] * h_ref[...]
    buf_ref[8:8 + tr, :] = z
    w = w_ref[...]
    y = w[0:1] * buf_ref[6:6 + tr, :] + w[1:2] * buf_ref[7:7 + tr, :] + w[2:3] * z
    out_ref[...] = _rms_unit(b_ref[...] * y)
    tail = z[tr - 8:tr]
    tail_ref[0] = tail
    buf_ref[0:8, :] = tail


def _sconv_prompt(hcat, w, bsz, t_len):
    tr = 512
    nt = t_len // tr
    m = hcat.shape[0]

    def col(off):
        return pl.BlockSpec((tr, 512), lambda b, t: (b * nt + t, off // 512))

    return pl.pallas_call(
        functools.partial(_sconv_prompt_kernel, tr=tr),
        grid=(bsz, nt),
        in_specs=[col(C_SCB), col(C_SCC), col(C_SCH), pl.BlockSpec((3, 512), lambda b, t: (0, 0))],
        out_specs=[pl.BlockSpec((tr, 512), lambda b, t: (b * nt + t, 0)),
                   pl.BlockSpec((1, 8, 512), lambda b, t: (b, 0, 0))],
        out_shape=[jax.ShapeDtypeStruct((m, 512), f32), jax.ShapeDtypeStruct((bsz, 8, 512), f32)],
        scratch_shapes=[pltpu.VMEM((8 + tr, 512), f32)],
        compiler_params=_cp(("parallel", "arbitrary")),
        name="sconv_prompt",
    )(hcat, hcat, hcat, w)


def _cmul(ar, ai, br, bi):
    return ar * br - ai * bi, ar * bi + ai * br


def _s5_prep_kernel(ar_ref, ai_ref, ldt_ref, br_ref, bi_ref, tab_ref, bbr_ref, bbi_ref):
    ar, ai = ar_ref[...], ai_ref[...]
    dt = jnp.exp(ldt_ref[...])
    mag = jnp.exp(ar * dt)
    abr, abi = mag * jnp.cos(ai * dt), mag * jnp.sin(ai * dt)
    den = ar * ar + ai * ai
    nr, ni = abr - 1.0, abi
    cre = (nr * ar + ni * ai) / den
    cim = (ni * ar - nr * ai) / den
    br, bi = br_ref[...], bi_ref[...]
    bbr_ref[...] = cre * br - cim * bi
    bbi_ref[...] = cre * bi + cim * br
    pw = [(abr, abi)]
    for _ in range(7):
        pw.append(_cmul(pw[-1][0], pw[-1][1], abr, abi))
    n = ar.shape[1]
    row = _row((8, n))
    zero = jnp.zeros((8, n), f32)
    for idx, (sh, p) in enumerate(((1, pw[0]), (2, pw[1]), (4, pw[3]))):
        tab_ref[2 * idx] = jnp.where(row >= sh, jnp.broadcast_to(p[0], (8, n)), zero)
        tab_ref[2 * idx + 1] = jnp.where(row >= sh, jnp.broadcast_to(p[1], (8, n)), zero)
    pr, pi = zero, zero
    for i in range(8):
        pr = jnp.where(row == i, jnp.broadcast_to(pw[i][0], (8, n)), pr)
        pi = jnp.where(row == i, jnp.broadcast_to(pw[i][1], (8, n)), pi)
    tab_ref[6] = pr
    tab_ref[7] = pi


def _s5_prep(ar, ai, ldt, br, bi):
    n = S5_GROUPS * S5_STATE
    return pl.pallas_call(
        _s5_prep_kernel,
        out_shape=[jax.ShapeDtypeStruct((8, 8, n), f32), jax.ShapeDtypeStruct((S5_CH, n), f32),
                   jax.ShapeDtypeStruct((S5_CH, n), f32)],
        name="s5_prep",
    )(ar, ai, ldt, br, bi)


def _s5_scan_kernel(u_ref, wb_ref, wc_ref, d_ref, tab_ref, y_ref, hr_ref, hi_ref,
                    xbuf_ref, cr_ref, ci_ref, *, tt):
    t = pl.program_id(2)

    @pl.when(t == 0)
    def _():
        cr_ref[...] = jnp.zeros((8, 512), f32)
        ci_ref[...] = jnp.zeros((8, 512), f32)

    u = u_ref[...]
    xbuf_ref[...] = _dot(u.astype(bf16), wb_ref[0])

    def body(r, carry):
        cr, ci = carry
        rows = pl.ds(pl.multiple_of(r * 8, 8), 8)
        xr = xbuf_ref[rows, 0:512]
        xi = xbuf_ref[rows, 512:1024]
        for idx, sh in enumerate((1, 2, 4)):
            a_r, a_i = tab_ref[2 * idx], tab_ref[2 * idx + 1]
            sr, si = pltpu.roll(xr, sh, 0), pltpu.roll(xi, sh, 0)
            xr, xi = xr + a_r * sr - a_i * si, xi + a_r * si + a_i * sr
        p_r, p_i = tab_ref[6], tab_ref[7]
        hr = xr + p_r * cr - p_i * ci
        hi = xi + p_r * ci + p_i * cr
        xbuf_ref[rows, 0:512] = hr
        xbuf_ref[rows, 512:1024] = hi
        return (jnp.broadcast_to(hr[7:8, :], (8, 512)), jnp.broadcast_to(hi[7:8, :], (8, 512)))

    cr, ci = lax.fori_loop(0, tt // 8, body, (cr_ref[...], ci_ref[...]), unroll=4)
    cr_ref[...] = cr
    ci_ref[...] = ci
    hr_ref[0] = cr[0:1, :]
    hi_ref[0] = ci[0:1, :]
    y_ref[...] = _dot(xbuf_ref[...].astype(bf16), wc_ref[0]) + d_ref[...] * u


def _s5_scan(hcat, wb, wc, d, tabs, bsz, t_len):
    tt = 1024
    nt = t_len // tt
    m = hcat.shape[0]
    n = S5_GROUPS * S5_STATE
    return pl.pallas_call(
        functools.partial(_s5_scan_kernel, tt=tt),
        grid=(bsz, S5_SLABS, nt),
        in_specs=[pl.BlockSpec((tt, 128), lambda b, s, t: (b * nt + t, C_S5U // 128 + s)),
                  pl.BlockSpec((1, 128, 1024), lambda b, s, t: (s, 0, 0)),
                  pl.BlockSpec((1, 1024, 128), lambda b, s, t: (s, 0, 0)),
                  pl.BlockSpec((1, 128), lambda b, s, t: (0, s)),
                  pl.BlockSpec((8, 8, 512), lambda b, s, t: (0, 0, s))],
        out_specs=[pl.BlockSpec((tt, 128), lambda b, s, t: (b * nt + t, s)),
                   pl.BlockSpec((1, 1, 512), lambda b, s, t: (b, 0, s)),
                   pl.BlockSpec((1, 1, 512), lambda b, s, t: (b, 0, s))],
        out_shape=[jax.ShapeDtypeStruct((m, 512), f32), jax.ShapeDtypeStruct((bsz, 1, n), f32),
                   jax.ShapeDtypeStruct((bsz, 1, n), f32)],
        scratch_shapes=[pltpu.VMEM((tt, 1024), f32), pltpu.VMEM((8, 512), f32), pltpu.VMEM((8, 512), f32)],
        compiler_params=_cp(("parallel", "parallel", "arbitrary")),
        name="s5_scan",
    )(hcat, wb, wc, d, tabs)


def _sample_small_kernel(scb_ref, scc_ref, sch_ref, scw_ref, scp_ref, u_ref, wb_ref, wc_ref, d_ref,
                         tab_ref, h0r_ref, h0i_ref, sc_ref, z_ref, y_ref, hr_ref, hi_ref):
    z = scc_ref[...] * sch_ref[...]
    w = scw_ref[...]
    y = w[0:1] * scp_ref[0] + w[1:2] * scp_ref[1] + w[2:3] * z
    sc_ref[...] = _rms_unit(scb_ref[...] * y)
    z_ref[...] = z
    u = u_ref[...]
    for s in range(S5_SLABS):
        x = _dot(u[:, s * 128:(s + 1) * 128].astype(bf16), wb_ref[s])
        lanes = slice(s * 512, (s + 1) * 512)
        a_r, a_i = tab_ref[6, 0:1, lanes], tab_ref[7, 0:1, lanes]
        h0r, h0i = h0r_ref[:, lanes], h0i_ref[:, lanes]
        hr = a_r * h0r - a_i * h0i + x[:, 0:512]
        hi = a_r * h0i + a_i * h0r + x[:, 512:1024]
        hr_ref[:, lanes] = hr
        hi_ref[:, lanes] = hi
        hcat = jnp.concatenate([hr, hi], axis=1).astype(bf16)
        cols = slice(s * 128, (s + 1) * 128)
        y_ref[:, cols] = _dot(hcat, wc_ref[s]) + d_ref[:, cols] * u[:, cols]


def _sample_small(scb, scc, sch, scw, scp, u, wb, wc, d, tabs, h0r, h0i):
    bsz = u.shape[0]
    n = S5_GROUPS * S5_STATE
    return pl.pallas_call(
        _sample_small_kernel,
        out_shape=[jax.ShapeDtypeStruct((bsz, 512), f32), jax.ShapeDtypeStruct((bsz, 512), f32),
                   jax.ShapeDtypeStruct((bsz, 512), f32), jax.ShapeDtypeStruct((bsz, n), f32),
                   jax.ShapeDtypeStruct((bsz, n), f32)],
        compiler_params=pltpu.CompilerParams(vmem_limit_bytes=VMEM_LIMIT),
        name="sample_small",
    )(scb, scc, sch, scw, scp, u, wb, wc, d, tabs, h0r, h0i)


def _gelu(x):
    return 0.5 * x * (1.0 + jnp.tanh(math.sqrt(2.0 / math.pi) * (x + 0.044715 * (x * x * x))))


def _mix_kernel(nsa_ref, sc_ref, diff_ref, s5_ref, gw_ref, gb_ref, gain_ref, wo_ref, x_ref,
                g_ref, b_ref, out_ref, outb_ref):
    y = _gelu(s5_ref[...])
    s5o = y * _sigmoid(_dot(y.astype(bf16), gw_ref[...]) + gb_ref[...])
    parts = (_rms_unit(nsa_ref[...]), sc_ref[...], diff_ref[...], _rms_unit(s5o))
    mixed = jnp.concatenate([(p * gain_ref[:, k * 512:(k + 1) * 512]).astype(bf16) for k, p in enumerate(parts)],
                            axis=1)
    o = _layer_norm(DN_ALPHA * x_ref[...] + _dot(mixed, wo_ref[...]), g_ref[...], b_ref[...])
    out_ref[...] = o
    outb_ref[...] = o.astype(bf16)


def _mix(nsa, sc, diff, s5y, gw, gb, gain, wo, x, g, b, tm, layer):
    m = x.shape[0]
    row = lambda w: pl.BlockSpec((tm, w), lambda i: (i, 0))
    full = lambda shp: pl.BlockSpec(shp, lambda i: (0, 0))
    stacked = lambda shp: pl.BlockSpec((None,) + shp, lambda i: (layer, 0, 0))
    return pl.pallas_call(
        _mix_kernel,
        grid=(m // tm,),
        in_specs=[row(512), row(512), row(512), row(512), stacked((512, 512)), full((1, 512)),
                  full((1, D_MODEL)), stacked((D_MODEL, D_MODEL)), row(D_MODEL), full((1, D_MODEL)),
                  full((1, D_MODEL))],
        out_specs=[row(D_MODEL), row(D_MODEL)],
        out_shape=[jax.ShapeDtypeStruct((m, D_MODEL), f32), jax.ShapeDtypeStruct((m, D_MODEL), bf16)],
        compiler_params=_cp(("parallel",)),
        name="mix_outproj_ln",
    )(nsa, sc, diff, s5y, gw, gb, gain, wo, x, g, b)


def _ffn_tail(acc_ref, xres_ref, g_ref, b_ref, out_ref, outb_ref):
    o = _layer_norm(DN_ALPHA * xres_ref[...] + acc_ref[...], g_ref[...], b_ref[...])
    out_ref[...] = o
    outb_ref[...] = o.astype(bf16)


def _ffn_prompt_kernel(x_ref, halo_ref, wa_ref, wb_ref, wd_ref, cw_ref, xres_ref, g_ref, b_ref,
                       out_ref, outb_ref, tail_ref, acc_ref, abuf_ref, *, tm, nf, tiles_per_seq):
    i = pl.program_id(0)
    f = pl.program_id(1)

    @pl.when(f == 0)
    def _():
        acc_ref[...] = jnp.zeros_like(acc_ref)

    x = x_ref[...]
    halo = halo_ref[...]
    keep = jnp.where(i % tiles_per_seq != 0, 1.0, 0.0)
    cw = cw_ref[...]
    down = None
    for c in range(abuf_ref.shape[0]):
        cols = slice(c * FFN_CHUNK, (c + 1) * FFN_CHUNK)
        a = _dot(x, wa_ref[:, cols])
        bb = _dot(x, wb_ref[:, cols])
        abuf_ref[c, 0:16, :] = _dot(halo, wa_ref[:, cols]) * keep
        abuf_ref[c, 16:16 + tm, :] = a
        ac = (cw[0:1, cols] * abuf_ref[c, 14:14 + tm, :] + cw[1:2, cols] * abuf_ref[c, 15:15 + tm, :]
              + cw[2:3, cols] * a)
        gate = (ac * _sigmoid(ac) * bb).astype(bf16)
        d = _dot(gate, wd_ref[cols, :])
        down = d if down is None else down + d
        tail_ref[0, :, cols] = a[tm - 8:tm]
    acc_ref[...] += down

    @pl.when(f == nf - 1)
    def _():
        _ffn_tail(acc_ref, xres_ref, g_ref, b_ref, out_ref, outb_ref)


def _ffn_prompt(xb, x, wup, wd, cw, g, b, t_len, layer):
    m = x.shape[0]
    tm, tf = 512, FFN_TF
    nf = D_FF // tf
    full = lambda shp: pl.BlockSpec(shp, lambda i, f: (0, 0))
    return pl.pallas_call(
        functools.partial(_ffn_prompt_kernel, tm=tm, nf=nf, tiles_per_seq=t_len // tm),
        grid=(m // tm, nf),
        in_specs=[pl.BlockSpec((tm, D_MODEL), lambda i, f: (i, 0)),
                  pl.BlockSpec((16, D_MODEL), lambda i, f: (jnp.maximum(i * (tm // 16) - 1, 0), 0)),
                  pl.BlockSpec((None, D_MODEL, tf), lambda i, f: (layer, 0, f)),
                  pl.BlockSpec((None, D_MODEL, tf), lambda i, f: (layer, 0, f + nf)),
                  pl.BlockSpec((None, tf, D_MODEL), lambda i, f: (layer, f, 0)),
                  pl.BlockSpec((3, tf), lambda i, f: (0, f)),
                  pl.BlockSpec((tm, D_MODEL), lambda i, f: (i, 0)),
                  full((1, D_MODEL)), full((1, D_MODEL))],
        out_specs=[pl.BlockSpec((tm, D_MODEL), lambda i, f: (i, 0)),
                   pl.BlockSpec((tm, D_MODEL), lambda i, f: (i, 0)),
                   pl.BlockSpec((1, 8, tf), lambda i, f: (i, 0, f))],
        out_shape=[jax.ShapeDtypeStruct((m, D_MODEL), f32), jax.ShapeDtypeStruct((m, D_MODEL), bf16),
                   jax.ShapeDtypeStruct((m // tm, 8, D_FF), f32)],
        scratch_shapes=[pltpu.VMEM((tm, D_MODEL), f32), pltpu.VMEM((tf // FFN_CHUNK, 16 + tm, FFN_CHUNK), f32)],
        compiler_params=_cp(("parallel", "arbitrary")),
        name="ffn_prompt",
    )(xb, xb, wup, wup, wd, cw, x, g, b)


def _ffn_sample_kernel(x_ref, p0_ref, p1_ref, wa_ref, wb_ref, wd_ref, cw_ref, xres_ref, g_ref, b_ref,
                       out_ref, outb_ref, aup_ref, acc_ref, *, nf):
    f = pl.program_id(0)

    @pl.when(f == 0)
    def _():
        acc_ref[...] = jnp.zeros_like(acc_ref)

    x = x_ref[...]
    a = _dot(x, wa_ref[...])
    bb = _dot(x, wb_ref[...])
    cw = cw_ref[...]
    ac = cw[0:1] * p0_ref[...] + cw[1:2] * p1_ref[...] + cw[2:3] * a
    gate = (ac * _sigmoid(ac) * bb).astype(bf16)
    acc_ref[...] += _dot(gate, wd_ref[...])
    aup_ref[...] = a

    @pl.when(f == nf - 1)
    def _():
        _ffn_tail(acc_ref, xres_ref, g_ref, b_ref, out_ref, outb_ref)


def _ffn_sample(xb, x, p0, p1, wup, wd, cw, g, b, layer):
    m = x.shape[0]
    tf = FFN_TF
    nf = D_FF // tf
    full = lambda shp: pl.BlockSpec(shp, lambda f: (0, 0))
    return pl.pallas_call(
        functools.partial(_ffn_sample_kernel, nf=nf),
        grid=(nf,),
        in_specs=[full((m, D_MODEL)), pl.BlockSpec((m, tf), lambda f: (0, f)),
                  pl.BlockSpec((m, tf), lambda f: (0, f)),
                  pl.BlockSpec((None, D_MODEL, tf), lambda f: (layer, 0, f)),
                  pl.BlockSpec((None, D_MODEL, tf), lambda f: (layer, 0, f + nf)),
                  pl.BlockSpec((None, tf, D_MODEL), lambda f: (layer, f, 0)),
                  pl.BlockSpec((3, tf), lambda f: (0, f)),
                  full((m, D_MODEL)), full((1, D_MODEL)), full((1, D_MODEL))],
        out_specs=[full((m, D_MODEL)), full((m, D_MODEL)), pl.BlockSpec((m, tf), lambda f: (0, f))],
        out_shape=[jax.ShapeDtypeStruct((m, D_MODEL), f32), jax.ShapeDtypeStruct((m, D_MODEL), bf16),
                   jax.ShapeDtypeStruct((m, D_FF), f32)],
        scratch_shapes=[pltpu.VMEM((m, D_MODEL), f32)],
        compiler_params=_cp(("arbitrary",)),
        name="ffn_sample",
    )(xb, p0, p1, wup, wup, wd, cw, x, g, b)


def _cmp_sel_sample_kernel(pt_ref, new_ref, qraw_ref, pe1_ref, pe2_ref, w1_ref, w2_ref, cov_ref, pool_ref,
                           ocmp_ref, idx_ref, buf_ref, x_ref, sb_ref, kvc_ref, cst_ref, sem_ref,
                           *, layer, n_pages, n_batch, past_len):
    b = pl.program_id(0)
    slot = b % 2
    n = n_pages * (PAGE_SIZE // CMP_STRIDE)

    def copy(bb, sl, p):
        return pltpu.make_async_copy(pool_ref.at[layer, pt_ref[bb, p]], buf_ref.at[sl, p], sem_ref.at[sl])

    def fetch(bb, sl):
        for p in range(n_pages):
            copy(bb, sl, p).start()

    @pl.when(b == 0)
    def _():
        fetch(0, 0)
        for idx, (pe_ref, w_ref) in enumerate(((pe1_ref, w1_ref), (pe2_ref, w2_ref))):
            tot = None
            for s in range(CMP_STRIDE):
                pes = jnp.broadcast_to(pe_ref[:, s * 256:(s + 1) * 256], (8, 256)).astype(bf16)
                d = _dot(pes, w_ref[s * 256:(s + 1) * 256, :])
                tot = d if tot is None else tot + d
            cst_ref[idx] = tot

    @pl.when(b + 1 < n_batch)
    def _():
        fetch(b + 1, 1 - slot)

    for p in range(n_pages):
        copy(b, slot, p).wait()

    r_i = _row((PAGE_SIZE, PAGE_SIZE))
    pick = jnp.where(_lane((PAGE_SIZE, PAGE_SIZE)) == CMP_STRIDE * (r_i % 8) + r_i // 8, 1.0, 0.0).astype(bf16)

    def regroup(q, carry):
        rows = pl.ds(pl.multiple_of(q * 16, 16), 16)
        r0 = _dot_nt(pick, buf_ref[slot, 2 * q].reshape(256, PAGE_SIZE).astype(bf16))
        r1 = _dot_nt(pick, buf_ref[slot, 2 * q + 1].reshape(256, PAGE_SIZE).astype(bf16))
        for s in range(CMP_STRIDE):
            pair = jnp.concatenate([r0[s * 8:(s + 1) * 8], r1[s * 8:(s + 1) * 8]], axis=0)
            x_ref[rows, s * 256:(s + 1) * 256] = pair.astype(bf16)
        return carry

    lax.fori_loop(0, n_pages // 2, regroup, 0, unroll=4)

    xb = x_ref[...]
    a = _dot(xb, w1_ref[...]) + cst_ref[0, 0:1, :]
    bm = _dot(xb, w2_ref[...]) + cst_ref[1, 0:1, :]
    new8 = jnp.broadcast_to(new_ref[0], (8, 256)).astype(bf16)
    row0 = _row((8, 256)) == 0
    a_new = cst_ref[0] + jnp.where(row0, _dot(new8, w1_ref[0:256, :]), 0.0)
    b_new = cst_ref[1] + jnp.where(row0, _dot(new8, w2_ref[0:256, :]), 0.0)
    sb_ref[0:n, :] = bm
    sb_ref[n:n + 8, :] = b_new
    sb_ref[n + 8:n + 16, :] = jnp.zeros((8, 256), f32)
    kvc_ref[0:n, :] = a + sb_ref[1:n + 1, :]
    kvc_ref[n:n + 8, :] = a_new + sb_ref[n + 1:n + 9, :]
    kvc_ref[n + 8:NC_PAD, :] = jnp.zeros((NC_PAD - n - 8, 256), f32)

    qpos = past_len
    n_sel = past_len // SEL_BLOCK + 1
    n_cmp = n_sel * SEL_BLOCK // CMP_STRIDE - 1
    cur = qpos // SEL_BLOCK
    n_i = _lane((1, NC_PAD))
    maskc = ((16 * n_i + 31) <= qpos) & (n_i < n_cmp)
    cov = cov_ref[...]
    jj = _lane((1, 256))
    forced = (jj == 0) | (jj == cur) | (jj == cur - 1)
    kk = _row((256, 256))
    jjm = _lane((256, 256))
    eye = kk == jjm
    before = jnp.where(kk < jjm, 1.0, 0.0).astype(bf16)
    slot_id = _row((SEL_TOPK, 256))
    jj16 = _lane((SEL_TOPK, 256)).astype(f32)
    qraw = qraw_ref[0]
    for h in range(NSA_KV_HEADS):
        kc = kvc_ref[:, 0:128].astype(bf16)
        vc = kvc_ref[:, 128:256].astype(bf16)
        qr = (_nsa_q8(qraw, h, h) * SCALE).astype(bf16)
        p_c = _msoftmax(_dot_nt(qr, kc), maskc)
        ocmp_ref[0, h] = _dot(p_c.astype(bf16), vc)
        psum = jnp.broadcast_to(p_c[0:1] + p_c[1:2] + p_c[2:3] + p_c[3:4], (8, NC_PAD))
        p_hi, p_lo = _split_hi_lo(psum)
        imp = (_dot(p_hi, cov) + _dot(p_lo, cov))[0:1]
        imp = jnp.where(forced, imp + FORCE_BONUS, imp)
        imp = jnp.where(jj <= cur, imp, -FORCE_BONUS)
        imp = jnp.where(jj < n_sel, imp, -3e38)
        imp_j = jnp.broadcast_to(imp, (256, 256))
        imp_k = jnp.broadcast_to(jnp.sum(jnp.where(eye, imp_j, 0.0), axis=1, keepdims=True), (256, 256))
        beats = (imp_k > imp_j) | ((imp_k == imp_j) & (kk < jjm))
        rank = jnp.sum(jnp.where(beats, 1.0, 0.0), axis=0, keepdims=True)
        sel = jnp.where((rank < SEL_TOPK) & (jj < n_sel), 1.0, 0.0)
        pos = _dot(jnp.broadcast_to(sel, (8, 256)).astype(bf16), before)[0:1]
        hit = (jnp.broadcast_to(pos, (SEL_TOPK, 256)) == slot_id.astype(f32)) & (jnp.broadcast_to(sel, (SEL_TOPK, 256)) > 0.5)
        ids = jnp.sum(jnp.where(hit, jj16, 0.0), axis=1, keepdims=True)
        idx_ref[0, h * SEL_TOPK:(h + 1) * SEL_TOPK, :] = jnp.broadcast_to(ids, (SEL_TOPK, 128)).astype(jnp.int32)


def _cmp_sel_sample(page_table, new_rows, qraw, pe1, pe2, w1, w2, pool_t, layer, past_len):
    n_batch, n_pages = page_table.shape
    n = n_pages * (PAGE_SIZE // CMP_STRIDE)
    n_sel = past_len // SEL_BLOCK + 1
    cov = _cover_matrix(NC_PAD, 256, n_sel * SEL_BLOCK // CMP_STRIDE - 1, n_sel)
    full = lambda shp: pl.BlockSpec(shp, lambda b, pt: (0,) * len(shp))
    grid_spec = pltpu.PrefetchScalarGridSpec(
        num_scalar_prefetch=1,
        grid=(n_batch,),
        in_specs=[pl.BlockSpec((1, 1, 256), lambda b, pt: (b, 0, 0)),
                  pl.BlockSpec((1, 1, 512), lambda b, pt: (b, 0, 0)), full((1, 4096)), full((1, 4096)),
                  full((4096, 256)), full((4096, 256)), full((NC_PAD, 256)), pl.BlockSpec(memory_space=pl.ANY)],
        out_specs=[pl.BlockSpec((1, 2, 8, 128), lambda b, pt: (b, 0, 0, 0)),
                   pl.BlockSpec((1, 2 * SEL_TOPK, 128), lambda b, pt: (b, 0, 0))],
        scratch_shapes=[pltpu.VMEM((2, n_pages, 2, 2, HEAD_DIM, PAGE_SIZE), f32),
                        pltpu.VMEM((n, CMP_STRIDE * 256), bf16), pltpu.VMEM((n + 16, 256), f32),
                        pltpu.VMEM((NC_PAD, 256), f32), pltpu.VMEM((2, 8, 256), f32),
                        pltpu.SemaphoreType.DMA((2,))],
    )
    return pl.pallas_call(
        functools.partial(_cmp_sel_sample_kernel, layer=layer, n_pages=n_pages, n_batch=n_batch,
                          past_len=past_len),
        grid_spec=grid_spec,
        out_shape=[jax.ShapeDtypeStruct((n_batch, 2, 8, 128), f32),
                   jax.ShapeDtypeStruct((n_batch, 2 * SEL_TOPK, 128), jnp.int32)],
        compiler_params=_cp(("arbitrary",)),
        name="cmp_sel_sample",
    )(page_table, new_rows, qraw, pe1, pe2, w1, w2, cov, pool_t)


def _rows8(row_chunks):
    rid = _row((8, 128))
    out = jnp.zeros((8, 128), f32)
    for r, c in enumerate(row_chunks):
        out = jnp.where(rid == r, jnp.broadcast_to(c, (8, 128)), out)
    return out


def _nsa_q8(qrow, h, half):
    chunks = []
    for g in range(4):
        hd = h * 4 + g
        chunks.append(qrow[:, (hd // 2) * 128:(hd // 2 + 1) * 128])
    q8 = _rows8(chunks)
    sw = pltpu.roll(q8, 64, 1)
    in_place = (_row((8, 128)) % 2) == half
    q8 = jnp.where(in_place, q8, sw)
    return jnp.where((_lane((8, 128)) // 64) == half, q8, 0.0)


def _nsa_sel_sample_kernel(pt_ref, idx_ref, qrot_ref, gate_ref, ocmp_ref, snew_ref, wnew_ref, win_ref,
                           pool_ref, out_ref, nwin_ref, buf_ref, sem_ref, *, layer, n_batch, n_past_blocks):
    b = pl.program_id(0)
    slot = b % 2

    def copy(bb, sl, h, s, kv):
        j = jnp.minimum(idx_ref[bb, h * SEL_TOPK + s], n_past_blocks - 1)
        page = pt_ref[bb, j // (PAGE_SIZE // SEL_BLOCK)]
        return pltpu.make_async_copy(pool_ref.at[layer, page, kv], buf_ref.at[sl, h, kv, s], sem_ref.at[sl])

    def for_all(bb, sl, fn):
        for h in range(NSA_KV_HEADS):
            for s in range(SEL_TOPK):
                for kv in range(2):
                    fn(copy(bb, sl, h, s, kv))

    @pl.when(b == 0)
    def _():
        for_all(0, 0, lambda c: c.start())

    @pl.when(b + 1 < n_batch)
    def _():
        for_all(b + 1, 1 - slot, lambda c: c.start())

    gs = _sigmoid(gate_ref[0])
    lane128 = _lane((1, 128))
    qrot = qrot_ref[0]

    for_all(b, slot, lambda c: c.wait())

    snew = snew_ref[0]
    wnew = wnew_ref[0]
    o_all = []
    for h in range(NSA_KV_HEADS):
        qo = _nsa_q8(qrot, h, h) * SCALE
        qob = qo.astype(bf16)
        halfmask = (lane128 // 64) == h
        scores, valids = [], []
        has_new = False
        for s in range(SEL_TOPK):
            j = idx_ref[b, h * SEL_TOPK + s]
            kt = buf_ref[slot, h, 0, s].reshape(128, 128).astype(bf16)
            valid = ((lane128 // SEL_BLOCK) == (j % (PAGE_SIZE // SEL_BLOCK))) & (j < n_past_blocks)
            scores.append(jnp.where(valid, _dot(qob, kt), NEG_INF))
            valids.append(valid)
            has_new = jnp.logical_or(has_new, j == n_past_blocks)
        s_new = jnp.sum(qo * snew[:, 0:128], axis=-1, keepdims=True)
        s_new = jnp.where(has_new, s_new, NEG_INF)
        smax = scores[0]
        for sc in scores[1:]:
            smax = jnp.maximum(smax, sc)
        mx = jnp.maximum(jnp.max(smax, axis=-1, keepdims=True), s_new)
        e_new = jnp.where(has_new, jnp.exp(s_new - mx), 0.0)
        esum = jnp.zeros((8, 128), f32)
        acc = jnp.zeros((8, 128), f32)
        for s in range(SEL_TOPK):
            e = jnp.where(valids[s], jnp.exp(scores[s] - mx), 0.0)
            esum = esum + e
            vt = buf_ref[slot, h, 1, s].reshape(128, 128).astype(bf16)
            acc = acc + _dot_nt(e.astype(bf16), vt)
        inv = 1.0 / jnp.maximum(jnp.sum(esum, axis=-1, keepdims=True) + e_new, 1e-30)
        o_slc = (acc + e_new * snew[:, 128:256]) * inv
        wt = win_ref[0, 0]
        kt = wt[0].reshape(128, WINDOW).astype(bf16)
        vt = wt[1].reshape(128, WINDOW).astype(bf16)
        maskw = _lane((1, WINDOW)) >= 1
        s_w = jnp.where(maskw, _dot(qob, kt), NEG_INF)
        sw_new = jnp.sum(qo * wnew[:, 0:128], axis=-1, keepdims=True)
        mx = jnp.maximum(jnp.max(s_w, axis=-1, keepdims=True), sw_new)
        e = jnp.where(maskw, jnp.exp(s_w - mx), 0.0)
        e_new = jnp.exp(sw_new - mx)
        inv = 1.0 / jnp.maximum(jnp.sum(e, axis=-1, keepdims=True) + e_new, 1e-30)
        o_win = (_dot_nt(e.astype(bf16), vt) + e_new * wnew[:, 128:256]) * inv
        gate_rows = []
        for c in range(3):
            gate_rows.append(_rows8([jnp.broadcast_to(gs[:, (h * 4 + g) * 3 + c:(h * 4 + g) * 3 + c + 1], (1, 128))
                                     for g in range(4)]))
        o8 = gate_rows[0] * ocmp_ref[0, h] + gate_rows[1] * o_slc + gate_rows[2] * o_win
        o_all.append(jnp.where(halfmask, o8, 0.0))

    lo = lane128 < 64
    for h in range(NSA_KV_HEADS):
        o8 = o_all[h]
        o8s = pltpu.roll(o8, 64, 1)
        low_src, high_src = (o8, o8s) if h == 0 else (o8s, o8)
        for gp in range(2):
            ch = jnp.where(lo, low_src[2 * gp:2 * gp + 1], high_src[2 * gp + 1:2 * gp + 2])
            out_ref[0, :, h * 256 + gp * 128:h * 256 + (gp + 1) * 128] = ch

    last = _lane((1, WINDOW)) == WINDOW - 1
    eye64 = _row((64, 64)) == _lane((64, 64))
    for kv in range(2):
        for h in range(NSA_KV_HEADS):
            c = kv * 2 + h
            newc = jnp.broadcast_to(wnew[:, c * 64:(c + 1) * 64], (64, 64))
            colv = jnp.sum(jnp.where(eye64, newc, 0.0), axis=1, keepdims=True)
            old = win_ref[0, 0, kv, h]
            nwin_ref[0, kv, h] = jnp.where(last, colv, pltpu.roll(old, WINDOW - 1, 1))


def _nsa_sel_sample(page_table, sel_idx, qrot, gates, ocmp, snew, wnew, win_t, pool_t, layer, past_len):
    n_batch = page_table.shape[0]
    row = lambda w: pl.BlockSpec((1, 1, w), lambda b, pt, ix: (b, 0, 0))
    grid_spec = pltpu.PrefetchScalarGridSpec(
        num_scalar_prefetch=2,
        grid=(n_batch,),
        in_specs=[row(512), row(128), pl.BlockSpec((1, 2, 8, 128), lambda b, pt, ix: (b, 0, 0, 0)),
                  row(256), row(256),
                  pl.BlockSpec((1, 1, 2, 2, HEAD_DIM, WINDOW), lambda b, pt, ix: (layer, b, 0, 0, 0, 0)),
                  pl.BlockSpec(memory_space=pl.ANY)],
        out_specs=[row(512), pl.BlockSpec((1, 2, 2, HEAD_DIM, WINDOW), lambda b, pt, ix: (b, 0, 0, 0, 0))],
        scratch_shapes=[pltpu.VMEM((2, NSA_KV_HEADS, 2, SEL_TOPK, 2, HEAD_DIM, PAGE_SIZE), f32),
                        pltpu.SemaphoreType.DMA((2,))],
    )
    return pl.pallas_call(
        functools.partial(_nsa_sel_sample_kernel, layer=layer, n_batch=n_batch,
                          n_past_blocks=past_len // SEL_BLOCK),
        grid_spec=grid_spec,
        out_shape=[jax.ShapeDtypeStruct((n_batch, 1, 512), f32),
                   jax.ShapeDtypeStruct((n_batch, 2, 2, HEAD_DIM, WINDOW), f32)],
        compiler_params=_cp(("arbitrary",)),
        name="nsa_sel_sample",
    )(page_table, sel_idx, qrot, gates, ocmp, snew, wnew, win_t, pool_t)


def _diff_sample_kernel(pt_ref, q_ref, new_ref, dl_ref, pool_ref, out_ref, buf_ref, m_ref, l_ref,
                        acc_ref, sem_ref, *, layer, n_pages, n_batch, n_split, lam_init):
    b = pl.program_id(0)
    hf = pl.program_id(1)
    step = b * n_split + hf
    slot = step % 2
    pps = n_pages // n_split
    rows = pps * PAGE_SIZE

    def copy(bb, hh, sl, p):
        return pltpu.make_async_copy(pool_ref.at[layer, pt_ref[bb, hh * pps + p]],
                                     buf_ref.at[sl, pl.ds(p * 4 * PAGE_SIZE, 4 * PAGE_SIZE), :],
                                     sem_ref.at[sl])

    def fetch(bb, hh, sl):
        for p in range(pps):
            copy(bb, hh, sl, p).start()

    @pl.when(step == 0)
    def _():
        fetch(0, 0, 0)

    @pl.when(step + 1 < n_batch * n_split)
    def _():
        nxt = step + 1
        fetch(nxt // n_split, nxt % n_split, 1 - slot)

    @pl.when(hf == 0)
    def _():
        m_ref[...] = jnp.full(m_ref.shape, NEG_INF, f32)
        l_ref[...] = jnp.zeros(l_ref.shape, f32)
        acc_ref[...] = jnp.zeros(acc_ref.shape, f32)

    for p in range(pps):
        copy(b, hf, slot, p).wait()

    qrow = q_ref[0]
    lane = _lane((8, 128))
    rid = _row((8, 128))
    q8s = []
    for h in range(2):
        q8 = _rows8([qrow[:, h * 256 + (r // 2) * 128:h * 256 + (r // 2 + 1) * 128] for r in range(4)])
        q8s.append(jnp.where((lane // 64) == (rid % 2), q8, 0.0) * SCALE)
    for h in range(2):
        k = buf_ref[slot, pl.ds(h, rows, stride=4), :].astype(bf16)
        v = buf_ref[slot, pl.ds(2 + h, rows, stride=4), :].astype(bf16)
        s = _dot_nt(q8s[h].astype(bf16), k)
        m_old = m_ref[h]
        m_new = jnp.maximum(m_old, jnp.max(s, axis=-1, keepdims=True))
        alpha = jnp.exp(m_old - m_new)
        p_ = jnp.exp(s - m_new)
        l_ref[h] = alpha * l_ref[h] + jnp.sum(p_, axis=-1, keepdims=True)
        acc_ref[h] = alpha * acc_ref[h] + _dot(p_.astype(bf16), v)
        m_ref[h] = m_new

    @pl.when(hf == n_split - 1)
    def _():
        lam = _diff_lambda(dl_ref[...], lam_init)
        new = new_ref[0]
        for h in range(2):
            s_new = jnp.sum(q8s[h] * new[:, h * 128:(h + 1) * 128], axis=-1, keepdims=True)
            m_old = m_ref[h]
            m_new = jnp.maximum(m_old, s_new)
            alpha = jnp.exp(m_old - m_new)
            p_new = jnp.exp(s_new - m_new)
            l_ = alpha * l_ref[h] + p_new
            acc = alpha * acc_ref[h] + p_new * new[:, 256 + h * 128:256 + (h + 1) * 128]
            o = acc * (1.0 / jnp.maximum(l_, 1e-30))
            for g in range(2):
                og = o[2 * g:2 * g + 1] - lam * o[2 * g + 1:2 * g + 2]
                out_ref[0, :, (h * 2 + g) * 128:(h * 2 + g + 1) * 128] = _rms_unit(og) * (1.0 - lam_init)


def _diff_sample(page_table, dqrot, new_rows, dl, pool, layer, lam_init):
    n_batch, n_pages = page_table.shape
    n_split = 2
    pps = n_pages // n_split
    row = lambda w: pl.BlockSpec((1, 1, w), lambda b, s, pt: (b, 0, 0))
    grid_spec = pltpu.PrefetchScalarGridSpec(
        num_scalar_prefetch=1,
        grid=(n_batch, n_split),
        in_specs=[row(512), row(512), pl.BlockSpec((4, 64), lambda b, s, pt: (0, 0)),
                  pl.BlockSpec(memory_space=pl.ANY)],
        out_specs=row(512),
        scratch_shapes=[pltpu.VMEM((2, pps * 4 * PAGE_SIZE, 128), f32), pltpu.VMEM((2, 8, 1), f32),
                        pltpu.VMEM((2, 8, 1), f32), pltpu.VMEM((2, 8, 128), f32),
                        pltpu.SemaphoreType.DMA((2,))],
    )
    return pl.pallas_call(
        functools.partial(_diff_sample_kernel, layer=layer, n_pages=n_pages, n_batch=n_batch,
                          n_split=n_split, lam_init=lam_init),
        grid_spec=grid_spec,
        out_shape=jax.ShapeDtypeStruct((n_batch, 1, 512), f32),
        compiler_params=_cp(("arbitrary", "arbitrary")),
        name="diff_sample",
    )(page_table, dqrot, new_rows, dl, pool)


def _prep_w_in(w):
    parts = jnp.split(w, [sum(IN_SPLITS[:i + 1]) for i in range(len(IN_SPLITS) - 1)], axis=-1)
    nq, ncmp, nslc, nwin, ngate, scb, scc, sch, dq, dk, dv, s5u = parts
    gate = jnp.pad(ngate, ((0, 0), (0, HC - C_GATE - ngate.shape[1])))
    wcat = jnp.concatenate([nq, scb, scc, sch, dq, s5u, ncmp, nslc, nwin, dk, dv, gate], axis=-1).astype(bf16)
    return jnp.transpose(wcat.reshape(D_MODEL, HC // IN_TN, IN_TN), (1, 0, 2))


def _prep_phi(pe, w):
    w2 = w.reshape(2, 2, CMP_STRIDE, HEAD_DIM, HEAD_DIM)
    wc = jnp.repeat(w2, 2, axis=0)
    eye = jnp.eye(4, dtype=f32)
    ws, pes = [], []
    for half in range(2):
        ws.append(jnp.einsum('csde,cf->scdfe', wc[:, half], eye).reshape(4096, 256).astype(bf16))
        pc = jnp.repeat(pe[:, half * CMP_STRIDE:(half + 1) * CMP_STRIDE], 2, axis=0)
        pes.append(jnp.transpose(pc, (1, 0, 2)).reshape(1, 4096))
    return pes[0], pes[1], ws[0], ws[1]


def _prep_s5(bbr, bbi, c_re, c_im):
    eye = jnp.eye(8, dtype=f32)

    def wb_of(bb):
        x = bb.reshape(S5_CH, S5_SLABS, 8, S5_STATE)
        return jnp.einsum('csgp,hg->shcgp', x, eye).reshape(S5_SLABS, 128, 512)

    wb = jnp.concatenate([wb_of(bbr), wb_of(bbi)], axis=-1).astype(bf16)

    def wc_of(c):
        x = c.reshape(S5_SLABS, 8, S5_CH, S5_STATE)
        return jnp.einsum('sgcp,hg->shpgc', x, eye).reshape(S5_SLABS, 512, 128)

    wc = jnp.concatenate([wc_of(c_re), -wc_of(c_im)], axis=1).astype(bf16)
    return wb, wc


def _rope_tables(pos):
    half = HEAD_DIM // 2
    inv = ROPE_THETA ** (-jnp.arange(half, dtype=f32) / half)
    ang = pos.astype(f32)[:, None] * inv[None, :]
    c, s = jnp.cos(ang), jnp.sin(ang)
    return jnp.tile(c, (1, 4)), jnp.tile(jnp.concatenate([-s, s], axis=1), (1, 2))


def kernel(x_prompt, x_sample, cache_nsa_cmp, cache_nsa_slc, cache_diff, state_nsa_win, state_sconv, state_s5_re, state_s5_im, state_ffn_conv, page_table, w_in, nsa_phi_pe, nsa_phi_w, sc_conv_w, diff_lambda, s5_a_re, s5_a_im, s5_log_dt, s5_b_re, s5_b_im, s5_c_re, s5_c_im, s5_d, s5_glu_w, s5_glu_b, mix_gain, w_out, ln1_g, ln1_b, ffn_w_up, ffn_conv_w, ffn_w_down, ln2_g, ln2_b):
    bp, t_len, _ = x_prompt.shape
    bs = x_sample.shape[0]
    n_pool = cache_nsa_cmp.shape[1]
    n_pages = page_table.shape[1]
    past_len = n_pages * PAGE_SIZE
    n_state = S5_GROUPS * S5_STATE
    mp = bp * t_len

    cos_p, sin_p = _rope_tables(jnp.arange(t_len))
    cos_s, sin_s = _rope_tables(jnp.full((bs,), past_len))

    pool_cmp = jnp.transpose(cache_nsa_cmp, (0, 1, 3, 4, 5, 2))
    pool_slc = jnp.transpose(cache_nsa_slc, (0, 1, 3, 4, 5, 2))
    pool_diff = cache_diff.reshape(DEPTH, n_pool, PAGE_SIZE * 4, 128)
    win_t = jnp.transpose(state_nsa_win, (0, 1, 3, 4, 5, 2))

    xp = x_prompt.reshape(mp, D_MODEL)
    xs = x_sample.reshape(bs, D_MODEL)
    xp_b, xs_b = xp.astype(bf16), xs.astype(bf16)

    outs_p = {k: [] for k in ('cmp', 'slc', 'win', 'diff', 'sc', 's5r', 's5i', 'ffn')}
    outs_s = {k: [] for k in ('cmp', 'slc', 'win', 'diff', 'sc', 's5r', 's5i', 'ffn')}

    gw = s5_glu_w.astype(bf16)
    wo = w_out.astype(bf16)
    wup = ffn_w_up.astype(bf16)
    wdn = ffn_w_down.astype(bf16)

    for l in range(DEPTH):
        lam_init = 0.8 - 0.6 * math.exp(-0.3 * l)
        w_in_l = _prep_w_in(w_in[l])
        pe1, pe2, w1, w2 = _prep_phi(nsa_phi_pe[l], nsa_phi_w[l])
        tabs, bbr, bbi = _s5_prep(s5_a_re[l].reshape(1, n_state), s5_a_im[l].reshape(1, n_state),
                                  jnp.repeat(s5_log_dt[l], S5_STATE).reshape(1, n_state),
                                  jnp.transpose(s5_b_re[l], (2, 0, 1)).reshape(S5_CH, n_state),
                                  jnp.transpose(s5_b_im[l], (2, 0, 1)).reshape(S5_CH, n_state))
        wb5, wc5 = _prep_s5(bbr, bbi, s5_c_re[l], s5_c_im[l])
        d5 = s5_d[l].reshape(1, 512)
        gb = s5_glu_b[l].reshape(1, 512)
        gain = mix_gain[l].reshape(1, D_MODEL)
        g1, b1 = ln1_g[l].reshape(1, D_MODEL), ln1_b[l].reshape(1, D_MODEL)
        g2, b2 = ln2_g[l].reshape(1, D_MODEL), ln2_b[l].reshape(1, D_MODEL)
        cwf = ffn_conv_w[l]
        scw = sc_conv_w[l]
        dl = diff_lambda[l]

        hcat = _in_proj(xp_b, w_in_l, 1024)
        qrot, dqrot, kvslc, kvwin, kvdiff = _rope(hcat, cos_p, sin_p, 512)
        kvcmp = hcat[:, C_CMP:C_CMP + 256]
        kvc = _cmp_prompt(kvcmp.reshape(mp // CMP_STRIDE, CMP_STRIDE * 256), pe1, pe2, w1, w2, bp)
        nsa = _nsa_prompt(hcat, qrot, kvc, kvslc, kvwin, bp, t_len)
        dif = _diff_prompt(dqrot, kvdiff, dl, bp, t_len, lam_init)
        sc, sc_tail = _sconv_prompt(hcat, scw, bp, t_len)
        s5y, s5r, s5i = _s5_scan(hcat, wb5, wc5, d5, tabs, bp, t_len)
        x1, x1b = _mix(nsa, sc, dif, s5y, gw, gb, gain, wo, xp, g1, b1, 256, l)
        xp, xp_b, ffn_tail = _ffn_prompt(x1b, x1, wup, wdn, cwf, g2, b2, t_len, l)

        outs_p['cmp'].append(kvcmp.reshape(bp, t_len, 2, 2, HEAD_DIM))
        outs_p['slc'].append(kvslc.reshape(bp, t_len, 2, 2, HEAD_DIM))
        outs_p['win'].append(kvwin.reshape(bp, t_len, 2, 2, HEAD_DIM)[:, t_len - WINDOW:])
        outs_p['diff'].append(kvdiff.reshape(bp, t_len, 2, 2, 2 * HEAD_DIM))
        outs_p['sc'].append(sc_tail[:, 6:8])
        outs_p['s5r'].append(s5r.reshape(bp, S5_GROUPS, S5_STATE))
        outs_p['s5i'].append(s5i.reshape(bp, S5_GROUPS, S5_STATE))
        tiles_per_seq = ffn_tail.shape[0] // bp
        outs_p['ffn'].append(ffn_tail.reshape(bp, tiles_per_seq, 8, D_FF)[:, -1, 6:8])

        hs = _in_proj(xs_b, w_in_l, bs)
        qrot_s, dqrot_s, kvslc_s, kvwin_s, kvdiff_s = _rope(hs, cos_s, sin_s, bs)
        kvcmp_s = hs[:, C_CMP:C_CMP + 256]
        ocmp_s, sel_s = _cmp_sel_sample(page_table, kvcmp_s.reshape(bs, 1, 256),
                                        hs[:, C_NQ:C_NQ + 512].reshape(bs, 1, 512), pe1, pe2, w1, w2,
                                        pool_cmp, l, past_len)
        nsa_s, nwin_t = _nsa_sel_sample(page_table, sel_s[:, :, 0], qrot_s.reshape(bs, 1, 512),
                                        hs[:, C_GATE:C_GATE + 128].reshape(bs, 1, 128), ocmp_s,
                                        kvslc_s.reshape(bs, 1, 256), kvwin_s.reshape(bs, 1, 256),
                                        win_t, pool_slc, l, past_len)
        dif_s = _diff_sample(page_table, dqrot_s.reshape(bs, 1, 512), kvdiff_s.reshape(bs, 1, 512),
                             dl, pool_diff, l, lam_init)
        scp = jnp.transpose(state_sconv[l], (1, 0, 2))
        sc_s, z_s, s5y_s, s5r_s, s5i_s = _sample_small(
            hs[:, C_SCB:C_SCB + 512], hs[:, C_SCC:C_SCC + 512], hs[:, C_SCH:C_SCH + 512], scw, scp,
            hs[:, C_S5U:C_S5U + 512], wb5, wc5, d5, tabs,
            state_s5_re[l].reshape(bs, n_state), state_s5_im[l].reshape(bs, n_state))
        x1s, x1sb = _mix(nsa_s.reshape(bs, 512), sc_s, dif_s.reshape(bs, 512), s5y_s, gw, gb, gain, wo,
                         xs, g1, b1, bs, l)
        prev_ffn = state_ffn_conv[l]
        xs, xs_b, aup_s = _ffn_sample(x1sb, x1s, prev_ffn[:, 0], prev_ffn[:, 1], wup, wdn, cwf, g2, b2, l)

        outs_s['cmp'].append(kvcmp_s.reshape(bs, 1, 2, 2, HEAD_DIM))
        outs_s['slc'].append(kvslc_s.reshape(bs, 1, 2, 2, HEAD_DIM))
        outs_s['win'].append(jnp.transpose(nwin_t, (0, 4, 1, 2, 3)))
        outs_s['diff'].append(kvdiff_s.reshape(bs, 1, 2, 2, 2 * HEAD_DIM))
        outs_s['sc'].append(jnp.stack([state_sconv[l][:, 1], z_s], axis=1))
        outs_s['s5r'].append(s5r_s.reshape(bs, S5_GROUPS, S5_STATE))
        outs_s['s5i'].append(s5i_s.reshape(bs, S5_GROUPS, S5_STATE))
        outs_s['ffn'].append(jnp.stack([prev_ffn[:, 1], aup_s], axis=1))

    order = ('cmp', 'slc', 'win', 'diff', 'sc', 's5r', 's5i', 'ffn')
    res = [xp.reshape(bp, t_len, D_MODEL), xs.reshape(bs, 1, D_MODEL)]
    res += [jnp.stack(outs_p[k], axis=0) for k in order]
    res += [jnp.stack(outs_s[k], axis=0) for k in order]
    return tuple(res)
```

```python
import functools
import math

import jax
import jax.numpy as jnp
from jax import lax
from jax.experimental import pallas as pl
from jax.experimental.pallas import tpu as pltpu

f32 = jnp.float32
bf16 = jnp.bfloat16

D_MODEL = 2048
DEPTH = 2
PAGE_SIZE = 128
HEAD_DIM = 64
GROUP_WIDTH = D_MODEL // 4
NSA_KV_HEADS = 2
NSA_GROUP = 4
CMP_STRIDE = 16
CMP_LEN = 32
SEL_BLOCK = 64
SEL_TOPK = 16
WINDOW = 512
FORCE_BONUS = 1e4
CONV_W = 3
S5_CH = 16
S5_GROUPS = 32
S5_STATE = 64
D_FF = 5632
ROPE_THETA = 10000.0
QBLOCK = 128
LN_EPS = 1e-5
RMS_EPS = 1e-6
NEG_INF = -1e30
DN_ALPHA = (2 * DEPTH) ** 0.25
SCALE = HEAD_DIM ** -0.5
LOG2E = math.log2(math.e)

IN_SPLITS = (512, 256, 256, 256, 24, 512, 512, 512, 512, 256, 256, 512)
C_NQ, C_SCB, C_SCC, C_SCH, C_DQ, C_S5U = 0, 512, 1024, 1536, 2048, 2560
C_CMP, C_SLC, C_WIN, C_DK, C_DV, C_GATE = 3072, 3328, 3584, 3840, 4096, 4352
HC = 4608
IN_TN = 1536
FFN_TF = 512

VMEM_CAP_V7X = 64 * 1024 * 1024
VMEM_LIMIT = 56 * 1024 * 1024
NC_PAD = 640
S5_SLABS = 4
NSA_CLASS_BLOCKS = 4
DIFF_CLASS_BLOCKS = 2
FFN_CHUNK = 256


def _cp(sem):
    return pltpu.CompilerParams(dimension_semantics=sem, vmem_limit_bytes=VMEM_LIMIT)


def _dot(a, b):
    return jnp.dot(a, b, preferred_element_type=f32)


def _dot_nt(a, b):
    return lax.dot_general(a, b, (((1,), (1,)), ((), ())), preferred_element_type=f32)


def _lane(shape):
    return lax.broadcasted_iota(jnp.int32, shape, len(shape) - 1)


def _row(shape):
    return lax.broadcasted_iota(jnp.int32, shape, len(shape) - 2)


def _msoftmax(s, mask):
    s = jnp.where(mask, s, NEG_INF)
    m = jnp.max(s, axis=-1, keepdims=True)
    e = jnp.where(mask, jnp.exp(s - m), 0.0)
    return e * (1.0 / jnp.maximum(jnp.sum(e, axis=-1, keepdims=True), 1e-30))


def _exp2_softmax(s, bias):
    s = s + bias[None]
    e = jnp.exp2(s - jnp.max(s, axis=-1, keepdims=True))
    return e, 1.0 / jnp.maximum(jnp.sum(e, axis=-1, keepdims=True), 1e-30)


def _sigmoid(x):
    return 1.0 / (1.0 + jnp.exp(-x))


def _rms_unit(x):
    return x * lax.rsqrt(jnp.mean(x * x, axis=-1, keepdims=True) + RMS_EPS)


def _layer_norm(z, g, b):
    mu = jnp.mean(z, axis=-1, keepdims=True)
    d = z - mu
    var = jnp.mean(d * d, axis=-1, keepdims=True)
    return d * lax.rsqrt(var + LN_EPS) * g + b


def _split_hi_lo(x):
    hi = x.astype(bf16)
    lo = (x - hi.astype(f32)).astype(bf16)
    return hi, lo


def _matmul_kernel(x_ref, w_ref, o_ref):
    o_ref[...] = _dot(x_ref[...], w_ref[...])


def _in_proj(xb, w, tm):
    m = xb.shape[0]
    tn = IN_TN
    return pl.pallas_call(
        _matmul_kernel,
        grid=(m // tm, HC // tn),
        in_specs=[pl.BlockSpec((tm, D_MODEL), lambda i, j: (i, 0)),
                  pl.BlockSpec((None, D_MODEL, tn), lambda i, j: (j, 0, 0))],
        out_specs=pl.BlockSpec((tm, tn), lambda i, j: (i, j)),
        out_shape=jax.ShapeDtypeStruct((m, HC), f32),
        compiler_params=_cp(("parallel", "arbitrary")),
        name="in_proj",
    )(xb, w)


def _rope_cols(x, cos, sin):
    outs = []
    first = (_lane((1, 128)) % 64) < 32
    for c in range(x.shape[1] // 128):
        xc = x[:, c * 128:(c + 1) * 128]
        sw = jnp.where(first, pltpu.roll(xc, 96, 1), pltpu.roll(xc, 32, 1))
        outs.append(xc * cos + sw * sin)
    return outs


def _rope_kernel(nq_ref, dq_ref, slc_ref, win_ref, dk_ref, dv_ref, cos_ref, sin_ref,
                 qrot_ref, dqrot_ref, kvslc_ref, kvwin_ref, kvdiff_ref):
    cos = cos_ref[...]
    sin = sin_ref[...]
    for c, v in enumerate(_rope_cols(nq_ref[...], cos, sin)):
        qrot_ref[:, c * 128:(c + 1) * 128] = v
    for c, v in enumerate(_rope_cols(dq_ref[...], cos, sin)):
        dqrot_ref[:, c * 128:(c + 1) * 128] = v
    kvslc_ref[:, 0:128] = _rope_cols(slc_ref[:, 0:128], cos, sin)[0]
    kvslc_ref[:, 128:256] = slc_ref[:, 128:256]
    kvwin_ref[:, 0:128] = _rope_cols(win_ref[:, 0:128], cos, sin)[0]
    kvwin_ref[:, 128:256] = win_ref[:, 128:256]
    for c, v in enumerate(_rope_cols(dk_ref[...], cos, sin)):
        kvdiff_ref[:, c * 128:(c + 1) * 128] = v
    kvdiff_ref[:, 256:512] = dv_ref[...]


def _rope(hcat, cos, sin, tr):
    m = hcat.shape[0]
    nt = cos.shape[0] // tr

    def col(w, off):
        return pl.BlockSpec((tr, w), lambda i: (i, off // w))

    tab = pl.BlockSpec((tr, 128), lambda i: (i % nt, 0))
    return pl.pallas_call(
        _rope_kernel,
        grid=(m // tr,),
        in_specs=[col(512, C_NQ), col(512, C_DQ), col(256, C_SLC), col(256, C_WIN),
                  col(256, C_DK), col(256, C_DV), tab, tab],
        out_specs=[pl.BlockSpec((tr, 512), lambda i: (i, 0)),
                   pl.BlockSpec((tr, 512), lambda i: (i, 0)),
                   pl.BlockSpec((tr, 256), lambda i: (i, 0)),
                   pl.BlockSpec((tr, 256), lambda i: (i, 0)),
                   pl.BlockSpec((tr, 512), lambda i: (i, 0))],
        out_shape=[jax.ShapeDtypeStruct((m, 512), f32), jax.ShapeDtypeStruct((m, 512), f32),
                   jax.ShapeDtypeStruct((m, 256), f32), jax.ShapeDtypeStruct((m, 256), f32),
                   jax.ShapeDtypeStruct((m, 512), f32)],
        compiler_params=_cp(("parallel",)),
        name="rope",
    )(hcat, hcat, hcat, hcat, hcat, hcat, cos, sin)


def _cmp_prompt_kernel(z_ref, pe1_ref, pe2_ref, w1_ref, w2_ref, o_ref, sb_ref):
    z = z_ref[...]
    a = _dot((z + pe1_ref[...]).astype(bf16), w1_ref[...])
    bm = _dot((z + pe2_ref[...]).astype(bf16), w2_ref[...])
    n = z.shape[0]
    sb_ref[0:n, :] = bm
    sb_ref[n:n + 8, :] = jnp.zeros((8, 256), f32)
    o_ref[0] = a + sb_ref[1:n + 1, :]


def _cmp_prompt(z, pe1, pe2, w1, w2, bsz):
    n = z.shape[0] // bsz
    full = lambda shp: pl.BlockSpec(shp, lambda b: (0, 0))
    return pl.pallas_call(
        _cmp_prompt_kernel,
        grid=(bsz,),
        in_specs=[pl.BlockSpec((n, 4096), lambda b: (b, 0)), full((1, 4096)), full((1, 4096)),
                  full((4096, 256)), full((4096, 256))],
        out_specs=pl.BlockSpec((1, n, 256), lambda b: (b, 0, 0)),
        out_shape=jax.ShapeDtypeStruct((bsz, n, 256), f32),
        scratch_shapes=[pltpu.VMEM((n + 8, 256), f32)],
        compiler_params=_cp(("parallel",)),
        name="cmp_prompt",
    )(z, pe1, pe2, w1, w2)


def _nsa_qstack(blk, h):
    halfmask = (_lane((1, 128)) // 64) == h
    parts = []
    for g in range(4):
        c = blk[:, (g // 2) * 128:(g // 2 + 1) * 128]
        if g % 2 != h:
            c = pltpu.roll(c, 64, 1)
        parts.append(jnp.where(halfmask, c, 0.0))
    return jnp.concatenate(parts, axis=0)


def _nsa_assemble(o_list, h):
    lo = _lane((1, 128)) < 64
    chunks = []
    for gp in range(2):
        a, b = o_list[2 * gp], o_list[2 * gp + 1]
        if h == 0:
            b = pltpu.roll(b, 64, 1)
        else:
            a = pltpu.roll(a, 64, 1)
        chunks.append(jnp.where(lo, a, b))
    return chunks


def _nsa_prompt_body(qraw_ref, qrot_ref, gate_ref, kvc_ref, slc_ref, win_ref, cov_ref, exp_ref, out_ref,
                     *, s0, kmax, t_len):
    qb = QBLOCK
    qpos = s0 + _row((qb, 1))
    gs = _sigmoid(gate_ref[...])
    n_i = _lane((1, 128))
    maskc = ((16 * n_i + 31) <= qpos) & (n_i < 127)
    cov = cov_ref[...]
    expand = exp_ref[:, 0:kmax]
    causal = _lane((1, kmax)) <= qpos
    wlen = WINDOW + qb
    start = pl.multiple_of(jnp.clip(s0 - WINDOW, 0, t_len - wlen), 128)
    wpos = start + _lane((1, wlen))
    bias_w = jnp.where((wpos <= qpos) & ((qpos - wpos) < WINDOW), 0.0, NEG_INF)
    jj = _lane((1, 128))
    cur = qpos // SEL_BLOCK
    n_sel = kmax // SEL_BLOCK
    n_blk = t_len // SEL_BLOCK
    j_t = _row((n_blk, qb))
    forced = (jj == 0) | (jj == cur) | (jj == cur - 1)

    for h in range(NSA_KV_HEADS):
        kc = kvc_ref[0, :, 0:128].astype(bf16)
        vc = kvc_ref[0, :, 128:256].astype(bf16)
        qr = (_nsa_qstack(qraw_ref[:, h * 256:(h + 1) * 256], h) * SCALE).astype(bf16)
        qo = (_nsa_qstack(qrot_ref[:, h * 256:(h + 1) * 256], h) * (SCALE * LOG2E)).astype(bf16)
        s_c = _dot_nt(qr, kc).reshape(4, qb, 128)
        p_c = _msoftmax(s_c, maskc[None])
        o_cmp = _dot(p_c.reshape(4 * qb, 128).astype(bf16), vc)
        psum = p_c[0] + p_c[1] + p_c[2] + p_c[3]
        p_hi, p_lo = _split_hi_lo(psum)
        imp = _dot(p_hi, cov) + _dot(p_lo, cov)
        imp = jnp.where(forced, imp + FORCE_BONUS, imp)
        imp = jnp.where(jj <= cur, imp, -FORCE_BONUS)
        imp = jnp.where(jj < n_sel, imp, -3e38)
        imp_t = imp.T[0:n_blk]
        rank = jnp.zeros((n_blk, qb), f32)
        for k in range(n_sel):
            rk = imp_t[k:k + 1, :]
            beats = (rk > imp_t) | ((rk == imp_t) & (j_t > k))
            rank = rank + jnp.where(beats, 1.0, 0.0)
        sel_t = jnp.where((rank < SEL_TOPK) & (j_t < n_sel), 1.0, 0.0)
        sel = jnp.concatenate([sel_t, jnp.zeros((128 - n_blk, qb), f32)], axis=0).T.astype(bf16)
        bias_s = jnp.where((_dot(sel, expand) > 0.5) & causal, 0.0, NEG_INF)
        ks = slc_ref[0:kmax, 0:128].astype(bf16)
        vs = slc_ref[0:kmax, 128:256].astype(bf16)
        e_s, inv_s = _exp2_softmax(_dot_nt(qo, ks).reshape(4, qb, kmax), bias_s)
        o_slc = _dot(e_s.reshape(4 * qb, kmax).astype(bf16), vs) * inv_s.reshape(4 * qb, 1)
        kw = win_ref[pl.ds(start, wlen), 0:128].astype(bf16)
        vw = win_ref[pl.ds(start, wlen), 128:256].astype(bf16)
        e_w, inv_w = _exp2_softmax(_dot_nt(qo, kw).reshape(4, qb, wlen), bias_w)
        o_win = _dot(e_w.reshape(4 * qb, wlen).astype(bf16), vw) * inv_w.reshape(4 * qb, 1)
        o_list = []
        for g in range(NSA_GROUP):
            gi = (h * NSA_GROUP + g) * 3
            r = slice(g * qb, (g + 1) * qb)
            o_list.append(gs[:, gi:gi + 1] * o_cmp[r] + gs[:, gi + 1:gi + 2] * o_slc[r]
                          + gs[:, gi + 2:gi + 3] * o_win[r])
        for gp, ch in enumerate(_nsa_assemble(o_list, h)):
            out_ref[:, h * 256 + gp * 128:h * 256 + (gp + 1) * 128] = ch


def _by_key_class(body, t_len, blocks):
    i = pl.program_id(1)
    span = blocks * QBLOCK
    for c in range(t_len // span):
        @pl.when(i // blocks == c)
        def _(c=c):
            body(s0=i * QBLOCK, kmax=(c + 1) * span)


def _nsa_prompt_kernel(*refs, t_len):
    _by_key_class(functools.partial(_nsa_prompt_body, *refs, t_len=t_len), t_len, NSA_CLASS_BLOCKS)


def _cover_matrix(n_rows, n_cols, n_cmp, n_sel):
    n = jnp.arange(n_rows)[:, None]
    j = jnp.arange(n_cols)[None, :]
    cov = jnp.clip(jnp.minimum(CMP_STRIDE * n + CMP_LEN, SEL_BLOCK * (j + 1)) - jnp.maximum(CMP_STRIDE * n, SEL_BLOCK * j),
                   0, CMP_LEN)
    cov = jnp.where((n < n_cmp) & (j < n_sel), cov, 0).astype(f32) / CMP_LEN
    return cov.astype(bf16)


def _expand_matrix(n_rows, n_keys):
    return (jnp.arange(n_keys)[None, :] // SEL_BLOCK == jnp.arange(n_rows)[:, None]).astype(bf16)


def _nsa_prompt(hcat, qrot, kvc, kvslc, kvwin, bsz, t_len):
    m = hcat.shape[0]
    nqb = t_len // QBLOCK
    cov = _cover_matrix(128, 128, t_len // CMP_STRIDE - 1, t_len // SEL_BLOCK)
    expand = _expand_matrix(128, t_len)
    return pl.pallas_call(
        functools.partial(_nsa_prompt_kernel, t_len=t_len),
        grid=(bsz, nqb),
        in_specs=[pl.BlockSpec((QBLOCK, 512), lambda b, i: (b * nqb + i, 0)),
                  pl.BlockSpec((QBLOCK, 512), lambda b, i: (b * nqb + i, 0)),
                  pl.BlockSpec((QBLOCK, 128), lambda b, i: (b * nqb + i, C_GATE // 128)),
                  pl.BlockSpec((1, 128, 256), lambda b, i: (b, 0, 0)),
                  pl.BlockSpec((t_len, 256), lambda b, i: (b, 0)),
                  pl.BlockSpec((t_len, 256), lambda b, i: (b, 0)),
                  pl.BlockSpec((128, 128), lambda b, i: (0, 0)),
                  pl.BlockSpec((128, t_len), lambda b, i: (0, 0))],
        out_specs=pl.BlockSpec((QBLOCK, 512), lambda b, i: (b * nqb + i, 0)),
        out_shape=jax.ShapeDtypeStruct((m, 512), f32),
        compiler_params=_cp(("parallel", "arbitrary")),
        name="nsa_prompt",
    )(hcat, qrot, hcat, kvc, kvslc, kvwin, cov, expand)


def _diff_lambda(dl, lam_init):
    a = jnp.sum(dl[0:1, :] * dl[1:2, :], axis=-1, keepdims=True)
    b = jnp.sum(dl[2:3, :] * dl[3:4, :], axis=-1, keepdims=True)
    return jnp.exp(a) - jnp.exp(b) + lam_init


def _diff_prompt_body(dq_ref, kv_ref, dl_ref, out_ref, *, s0, kmax, lam_init):
    qb = QBLOCK
    qpos = s0 + _row((qb, 1))
    bias = jnp.where(_lane((1, kmax)) <= qpos, 0.0, NEG_INF)
    lam = _diff_lambda(dl_ref[...], lam_init)
    lane = _lane((1, 128))
    for h in range(2):
        k = kv_ref[0:kmax, h * 128:(h + 1) * 128].astype(bf16)
        v = kv_ref[0:kmax, 256 + h * 128:256 + (h + 1) * 128].astype(bf16)
        parts = []
        for g in range(2):
            c = dq_ref[:, h * 256 + g * 128:h * 256 + (g + 1) * 128] * (SCALE * LOG2E)
            for i in range(2):
                parts.append(jnp.where((lane // 64) == i, c, 0.0))
        q = jnp.concatenate(parts, axis=0).astype(bf16)
        e, inv = _exp2_softmax(_dot_nt(q, k).reshape(4, qb, kmax), bias)
        o = _dot(e.reshape(4 * qb, kmax).astype(bf16), v) * inv.reshape(4 * qb, 1)
        for g in range(2):
            og = o[2 * g * qb:(2 * g + 1) * qb] - lam * o[(2 * g + 1) * qb:(2 * g + 2) * qb]
            out_ref[:, (h * 2 + g) * 128:(h * 2 + g + 1) * 128] = _rms_unit(og) * (1.0 - lam_init)


def _diff_prompt_kernel(*refs, t_len, lam_init):
    _by_key_class(functools.partial(_diff_prompt_body, *refs, lam_init=lam_init), t_len, DIFF_CLASS_BLOCKS)


def _diff_prompt(dqrot, kvdiff, dl, bsz, t_len, lam_init):
    m = dqrot.shape[0]
    nqb = t_len // QBLOCK
    return pl.pallas_call(
        functools.partial(_diff_prompt_kernel, t_len=t_len, lam_init=lam_init),
        grid=(bsz, nqb),
        in_specs=[pl.BlockSpec((QBLOCK, 512), lambda b, i: (b * nqb + i, 0)),
                  pl.BlockSpec((t_len, 512), lambda b, i: (b, 0)),
                  pl.BlockSpec((4, 64), lambda b, i: (0, 0))],
        out_specs=pl.BlockSpec((QBLOCK, 512), lambda b, i: (b * nqb + i, 0)),
        out_shape=jax.ShapeDtypeStruct((m, 512), f32),
        compiler_params=_cp(("parallel", "arbitrary")),
        name="diff_prompt",
    )(dqrot, kvdiff, dl)


def _sconv_prompt_kernel(b_ref, c_ref, h_ref, w_ref, out_ref, tail_ref, buf_ref, *, tr):
    t = pl.program_id(1)

    @pl.when(t == 0)
    def _():
        buf_ref[0:8, :] = jnp.zeros((8, 512), f32)

    z = c_ref[...] * h_ref[...]
    buf_ref[8:8 + tr, :] = z
    w = w_ref[...]
    y = w[0:1] * buf_ref[6:6 + tr, :] + w[1:2] * buf_ref[7:7 + tr, :] + w[2:3] * z
    out_ref[...] = _rms_unit(b_ref[...] * y)
    tail = z[tr - 8:tr]
    tail_ref[0] = tail
    buf_ref[0:8, :] = tail


def _sconv_prompt(hcat, w, bsz, t_len):
    tr = 512
    nt = t_len // tr
    m = hcat.shape[0]

    def col(off):
        return pl.BlockSpec((tr, 512), lambda b, t: (b * nt + t, off // 512))

    return pl.pallas_call(
        functools.partial(_sconv_prompt_kernel, tr=tr),
        grid=(bsz, nt),
        in_specs=[col(C_SCB), col(C_SCC), col(C_SCH), pl.BlockSpec((3, 512), lambda b, t: (0, 0))],
        out_specs=[pl.BlockSpec((tr, 512), lambda b, t: (b * nt + t, 0)),
                   pl.BlockSpec((1, 8, 512), lambda b, t: (b, 0, 0))],
        out_shape=[jax.ShapeDtypeStruct((m, 512), f32), jax.ShapeDtypeStruct((bsz, 8, 512), f32)],
        scratch_shapes=[pltpu.VMEM((8 + tr, 512), f32)],
        compiler_params=_cp(("parallel", "arbitrary")),
        name="sconv_prompt",
    )(hcat, hcat, hcat, w)


def _cmul(ar, ai, br, bi):
    return ar * br - ai * bi, ar * bi + ai * br


def _s5_prep_kernel(ar_ref, ai_ref, ldt_ref, br_ref, bi_ref, tab_ref, bbr_ref, bbi_ref):
    ar, ai = ar_ref[...], ai_ref[...]
    dt = jnp.exp(ldt_ref[...])
    mag = jnp.exp(ar * dt)
    abr, abi = mag * jnp.cos(ai * dt), mag * jnp.sin(ai * dt)
    den = ar * ar + ai * ai
    nr, ni = abr - 1.0, abi
    cre = (nr * ar + ni * ai) / den
    cim = (ni * ar - nr * ai) / den
    br, bi = br_ref[...], bi_ref[...]
    bbr_ref[...] = cre * br - cim * bi
    bbi_ref[...] = cre * bi + cim * br
    pw = [(abr, abi)]
    for _ in range(7):
        pw.append(_cmul(pw[-1][0], pw[-1][1], abr, abi))
    n = ar.shape[1]
    row = _row((8, n))
    zero = jnp.zeros((8, n), f32)
    for idx, (sh, p) in enumerate(((1, pw[0]), (2, pw[1]), (4, pw[3]))):
        tab_ref[2 * idx] = jnp.where(row >= sh, jnp.broadcast_to(p[0], (8, n)), zero)
        tab_ref[2 * idx + 1] = jnp.where(row >= sh, jnp.broadcast_to(p[1], (8, n)), zero)
    pr, pi = zero, zero
    for i in range(8):
        pr = jnp.where(row == i, jnp.broadcast_to(pw[i][0], (8, n)), pr)
        pi = jnp.where(row == i, jnp.broadcast_to(pw[i][1], (8, n)), pi)
    tab_ref[6] = pr
    tab_ref[7] = pi


def _s5_prep(ar, ai, ldt, br, bi):
    n = S5_GROUPS * S5_STATE
    return pl.pallas_call(
        _s5_prep_kernel,
        out_shape=[jax.ShapeDtypeStruct((8, 8, n), f32), jax.ShapeDtypeStruct((S5_CH, n), f32),
                   jax.ShapeDtypeStruct((S5_CH, n), f32)],
        name="s5_prep",
    )(ar, ai, ldt, br, bi)


def _s5_scan_kernel(u_ref, wb_ref, wc_ref, d_ref, tab_ref, y_ref, hr_ref, hi_ref,
                    xbuf_ref, cr_ref, ci_ref, *, tt):
    t = pl.program_id(2)

    @pl.when(t == 0)
    def _():
        cr_ref[...] = jnp.zeros((8, 512), f32)
        ci_ref[...] = jnp.zeros((8, 512), f32)

    u = u_ref[...]
    xbuf_ref[...] = _dot(u.astype(bf16), wb_ref[0])

    def body(r, carry):
        cr, ci = carry
        rows = pl.ds(pl.multiple_of(r * 8, 8), 8)
        xr = xbuf_ref[rows, 0:512]
        xi = xbuf_ref[rows, 512:1024]
        for idx, sh in enumerate((1, 2, 4)):
            a_r, a_i = tab_ref[2 * idx], tab_ref[2 * idx + 1]
            sr, si = pltpu.roll(xr, sh, 0), pltpu.roll(xi, sh, 0)
            xr, xi = xr + a_r * sr - a_i * si, xi + a_r * si + a_i * sr
        p_r, p_i = tab_ref[6], tab_ref[7]
        hr = xr + p_r * cr - p_i * ci
        hi = xi + p_r * ci + p_i * cr
        xbuf_ref[rows, 0:512] = hr
        xbuf_ref[rows, 512:1024] = hi
        return (jnp.broadcast_to(hr[7:8, :], (8, 512)), jnp.broadcast_to(hi[7:8, :], (8, 512)))

    cr, ci = lax.fori_loop(0, tt // 8, body, (cr_ref[...], ci_ref[...]), unroll=4)
    cr_ref[...] = cr
    ci_ref[...] = ci
    hr_ref[0] = cr[0:1, :]
    hi_ref[0] = ci[0:1, :]
    y_ref[...] = _dot(xbuf_ref[...].astype(bf16), wc_ref[0]) + d_ref[...] * u


def _s5_scan(hcat, wb, wc, d, tabs, bsz, t_len):
    tt = 1024
    nt = t_len // tt
    m = hcat.shape[0]
    n = S5_GROUPS * S5_STATE
    return pl.pallas_call(
        functools.partial(_s5_scan_kernel, tt=tt),
        grid=(bsz, S5_SLABS, nt),
        in_specs=[pl.BlockSpec((tt, 128), lambda b, s, t: (b * nt + t, C_S5U // 128 + s)),
                  pl.BlockSpec((1, 128, 1024), lambda b, s, t: (s, 0, 0)),
                  pl.BlockSpec((1, 1024, 128), lambda b, s, t: (s, 0, 0)),
                  pl.BlockSpec((1, 128), lambda b, s, t: (0, s)),
                  pl.BlockSpec((8, 8, 512), lambda b, s, t: (0, 0, s))],
        out_specs=[pl.BlockSpec((tt, 128), lambda b, s, t: (b * nt + t, s)),
                   pl.BlockSpec((1, 1, 512), lambda b, s, t: (b, 0, s)),
                   pl.BlockSpec((1, 1, 512), lambda b, s, t: (b, 0, s))],
        out_shape=[jax.ShapeDtypeStruct((m, 512), f32), jax.ShapeDtypeStruct((bsz, 1, n), f32),
                   jax.ShapeDtypeStruct((bsz, 1, n), f32)],
        scratch_shapes=[pltpu.VMEM((tt, 1024), f32), pltpu.VMEM((8, 512), f32), pltpu.VMEM((8, 512), f32)],
        compiler_params=_cp(("parallel", "parallel", "arbitrary")),
        name="s5_scan",
    )(hcat, wb, wc, d, tabs)


def _sample_small_kernel(scb_ref, scc_ref, sch_ref, scw_ref, scp_ref, u_ref, wb_ref, wc_ref, d_ref,
                         tab_ref, h0r_ref, h0i_ref, sc_ref, z_ref, y_ref, hr_ref, hi_ref):
    z = scc_ref[...] * sch_ref[...]
    w = scw_ref[...]
    y = w[0:1] * scp_ref[0] + w[1:2] * scp_ref[1] + w[2:3] * z
    sc_ref[...] = _rms_unit(scb_ref[...] * y)
    z_ref[...] = z
    u = u_ref[...]
    for s in range(S5_SLABS):
        x = _dot(u[:, s * 128:(s + 1) * 128].astype(bf16), wb_ref[s])
        lanes = slice(s * 512, (s + 1) * 512)
        a_r, a_i = tab_ref[6, 0:1, lanes], tab_ref[7, 0:1, lanes]
        h0r, h0i = h0r_ref[:, lanes], h0i_ref[:, lanes]
        hr = a_r * h0r - a_i * h0i + x[:, 0:512]
        hi = a_r * h0i + a_i * h0r + x[:, 512:1024]
        hr_ref[:, lanes] = hr
        hi_ref[:, lanes] = hi
        hcat = jnp.concatenate([hr, hi], axis=1).astype(bf16)
        cols = slice(s * 128, (s + 1) * 128)
        y_ref[:, cols] = _dot(hcat, wc_ref[s]) + d_ref[:, cols] * u[:, cols]


def _sample_small(scb, scc, sch, scw, scp, u, wb, wc, d, tabs, h0r, h0i):
    bsz = u.shape[0]
    n = S5_GROUPS * S5_STATE
    return pl.pallas_call(
        _sample_small_kernel,
        out_shape=[jax.ShapeDtypeStruct((bsz, 512), f32), jax.ShapeDtypeStruct((bsz, 512), f32),
                   jax.ShapeDtypeStruct((bsz, 512), f32), jax.ShapeDtypeStruct((bsz, n), f32),
                   jax.ShapeDtypeStruct((bsz, n), f32)],
        compiler_params=pltpu.CompilerParams(vmem_limit_bytes=VMEM_LIMIT),
        name="sample_small",
    )(scb, scc, sch, scw, scp, u, wb, wc, d, tabs, h0r, h0i)


def _gelu(x):
    return 0.5 * x * (1.0 + jnp.tanh(math.sqrt(2.0 / math.pi) * (x + 0.044715 * (x * x * x))))


def _mix_kernel(nsa_ref, sc_ref, diff_ref, s5_ref, gw_ref, gb_ref, gain_ref, wo_ref, x_ref,
                g_ref, b_ref, out_ref, outb_ref):
    y = _gelu(s5_ref[...])
    s5o = y * _sigmoid(_dot(y.astype(bf16), gw_ref[...]) + gb_ref[...])
    parts = (_rms_unit(nsa_ref[...]), sc_ref[...], diff_ref[...], _rms_unit(s5o))
    mixed = jnp.concatenate([(p * gain_ref[:, k * 512:(k + 1) * 512]).astype(bf16) for k, p in enumerate(parts)],
                            axis=1)
    o = _layer_norm(DN_ALPHA * x_ref[...] + _dot(mixed, wo_ref[...]), g_ref[...], b_ref[...])
    out_ref[...] = o
    outb_ref[...] = o.astype(bf16)


def _mix(nsa, sc, diff, s5y, gw, gb, gain, wo, x, g, b, tm, layer):
    m = x.shape[0]
    row = lambda w: pl.BlockSpec((tm, w), lambda i: (i, 0))
    full = lambda shp: pl.BlockSpec(shp, lambda i: (0, 0))
    stacked = lambda shp: pl.BlockSpec((None,) + shp, lambda i: (layer, 0, 0))
    return pl.pallas_call(
        _mix_kernel,
        grid=(m // tm,),
        in_specs=[row(512), row(512), row(512), row(512), stacked((512, 512)), full((1, 512)),
                  full((1, D_MODEL)), stacked((D_MODEL, D_MODEL)), row(D_MODEL), full((1, D_MODEL)),
                  full((1, D_MODEL))],
        out_specs=[row(D_MODEL), row(D_MODEL)],
        out_shape=[jax.ShapeDtypeStruct((m, D_MODEL), f32), jax.ShapeDtypeStruct((m, D_MODEL), bf16)],
        compiler_params=_cp(("parallel",)),
        name="mix_outproj_ln",
    )(nsa, sc, diff, s5y, gw, gb, gain, wo, x, g, b)


def _ffn_tail(acc_ref, xres_ref, g_ref, b_ref, out_ref, outb_ref):
    o = _layer_norm(DN_ALPHA * xres_ref[...] + acc_ref[...], g_ref[...], b_ref[...])
    out_ref[...] = o
    outb_ref[...] = o.astype(bf16)


def _ffn_prompt_kernel(x_ref, halo_ref, wa_ref, wb_ref, wd_ref, cw_ref, xres_ref, g_ref, b_ref,
                       out_ref, outb_ref, tail_ref, acc_ref, abuf_ref, *, tm, nf, tiles_per_seq):
    i = pl.program_id(0)
    f = pl.program_id(1)

    @pl.when(f == 0)
    def _():
        acc_ref[...] = jnp.zeros_like(acc_ref)

    x = x_ref[...]
    halo = halo_ref[...]
    keep = jnp.where(i % tiles_per_seq != 0, 1.0, 0.0)
    cw = cw_ref[...]
    gates = []
    for c in range(abuf_ref.shape[0]):
        cols = slice(c * FFN_CHUNK, (c + 1) * FFN_CHUNK)
        a = _dot(x, wa_ref[:, cols])
        bb = _dot(x, wb_ref[:, cols])
        abuf_ref[c, 0:16, :] = _dot(halo, wa_ref[:, cols]) * keep
        abuf_ref[c, 16:16 + tm, :] = a
        ac = (cw[0:1, cols] * abuf_ref[c, 14:14 + tm, :] + cw[1:2, cols] * abuf_ref[c, 15:15 + tm, :]
              + cw[2:3, cols] * a)
        gates.append((ac * _sigmoid(ac) * bb).astype(bf16))
        tail_ref[0, :, cols] = a[tm - 8:tm]
    acc_ref[...] += _dot(jnp.concatenate(gates, axis=1), wd_ref[...])

    @pl.when(f == nf - 1)
    def _():
        _ffn_tail(acc_ref, xres_ref, g_ref, b_ref, out_ref, outb_ref)


def _ffn_prompt(xb, x, wup, wd, cw, g, b, t_len, layer):
    m = x.shape[0]
    tm, tf = 512, FFN_TF
    nf = D_FF // tf
    full = lambda shp: pl.BlockSpec(shp, lambda i, f: (0, 0))
    return pl.pallas_call(
        functools.partial(_ffn_prompt_kernel, tm=tm, nf=nf, tiles_per_seq=t_len // tm),
        grid=(m // tm, nf),
        in_specs=[pl.BlockSpec((tm, D_MODEL), lambda i, f: (i, 0)),
                  pl.BlockSpec((16, D_MODEL), lambda i, f: (jnp.maximum(i * (tm // 16) - 1, 0), 0)),
                  pl.BlockSpec((None, D_MODEL, tf), lambda i, f: (layer, 0, f)),
                  pl.BlockSpec((None, D_MODEL, tf), lambda i, f: (layer, 0, f + nf)),
                  pl.BlockSpec((None, tf, D_MODEL), lambda i, f: (layer, f, 0)),
                  pl.BlockSpec((3, tf), lambda i, f: (0, f)),
                  pl.BlockSpec((tm, D_MODEL), lambda i, f: (i, 0)),
                  full((1, D_MODEL)), full((1, D_MODEL))],
        out_specs=[pl.BlockSpec((tm, D_MODEL), lambda i, f: (i, 0)),
                   pl.BlockSpec((tm, D_MODEL), lambda i, f: (i, 0)),
                   pl.BlockSpec((1, 8, tf), lambda i, f: (i, 0, f))],
        out_shape=[jax.ShapeDtypeStruct((m, D_MODEL), f32), jax.ShapeDtypeStruct((m, D_MODEL), bf16),
                   jax.ShapeDtypeStruct((m // tm, 8, D_FF), f32)],
        scratch_shapes=[pltpu.VMEM((tm, D_MODEL), f32), pltpu.VMEM((tf // FFN_CHUNK, 16 + tm, FFN_CHUNK), f32)],
        compiler_params=_cp(("parallel", "arbitrary")),
        name="ffn_prompt",
    )(xb, xb, wup, wup, wd, cw, x, g, b)


def _ffn_sample_kernel(x_ref, p0_ref, p1_ref, wa_ref, wb_ref, wd_ref, cw_ref, xres_ref, g_ref, b_ref,
                       out_ref, outb_ref, aup_ref, acc_ref, *, nf):
    f = pl.program_id(0)

    @pl.when(f == 0)
    def _():
        acc_ref[...] = jnp.zeros_like(acc_ref)

    x = x_ref[...]
    a = _dot(x, wa_ref[...])
    bb = _dot(x, wb_ref[...])
    cw = cw_ref[...]
    ac = cw[0:1] * p0_ref[...] + cw[1:2] * p1_ref[...] + cw[2:3] * a
    gate = (ac * _sigmoid(ac) * bb).astype(bf16)
    acc_ref[...] += _dot(gate, wd_ref[...])
    aup_ref[...] = a

    @pl.when(f == nf - 1)
    def _():
        _ffn_tail(acc_ref, xres_ref, g_ref, b_ref, out_ref, outb_ref)


def _ffn_sample(xb, x, p0, p1, wup, wd, cw, g, b, layer):
    m = x.shape[0]
    tf = FFN_TF
    nf = D_FF // tf
    full = lambda shp: pl.BlockSpec(shp, lambda f: (0, 0))
    return pl.pallas_call(
        functools.partial(_ffn_sample_kernel, nf=nf),
        grid=(nf,),
        in_specs=[full((m, D_MODEL)), pl.BlockSpec((m, tf), lambda f: (0, f)),
                  pl.BlockSpec((m, tf), lambda f: (0, f)),
                  pl.BlockSpec((None, D_MODEL, tf), lambda f: (layer, 0, f)),
                  pl.BlockSpec((None, D_MODEL, tf), lambda f: (layer, 0, f + nf)),
                  pl.BlockSpec((None, tf, D_MODEL), lambda f: (layer, f, 0)),
                  pl.BlockSpec((3, tf), lambda f: (0, f)),
                  full((m, D_MODEL)), full((1, D_MODEL)), full((1, D_MODEL))],
        out_specs=[full((m, D_MODEL)), full((m, D_MODEL)), pl.BlockSpec((m, tf), lambda f: (0, f))],
        out_shape=[jax.ShapeDtypeStruct((m, D_MODEL), f32), jax.ShapeDtypeStruct((m, D_MODEL), bf16),
                   jax.ShapeDtypeStruct((m, D_FF), f32)],
        scratch_shapes=[pltpu.VMEM((m, D_MODEL), f32)],
        compiler_params=_cp(("arbitrary",)),
        name="ffn_sample",
    )(xb, p0, p1, wup, wup, wd, cw, x, g, b)


def _cmp_sel_sample_kernel(pt_ref, new_ref, qraw_ref, pe1_ref, pe2_ref, w1_ref, w2_ref, cov_ref, pool_ref,
                           ocmp_ref, idx_ref, buf_ref, x_ref, sb_ref, kvc_ref, cst_ref, sem_ref,
                           *, layer, n_pages, n_batch, past_len):
    b = pl.program_id(0)
    slot = b % 2
    n = n_pages * (PAGE_SIZE // CMP_STRIDE)

    def copy(bb, sl, p):
        return pltpu.make_async_copy(pool_ref.at[layer, pt_ref[bb, p]], buf_ref.at[sl, p], sem_ref.at[sl])

    def fetch(bb, sl):
        for p in range(n_pages):
            copy(bb, sl, p).start()

    @pl.when(b == 0)
    def _():
        fetch(0, 0)
        for idx, (pe_ref, w_ref) in enumerate(((pe1_ref, w1_ref), (pe2_ref, w2_ref))):
            tot = None
            for s in range(CMP_STRIDE):
                pes = jnp.broadcast_to(pe_ref[:, s * 256:(s + 1) * 256], (8, 256)).astype(bf16)
                d = _dot(pes, w_ref[s * 256:(s + 1) * 256, :])
                tot = d if tot is None else tot + d
            cst_ref[idx] = tot

    @pl.when(b + 1 < n_batch)
    def _():
        fetch(b + 1, 1 - slot)

    for p in range(n_pages):
        copy(b, slot, p).wait()

    r_i = _row((PAGE_SIZE, PAGE_SIZE))
    pick = jnp.where(_lane((PAGE_SIZE, PAGE_SIZE)) == CMP_STRIDE * (r_i % 8) + r_i // 8, 1.0, 0.0).astype(bf16)

    def regroup(q, carry):
        rows = pl.ds(pl.multiple_of(q * 16, 16), 16)
        r0 = _dot_nt(pick, buf_ref[slot, 2 * q].reshape(256, PAGE_SIZE).astype(bf16))
        r1 = _dot_nt(pick, buf_ref[slot, 2 * q + 1].reshape(256, PAGE_SIZE).astype(bf16))
        for s in range(CMP_STRIDE):
            pair = jnp.concatenate([r0[s * 8:(s + 1) * 8], r1[s * 8:(s + 1) * 8]], axis=0)
            x_ref[rows, s * 256:(s + 1) * 256] = pair.astype(bf16)
        return carry

    lax.fori_loop(0, n_pages // 2, regroup, 0, unroll=4)

    xb = x_ref[...]
    a = _dot(xb, w1_ref[...]) + cst_ref[0, 0:1, :]
    bm = _dot(xb, w2_ref[...]) + cst_ref[1, 0:1, :]
    new8 = jnp.broadcast_to(new_ref[0], (8, 256)).astype(bf16)
    row0 = _row((8, 256)) == 0
    a_new = cst_ref[0] + jnp.where(row0, _dot(new8, w1_ref[0:256, :]), 0.0)
    b_new = cst_ref[1] + jnp.where(row0, _dot(new8, w2_ref[0:256, :]), 0.0)
    sb_ref[0:n, :] = bm
    sb_ref[n:n + 8, :] = b_new
    sb_ref[n + 8:n + 16, :] = jnp.zeros((8, 256), f32)
    kvc_ref[0:n, :] = a + sb_ref[1:n + 1, :]
    kvc_ref[n:n + 8, :] = a_new + sb_ref[n + 1:n + 9, :]
    kvc_ref[n + 8:NC_PAD, :] = jnp.zeros((NC_PAD - n - 8, 256), f32)

    qpos = past_len
    n_sel = past_len // SEL_BLOCK + 1
    n_cmp = n_sel * SEL_BLOCK // CMP_STRIDE - 1
    cur = qpos // SEL_BLOCK
    n_i = _lane((1, NC_PAD))
    maskc = ((16 * n_i + 31) <= qpos) & (n_i < n_cmp)
    cov = cov_ref[...]
    jj = _lane((1, 256))
    forced = (jj == 0) | (jj == cur) | (jj == cur - 1)
    kk = _row((256, 256))
    jjm = _lane((256, 256))
    eye = kk == jjm
    before = jnp.where(kk < jjm, 1.0, 0.0).astype(bf16)
    slot_id = _row((SEL_TOPK, 256))
    jj16 = _lane((SEL_TOPK, 256)).astype(f32)
    qraw = qraw_ref[0]
    for h in range(NSA_KV_HEADS):
        kc = kvc_ref[:, 0:128].astype(bf16)
        vc = kvc_ref[:, 128:256].astype(bf16)
        qr = (_nsa_q8(qraw, h, h) * SCALE).astype(bf16)
        p_c = _msoftmax(_dot_nt(qr, kc), maskc)
        ocmp_ref[0, h] = _dot(p_c.astype(bf16), vc)
        psum = jnp.broadcast_to(p_c[0:1] + p_c[1:2] + p_c[2:3] + p_c[3:4], (8, NC_PAD))
        p_hi, p_lo = _split_hi_lo(psum)
        imp = (_dot(p_hi, cov) + _dot(p_lo, cov))[0:1]
        imp = jnp.where(forced, imp + FORCE_BONUS, imp)
        imp = jnp.where(jj <= cur, imp, -FORCE_BONUS)
        imp = jnp.where(jj < n_sel, imp, -3e38)
        imp_j = jnp.broadcast_to(imp, (256, 256))
        imp_k = jnp.broadcast_to(jnp.sum(jnp.where(eye, imp_j, 0.0), axis=1, keepdims=True), (256, 256))
        beats = (imp_k > imp_j) | ((imp_k == imp_j) & (kk < jjm))
        rank = jnp.sum(jnp.where(beats, 1.0, 0.0), axis=0, keepdims=True)
        sel = jnp.where((rank < SEL_TOPK) & (jj < n_sel), 1.0, 0.0)
        pos = _dot(jnp.broadcast_to(sel, (8, 256)).astype(bf16), before)[0:1]
        hit = (jnp.broadcast_to(pos, (SEL_TOPK, 256)) == slot_id.astype(f32)) & (jnp.broadcast_to(sel, (SEL_TOPK, 256)) > 0.5)
        ids = jnp.sum(jnp.where(hit, jj16, 0.0), axis=1, keepdims=True)
        idx_ref[0, h * SEL_TOPK:(h + 1) * SEL_TOPK, :] = jnp.broadcast_to(ids, (SEL_TOPK, 128)).astype(jnp.int32)


def _cmp_sel_sample(page_table, new_rows, qraw, pe1, pe2, w1, w2, pool_t, layer, past_len):
    n_batch, n_pages = page_table.shape
    n = n_pages * (PAGE_SIZE // CMP_STRIDE)
    n_sel = past_len // SEL_BLOCK + 1
    cov = _cover_matrix(NC_PAD, 256, n_sel * SEL_BLOCK // CMP_STRIDE - 1, n_sel)
    full = lambda shp: pl.BlockSpec(shp, lambda b, pt: (0,) * len(shp))
    grid_spec = pltpu.PrefetchScalarGridSpec(
        num_scalar_prefetch=1,
        grid=(n_batch,),
        in_specs=[pl.BlockSpec((1, 1, 256), lambda b, pt: (b, 0, 0)),
                  pl.BlockSpec((1, 1, 512), lambda b, pt: (b, 0, 0)), full((1, 4096)), full((1, 4096)),
                  full((4096, 256)), full((4096, 256)), full((NC_PAD, 256)), pl.BlockSpec(memory_space=pl.ANY)],
        out_specs=[pl.BlockSpec((1, 2, 8, 128), lambda b, pt: (b, 0, 0, 0)),
                   pl.BlockSpec((1, 2 * SEL_TOPK, 128), lambda b, pt: (b, 0, 0))],
        scratch_shapes=[pltpu.VMEM((2, n_pages, 2, 2, HEAD_DIM, PAGE_SIZE), f32),
                        pltpu.VMEM((n, CMP_STRIDE * 256), bf16), pltpu.VMEM((n + 16, 256), f32),
                        pltpu.VMEM((NC_PAD, 256), f32), pltpu.VMEM((2, 8, 256), f32),
                        pltpu.SemaphoreType.DMA((2,))],
    )
    return pl.pallas_call(
        functools.partial(_cmp_sel_sample_kernel, layer=layer, n_pages=n_pages, n_batch=n_batch,
                          past_len=past_len),
        grid_spec=grid_spec,
        out_shape=[jax.ShapeDtypeStruct((n_batch, 2, 8, 128), f32),
                   jax.ShapeDtypeStruct((n_batch, 2 * SEL_TOPK, 128), jnp.int32)],
        compiler_params=_cp(("arbitrary",)),
        name="cmp_sel_sample",
    )(page_table, new_rows, qraw, pe1, pe2, w1, w2, cov, pool_t)


def _rows8(row_chunks):
    rid = _row((8, 128))
    out = jnp.zeros((8, 128), f32)
    for r, c in enumerate(row_chunks):
        out = jnp.where(rid == r, jnp.broadcast_to(c, (8, 128)), out)
    return out


def _nsa_q8(qrow, h, half):
    chunks = []
    for g in range(4):
        hd = h * 4 + g
        chunks.append(qrow[:, (hd // 2) * 128:(hd // 2 + 1) * 128])
    q8 = _rows8(chunks)
    sw = pltpu.roll(q8, 64, 1)
    in_place = (_row((8, 128)) % 2) == half
    q8 = jnp.where(in_place, q8, sw)
    return jnp.where((_lane((8, 128)) // 64) == half, q8, 0.0)


def _nsa_sel_sample_kernel(pt_ref, idx_ref, qrot_ref, gate_ref, ocmp_ref, snew_ref, wnew_ref, win_ref,
                           pool_ref, out_ref, nwin_ref, buf_ref, sem_ref, *, layer, n_batch, n_past_blocks):
    b = pl.program_id(0)
    slot = b % 2

    def copy(bb, sl, h, s, kv):
        j = jnp.minimum(idx_ref[bb, h * SEL_TOPK + s], n_past_blocks - 1)
        page = pt_ref[bb, j // (PAGE_SIZE // SEL_BLOCK)]
        return pltpu.make_async_copy(pool_ref.at[layer, page, kv], buf_ref.at[sl, h, kv, s], sem_ref.at[sl])

    def for_all(bb, sl, fn):
        for h in range(NSA_KV_HEADS):
            for s in range(SEL_TOPK):
                for kv in range(2):
                    fn(copy(bb, sl, h, s, kv))

    @pl.when(b == 0)
    def _():
        for_all(0, 0, lambda c: c.start())

    @pl.when(b + 1 < n_batch)
    def _():
        for_all(b + 1, 1 - slot, lambda c: c.start())

    gs = _sigmoid(gate_ref[0])
    lane128 = _lane((1, 128))
    qrot = qrot_ref[0]

    for_all(b, slot, lambda c: c.wait())

    snew = snew_ref[0]
    wnew = wnew_ref[0]
    o_all = []
    for h in range(NSA_KV_HEADS):
        qo = _nsa_q8(qrot, h, h) * SCALE
        qob = qo.astype(bf16)
        halfmask = (lane128 // 64) == h
        scores, valids = [], []
        has_new = False
        for s in range(SEL_TOPK):
            j = idx_ref[b, h * SEL_TOPK + s]
            kt = buf_ref[slot, h, 0, s].reshape(128, 128).astype(bf16)
            valid = ((lane128 // SEL_BLOCK) == (j % (PAGE_SIZE // SEL_BLOCK))) & (j < n_past_blocks)
            scores.append(jnp.where(valid, _dot(qob, kt), NEG_INF))
            valids.append(valid)
            has_new = jnp.logical_or(has_new, j == n_past_blocks)
        s_new = jnp.sum(qo * snew[:, 0:128], axis=-1, keepdims=True)
        s_new = jnp.where(has_new, s_new, NEG_INF)
        smax = scores[0]
        for sc in scores[1:]:
            smax = jnp.maximum(smax, sc)
        mx = jnp.maximum(jnp.max(smax, axis=-1, keepdims=True), s_new)
        e_new = jnp.where(has_new, jnp.exp(s_new - mx), 0.0)
        esum = jnp.zeros((8, 128), f32)
        acc = jnp.zeros((8, 128), f32)
        for s in range(SEL_TOPK):
            e = jnp.where(valids[s], jnp.exp(scores[s] - mx), 0.0)
            esum = esum + e
            vt = buf_ref[slot, h, 1, s].reshape(128, 128).astype(bf16)
            acc = acc + _dot_nt(e.astype(bf16), vt)
        inv = 1.0 / jnp.maximum(jnp.sum(esum, axis=-1, keepdims=True) + e_new, 1e-30)
        o_slc = (acc + e_new * snew[:, 128:256]) * inv
        wt = win_ref[0, 0]
        kt = wt[0].reshape(128, WINDOW).astype(bf16)
        vt = wt[1].reshape(128, WINDOW).astype(bf16)
        maskw = _lane((1, WINDOW)) >= 1
        s_w = jnp.where(maskw, _dot(qob, kt), NEG_INF)
        sw_new = jnp.sum(qo * wnew[:, 0:128], axis=-1, keepdims=True)
        mx = jnp.maximum(jnp.max(s_w, axis=-1, keepdims=True), sw_new)
        e = jnp.where(maskw, jnp.exp(s_w - mx), 0.0)
        e_new = jnp.exp(sw_new - mx)
        inv = 1.0 / jnp.maximum(jnp.sum(e, axis=-1, keepdims=True) + e_new, 1e-30)
        o_win = (_dot_nt(e.astype(bf16), vt) + e_new * wnew[:, 128:256]) * inv
        gate_rows = []
        for c in range(3):
            gate_rows.append(_rows8([jnp.broadcast_to(gs[:, (h * 4 + g) * 3 + c:(h * 4 + g) * 3 + c + 1], (1, 128))
                                     for g in range(4)]))
        o8 = gate_rows[0] * ocmp_ref[0, h] + gate_rows[1] * o_slc + gate_rows[2] * o_win
        o_all.append(jnp.where(halfmask, o8, 0.0))

    lo = lane128 < 64
    for h in range(NSA_KV_HEADS):
        o8 = o_all[h]
        o8s = pltpu.roll(o8, 64, 1)
        low_src, high_src = (o8, o8s) if h == 0 else (o8s, o8)
        for gp in range(2):
            ch = jnp.where(lo, low_src[2 * gp:2 * gp + 1], high_src[2 * gp + 1:2 * gp + 2])
            out_ref[0, :, h * 256 + gp * 128:h * 256 + (gp + 1) * 128] = ch

    last = _lane((1, WINDOW)) == WINDOW - 1
    eye64 = _row((64, 64)) == _lane((64, 64))
    for kv in range(2):
        for h in range(NSA_KV_HEADS):
            c = kv * 2 + h
            newc = jnp.broadcast_to(wnew[:, c * 64:(c + 1) * 64], (64, 64))
            colv = jnp.sum(jnp.where(eye64, newc, 0.0), axis=1, keepdims=True)
            old = win_ref[0, 0, kv, h]
            nwin_ref[0, kv, h] = jnp.where(last, colv, pltpu.roll(old, WINDOW - 1, 1))


def _nsa_sel_sample(page_table, sel_idx, qrot, gates, ocmp, snew, wnew, win_t, pool_t, layer, past_len):
    n_batch = page_table.shape[0]
    row = lambda w: pl.BlockSpec((1, 1, w), lambda b, pt, ix: (b, 0, 0))
    grid_spec = pltpu.PrefetchScalarGridSpec(
        num_scalar_prefetch=2,
        grid=(n_batch,),
        in_specs=[row(512), row(128), pl.BlockSpec((1, 2, 8, 128), lambda b, pt, ix: (b, 0, 0, 0)),
                  row(256), row(256),
                  pl.BlockSpec((1, 1, 2, 2, HEAD_DIM, WINDOW), lambda b, pt, ix: (layer, b, 0, 0, 0, 0)),
                  pl.BlockSpec(memory_space=pl.ANY)],
        out_specs=[row(512), pl.BlockSpec((1, 2, 2, HEAD_DIM, WINDOW), lambda b, pt, ix: (b, 0, 0, 0, 0))],
        scratch_shapes=[pltpu.VMEM((2, NSA_KV_HEADS, 2, SEL_TOPK, 2, HEAD_DIM, PAGE_SIZE), f32),
                        pltpu.SemaphoreType.DMA((2,))],
    )
    return pl.pallas_call(
        functools.partial(_nsa_sel_sample_kernel, layer=layer, n_batch=n_batch,
                          n_past_blocks=past_len // SEL_BLOCK),
        grid_spec=grid_spec,
        out_shape=[jax.ShapeDtypeStruct((n_batch, 1, 512), f32),
                   jax.ShapeDtypeStruct((n_batch, 2, 2, HEAD_DIM, WINDOW), f32)],
        compiler_params=_cp(("arbitrary",)),
        name="nsa_sel_sample",
    )(page_table, sel_idx, qrot, gates, ocmp, snew, wnew, win_t, pool_t)


def _diff_sample_kernel(pt_ref, q_ref, new_ref, dl_ref, pool_ref, out_ref, buf_ref, m_ref, l_ref,
                        acc_ref, sem_ref, *, layer, n_pages, n_batch, n_split, lam_init):
    b = pl.program_id(0)
    hf = pl.program_id(1)
    step = b * n_split + hf
    slot = step % 2
    pps = n_pages // n_split
    rows = pps * PAGE_SIZE

    def copy(bb, hh, sl, p):
        return pltpu.make_async_copy(pool_ref.at[layer, pt_ref[bb, hh * pps + p]],
                                     buf_ref.at[sl, pl.ds(p * 4 * PAGE_SIZE, 4 * PAGE_SIZE), :],
                                     sem_ref.at[sl])

    def fetch(bb, hh, sl):
        for p in range(pps):
            copy(bb, hh, sl, p).start()

    @pl.when(step == 0)
    def _():
        fetch(0, 0, 0)

    @pl.when(step + 1 < n_batch * n_split)
    def _():
        nxt = step + 1
        fetch(nxt // n_split, nxt % n_split, 1 - slot)

    @pl.when(hf == 0)
    def _():
        m_ref[...] = jnp.full(m_ref.shape, NEG_INF, f32)
        l_ref[...] = jnp.zeros(l_ref.shape, f32)
        acc_ref[...] = jnp.zeros(acc_ref.shape, f32)

    for p in range(pps):
        copy(b, hf, slot, p).wait()

    qrow = q_ref[0]
    lane = _lane((8, 128))
    rid = _row((8, 128))
    q8s = []
    for h in range(2):
        q8 = _rows8([qrow[:, h * 256 + (r // 2) * 128:h * 256 + (r // 2 + 1) * 128] for r in range(4)])
        q8s.append(jnp.where((lane // 64) == (rid % 2), q8, 0.0) * SCALE)
    for h in range(2):
        k = buf_ref[slot, pl.ds(h, rows, stride=4), :].astype(bf16)
        v = buf_ref[slot, pl.ds(2 + h, rows, stride=4), :].astype(bf16)
        s = _dot_nt(q8s[h].astype(bf16), k)
        m_old = m_ref[h]
        m_new = jnp.maximum(m_old, jnp.max(s, axis=-1, keepdims=True))
        alpha = jnp.exp(m_old - m_new)
        p_ = jnp.exp(s - m_new)
        l_ref[h] = alpha * l_ref[h] + jnp.sum(p_, axis=-1, keepdims=True)
        acc_ref[h] = alpha * acc_ref[h] + _dot(p_.astype(bf16), v)
        m_ref[h] = m_new

    @pl.when(hf == n_split - 1)
    def _():
        lam = _diff_lambda(dl_ref[...], lam_init)
        new = new_ref[0]
        for h in range(2):
            s_new = jnp.sum(q8s[h] * new[:, h * 128:(h + 1) * 128], axis=-1, keepdims=True)
            m_old = m_ref[h]
            m_new = jnp.maximum(m_old, s_new)
            alpha = jnp.exp(m_old - m_new)
            p_new = jnp.exp(s_new - m_new)
            l_ = alpha * l_ref[h] + p_new
            acc = alpha * acc_ref[h] + p_new * new[:, 256 + h * 128:256 + (h + 1) * 128]
            o = acc * (1.0 / jnp.maximum(l_, 1e-30))
            for g in range(2):
                og = o[2 * g:2 * g + 1] - lam * o[2 * g + 1:2 * g + 2]
                out_ref[0, :, (h * 2 + g) * 128:(h * 2 + g + 1) * 128] = _rms_unit(og) * (1.0 - lam_init)


def _diff_sample(page_table, dqrot, new_rows, dl, pool, layer, lam_init):
    n_batch, n_pages = page_table.shape
    n_split = 2
    pps = n_pages // n_split
    row = lambda w: pl.BlockSpec((1, 1, w), lambda b, s, pt: (b, 0, 0))
    grid_spec = pltpu.PrefetchScalarGridSpec(
        num_scalar_prefetch=1,
        grid=(n_batch, n_split),
        in_specs=[row(512), row(512), pl.BlockSpec((4, 64), lambda b, s, pt: (0, 0)),
                  pl.BlockSpec(memory_space=pl.ANY)],
        out_specs=row(512),
        scratch_shapes=[pltpu.VMEM((2, pps * 4 * PAGE_SIZE, 128), f32), pltpu.VMEM((2, 8, 1), f32),
                        pltpu.VMEM((2, 8, 1), f32), pltpu.VMEM((2, 8, 128), f32),
                        pltpu.SemaphoreType.DMA((2,))],
    )
    return pl.pallas_call(
        functools.partial(_diff_sample_kernel, layer=layer, n_pages=n_pages, n_batch=n_batch,
                          n_split=n_split, lam_init=lam_init),
        grid_spec=grid_spec,
        out_shape=jax.ShapeDtypeStruct((n_batch, 1, 512), f32),
        compiler_params=_cp(("arbitrary", "arbitrary")),
        name="diff_sample",
    )(page_table, dqrot, new_rows, dl, pool)


def _prep_w_in(w):
    parts = jnp.split(w, [sum(IN_SPLITS[:i + 1]) for i in range(len(IN_SPLITS) - 1)], axis=-1)
    nq, ncmp, nslc, nwin, ngate, scb, scc, sch, dq, dk, dv, s5u = parts
    gate = jnp.pad(ngate, ((0, 0), (0, HC - C_GATE - ngate.shape[1])))
    wcat = jnp.concatenate([nq, scb, scc, sch, dq, s5u, ncmp, nslc, nwin, dk, dv, gate], axis=-1).astype(bf16)
    return jnp.transpose(wcat.reshape(D_MODEL, HC // IN_TN, IN_TN), (1, 0, 2))


def _prep_phi(pe, w):
    w2 = w.reshape(2, 2, CMP_STRIDE, HEAD_DIM, HEAD_DIM)
    wc = jnp.repeat(w2, 2, axis=0)
    eye = jnp.eye(4, dtype=f32)
    ws, pes = [], []
    for half in range(2):
        ws.append(jnp.einsum('csde,cf->scdfe', wc[:, half], eye).reshape(4096, 256).astype(bf16))
        pc = jnp.repeat(pe[:, half * CMP_STRIDE:(half + 1) * CMP_STRIDE], 2, axis=0)
        pes.append(jnp.transpose(pc, (1, 0, 2)).reshape(1, 4096))
    return pes[0], pes[1], ws[0], ws[1]


def _prep_s5(bbr, bbi, c_re, c_im):
    eye = jnp.eye(8, dtype=f32)

    def wb_of(bb):
        x = bb.reshape(S5_CH, S5_SLABS, 8, S5_STATE)
        return jnp.einsum('csgp,hg->shcgp', x, eye).reshape(S5_SLABS, 128, 512)

    wb = jnp.concatenate([wb_of(bbr), wb_of(bbi)], axis=-1).astype(bf16)

    def wc_of(c):
        x = c.reshape(S5_SLABS, 8, S5_CH, S5_STATE)
        return jnp.einsum('sgcp,hg->shpgc', x, eye).reshape(S5_SLABS, 512, 128)

    wc = jnp.concatenate([wc_of(c_re), -wc_of(c_im)], axis=1).astype(bf16)
    return wb, wc


def _rope_tables(pos):
    half = HEAD_DIM // 2
    inv = ROPE_THETA ** (-jnp.arange(half, dtype=f32) / half)
    ang = pos.astype(f32)[:, None] * inv[None, :]
    c, s = jnp.cos(ang), jnp.sin(ang)
    return jnp.tile(c, (1, 4)), jnp.tile(jnp.concatenate([-s, s], axis=1), (1, 2))


def kernel(x_prompt, x_sample, cache_nsa_cmp, cache_nsa_slc, cache_diff, state_nsa_win, state_sconv, state_s5_re, state_s5_im, state_ffn_conv, page_table, w_in, nsa_phi_pe, nsa_phi_w, sc_conv_w, diff_lambda, s5_a_re, s5_a_im, s5_log_dt, s5_b_re, s5_b_im, s5_c_re, s5_c_im, s5_d, s5_glu_w, s5_glu_b, mix_gain, w_out, ln1_g, ln1_b, ffn_w_up, ffn_conv_w, ffn_w_down, ln2_g, ln2_b):
    bp, t_len, _ = x_prompt.shape
    bs = x_sample.shape[0]
    n_pool = cache_nsa_cmp.shape[1]
    n_pages = page_table.shape[1]
    past_len = n_pages * PAGE_SIZE
    n_state = S5_GROUPS * S5_STATE
    mp = bp * t_len

    cos_p, sin_p = _rope_tables(jnp.arange(t_len))
    cos_s, sin_s = _rope_tables(jnp.full((bs,), past_len))

    pool_cmp = jnp.transpose(cache_nsa_cmp, (0, 1, 3, 4, 5, 2))
    pool_slc = jnp.transpose(cache_nsa_slc, (0, 1, 3, 4, 5, 2))
    pool_diff = cache_diff.reshape(DEPTH, n_pool, PAGE_SIZE * 4, 128)
    win_t = jnp.transpose(state_nsa_win, (0, 1, 3, 4, 5, 2))

    xp = x_prompt.reshape(mp, D_MODEL)
    xs = x_sample.reshape(bs, D_MODEL)
    xp_b, xs_b = xp.astype(bf16), xs.astype(bf16)

    outs_p = {k: [] for k in ('cmp', 'slc', 'win', 'diff', 'sc', 's5r', 's5i', 'ffn')}
    outs_s = {k: [] for k in ('cmp', 'slc', 'win', 'diff', 'sc', 's5r', 's5i', 'ffn')}

    gw = s5_glu_w.astype(bf16)
    wo = w_out.astype(bf16)
    wup = ffn_w_up.astype(bf16)
    wdn = ffn_w_down.astype(bf16)

    for l in range(DEPTH):
        lam_init = 0.8 - 0.6 * math.exp(-0.3 * l)
        w_in_l = _prep_w_in(w_in[l])
        pe1, pe2, w1, w2 = _prep_phi(nsa_phi_pe[l], nsa_phi_w[l])
        tabs, bbr, bbi = _s5_prep(s5_a_re[l].reshape(1, n_state), s5_a_im[l].reshape(1, n_state),
                                  jnp.repeat(s5_log_dt[l], S5_STATE).reshape(1, n_state),
                                  jnp.transpose(s5_b_re[l], (2, 0, 1)).reshape(S5_CH, n_state),
                                  jnp.transpose(s5_b_im[l], (2, 0, 1)).reshape(S5_CH, n_state))
        wb5, wc5 = _prep_s5(bbr, bbi, s5_c_re[l], s5_c_im[l])
        d5 = s5_d[l].reshape(1, 512)
        gb = s5_glu_b[l].reshape(1, 512)
        gain = mix_gain[l].reshape(1, D_MODEL)
        g1, b1 = ln1_g[l].reshape(1, D_MODEL), ln1_b[l].reshape(1, D_MODEL)
        g2, b2 = ln2_g[l].reshape(1, D_MODEL), ln2_b[l].reshape(1, D_MODEL)
        cwf = ffn_conv_w[l]
        scw = sc_conv_w[l]
        dl = diff_lambda[l]

        hcat = _in_proj(xp_b, w_in_l, 1024)
        qrot, dqrot, kvslc, kvwin, kvdiff = _rope(hcat, cos_p, sin_p, 512)
        kvcmp = hcat[:, C_CMP:C_CMP + 256]
        kvc = _cmp_prompt(kvcmp.reshape(mp // CMP_STRIDE, CMP_STRIDE * 256), pe1, pe2, w1, w2, bp)
        nsa = _nsa_prompt(hcat, qrot, kvc, kvslc, kvwin, bp, t_len)
        dif = _diff_prompt(dqrot, kvdiff, dl, bp, t_len, lam_init)
        sc, sc_tail = _sconv_prompt(hcat, scw, bp, t_len)
        s5y, s5r, s5i = _s5_scan(hcat, wb5, wc5, d5, tabs, bp, t_len)
        x1, x1b = _mix(nsa, sc, dif, s5y, gw, gb, gain, wo, xp, g1, b1, 256, l)
        xp, xp_b, ffn_tail = _ffn_prompt(x1b, x1, wup, wdn, cwf, g2, b2, t_len, l)

        outs_p['cmp'].append(kvcmp.reshape(bp, t_len, 2, 2, HEAD_DIM))
        outs_p['slc'].append(kvslc.reshape(bp, t_len, 2, 2, HEAD_DIM))
        outs_p['win'].append(kvwin.reshape(bp, t_len, 2, 2, HEAD_DIM)[:, t_len - WINDOW:])
        outs_p['diff'].append(kvdiff.reshape(bp, t_len, 2, 2, 2 * HEAD_DIM))
        outs_p['sc'].append(sc_tail[:, 6:8])
        outs_p['s5r'].append(s5r.reshape(bp, S5_GROUPS, S5_STATE))
        outs_p['s5i'].append(s5i.reshape(bp, S5_GROUPS, S5_STATE))
        tiles_per_seq = ffn_tail.shape[0] // bp
        outs_p['ffn'].append(ffn_tail.reshape(bp, tiles_per_seq, 8, D_FF)[:, -1, 6:8])

        hs = _in_proj(xs_b, w_in_l, bs)
        qrot_s, dqrot_s, kvslc_s, kvwin_s, kvdiff_s = _rope(hs, cos_s, sin_s, bs)
        kvcmp_s = hs[:, C_CMP:C_CMP + 256]
        ocmp_s, sel_s = _cmp_sel_sample(page_table, kvcmp_s.reshape(bs, 1, 256),
                                        hs[:, C_NQ:C_NQ + 512].reshape(bs, 1, 512), pe1, pe2, w1, w2,
                                        pool_cmp, l, past_len)
        nsa_s, nwin_t = _nsa_sel_sample(page_table, sel_s[:, :, 0], qrot_s.reshape(bs, 1, 512),
                                        hs[:, C_GATE:C_GATE + 128].reshape(bs, 1, 128), ocmp_s,
                                        kvslc_s.reshape(bs, 1, 256), kvwin_s.reshape(bs, 1, 256),
                                        win_t, pool_slc, l, past_len)
        dif_s = _diff_sample(page_table, dqrot_s.reshape(bs, 1, 512), kvdiff_s.reshape(bs, 1, 512),
                             dl, pool_diff, l, lam_init)
        scp = jnp.transpose(state_sconv[l], (1, 0, 2))
        sc_s, z_s, s5y_s, s5r_s, s5i_s = _sample_small(
            hs[:, C_SCB:C_SCB + 512], hs[:, C_SCC:C_SCC + 512], hs[:, C_SCH:C_SCH + 512], scw, scp,
            hs[:, C_S5U:C_S5U + 512], wb5, wc5, d5, tabs,
            state_s5_re[l].reshape(bs, n_state), state_s5_im[l].reshape(bs, n_state))
        x1s, x1sb = _mix(nsa_s.reshape(bs, 512), sc_s, dif_s.reshape(bs, 512), s5y_s, gw, gb, gain, wo,
                         xs, g1, b1, bs, l)
        prev_ffn = state_ffn_conv[l]
        xs, xs_b, aup_s = _ffn_sample(x1sb, x1s, prev_ffn[:, 0], prev_ffn[:, 1], wup, wdn, cwf, g2, b2, l)

        outs_s['cmp'].append(kvcmp_s.reshape(bs, 1, 2, 2, HEAD_DIM))
        outs_s['slc'].append(kvslc_s.reshape(bs, 1, 2, 2, HEAD_DIM))
        outs_s['win'].append(jnp.transpose(nwin_t, (0, 4, 1, 2, 3)))
        outs_s['diff'].append(kvdiff_s.reshape(bs, 1, 2, 2, 2 * HEAD_DIM))
        outs_s['sc'].append(jnp.stack([state_sconv[l][:, 1], z_s], axis=1))
        outs_s['s5r'].append(s5r_s.reshape(bs, S5_GROUPS, S5_STATE))
        outs_s['s5i'].append(s5i_s.reshape(bs, S5_GROUPS, S5_STATE))
        outs_s['ffn'].append(jnp.stack([prev_ffn[:, 1], aup_s], axis=1))

    order = ('cmp', 'slc', 'win', 'diff', 'sc', 's5r', 's5i', 'ffn')
    res = [xp.reshape(bp, t_len, D_MODEL), xs.reshape(bs, 1, D_MODEL)]
    res += [jnp.stack(outs_p[k], axis=0) for k in order]
    res += [jnp.stack(outs_s[k], axis=0) for k in order]
    return tuple(res)
```
